```python
import math
import jax, jax.numpy as jnp
from jax import lax
import numpy as np

D_MODEL = 2048
BATCH = 1
SEQ = 16384
DEPTH = 2

N_MIXERS = 2
HEAD_DIM = 128
N_HEADS = D_MODEL // HEAD_DIM
DILATED_CONFIGS = ((128, 1), (512, 4), (2048, 16))
N_GROUPS_A = len(DILATED_CONFIGS)
BLOCK = 128
D_FF = ((8 * D_MODEL // 3 + 127) // 128) * 128
CONV_WIDTH = 3
EPS = 1e-6
N_MOD = 6
N_LAYERS_A = (DEPTH + 1) // 2
N_LAYERS_B = DEPTH // 2
NEG_BIG = -1e30

kernel_name = 'hybrid_dilated_stickbreaking_convffn_adaln'


def rms_norm(x, g):
    xf = x.astype(jnp.float32)
    y = xf * lax.rsqrt(jnp.mean(xf * xf, axis=-1, keepdims=True) + EPS)
    return (y * g.astype(jnp.float32)).astype(x.dtype)


def alibi_slopes(n):
    return jnp.asarray(np.array([2.0 ** (-8.0 * (i + 1) / n) for i in range(n)], dtype=np.float32))


def dilated_window_branch(q, k, v, window, dilation, slopes):
    b, s, h, dh = q.shape
    span = window // dilation
    unit = dilation * BLOCK
    s_pad = -(-s // unit) * unit
    pad = ((0, 0), (0, s_pad - s), (0, 0), (0, 0))

    def to_blocks(t):
        t = jnp.pad(t, pad).reshape(b, s_pad // dilation, dilation, h, dh).transpose(0, 2, 1, 3, 4)
        return t.reshape(b, dilation, -1, BLOCK, h, dh)

    def with_prev(t):
        prev = jnp.pad(t[:, :, :-1], ((0, 0), (0, 0), (1, 0), (0, 0), (0, 0), (0, 0)))
        return jnp.concatenate([prev, t], axis=3)

    qb = to_blocks(q)
    kw = with_prev(to_blocks(k))
    vw = with_prev(to_blocks(v))
    nb = qb.shape[2]
    scores = jnp.einsum('brnqhd,brnkhd->brnhqk', qb, kw).astype(jnp.float32) * (dh ** -0.5)
    qi = jnp.arange(BLOCK)[:, None]
    ki = jnp.arange(2 * BLOCK)[None, :]
    j = qi + BLOCK - ki
    key_sub = jnp.arange(nb)[:, None, None] * BLOCK + ki[None] - BLOCK
    mask = ((j >= 0) & (j <= span))[None] & (key_sub >= 0)
    bias = -slopes[:, None, None] * (j * dilation).astype(jnp.float32)[None]
    scores = jnp.where(mask[None, None, :, None], scores + bias[None, None, None], NEG_BIG)
    lse = jax.nn.logsumexp(scores, axis=-1)
    p = jnp.exp(scores - lse[..., None])
    out = jnp.einsum('brnhqk,brnkhd->brnqhd', p.astype(v.dtype), vw)
    out = out.reshape(b, dilation, s_pad // dilation, h, dh).transpose(0, 2, 1, 3, 4)
    out = out.reshape(b, s_pad, h, dh)[:, :s]
    lse = lse.transpose(0, 1, 2, 4, 3).reshape(b, dilation, s_pad // dilation, h)
    lse = lse.transpose(0, 2, 1, 3).reshape(b, s_pad, h)[:, :s]
    return out, lse


def dilated_attention_mixer(h, w_in, w_out):
    b, s, _ = h.shape
    qkv = (h @ w_in).reshape(b, s, N_GROUPS_A, 3, N_HEADS, HEAD_DIM)
    slopes = alibi_slopes(N_HEADS)
    outs, lses = [], []
    for g, (window, dilation) in enumerate(DILATED_CONFIGS):
        o, l = dilated_window_branch(qkv[:, :, g, 0], qkv[:, :, g, 1], qkv[:, :, g, 2],
                                     window, dilation, slopes)
        outs.append(o)
        lses.append(l)
    wts = jax.nn.softmax(jnp.stack(lses, axis=0), axis=0)
    o = jnp.einsum('gbsh,gbshd->bshd', wts.astype(h.dtype), jnp.stack(outs, axis=0))
    return o.reshape(b, s, N_HEADS * HEAD_DIM) @ w_out


def stick_breaking_mixer(h, w_in, w_out):
    b, s, _ = h.shape
    qkv = (h @ w_in).reshape(b, s, 3, N_HEADS, HEAD_DIM)
    q, k, v = qkv[:, :, 0], qkv[:, :, 1], qkv[:, :, 2]
    nb = s // BLOCK
    qb = q.reshape(b, nb, BLOCK, N_HEADS, HEAD_DIM).transpose(1, 0, 2, 3, 4)
    key_pos = jnp.arange(s)
    scale = HEAD_DIM ** -0.5

    def block(args):
        q_blk, blk = args
        z = jnp.einsum('bqhd,bkhd->bhqk', q_blk, k).astype(jnp.float32) * scale
        q_pos = blk * BLOCK + jnp.arange(BLOCK)
        causal = key_pos[None, :] < q_pos[:, None]
        log_beta = jax.nn.log_sigmoid(z)
        log_keep = jnp.where(causal, jax.nn.log_sigmoid(-z), 0.0)
        after = lax.cumsum(log_keep, axis=log_keep.ndim - 1, reverse=True) - log_keep
        a = jnp.where(causal, jnp.exp(log_beta + after), 0.0)
        return jnp.einsum('bhqk,bkhd->bqhd', a.astype(v.dtype), v)

    o = lax.map(block, (qb, jnp.arange(nb)))
    o = o.transpose(1, 0, 2, 3, 4).reshape(b, s, N_HEADS * HEAD_DIM)
    return o @ w_out


def conv_ffn(h, w_up, conv_w, conv_b, w_down):
    u = h @ w_up
    s = u.shape[1]
    up = jnp.pad(u, ((0, 0), (CONV_WIDTH - 1, 0), (0, 0)))
    y = conv_b
    for i in range(CONV_WIDTH):
        y = y + conv_w[i] * up[:, i:i + s]
    a, g = jnp.split(y, 2, axis=-1)
    return (jax.nn.silu(g) * a) @ w_down


def setup_inputs(seed: int = 0) -> dict:
    key = jax.random.key(seed)
    ks = jax.random.split(key, 16)
    f32 = jnp.float32
    d = D_MODEL
    sd = d ** -0.5
    return {
        'x': jax.random.normal(ks[0], (BATCH, SEQ, d), f32),
        'c': jax.random.normal(ks[1], (BATCH, d), f32),
        'norm_mix_g': 1.0 + 0.05 * jax.random.normal(ks[2], (DEPTH, d), f32),
        'norm_ffn_g': 1.0 + 0.05 * jax.random.normal(ks[3], (DEPTH, d), f32),
        'ada_w': 0.5 * sd * jax.random.normal(ks[4], (DEPTH, d, N_MOD * d), f32),
        'ada_b': 0.01 * jax.random.normal(ks[5], (DEPTH, N_MOD * d), f32),
        'w_in_a': sd * jax.random.normal(ks[6], (N_LAYERS_A, d, N_GROUPS_A * 3 * N_HEADS * HEAD_DIM), f32),
        'w_out_a': (N_HEADS * HEAD_DIM) ** -0.5 * jax.random.normal(ks[7], (N_LAYERS_A, N_HEADS * HEAD_DIM, d), f32),
        'w_in_b': sd * jax.random.normal(ks[8], (N_LAYERS_B, d, 3 * N_HEADS * HEAD_DIM), f32),
        'w_out_b': (N_HEADS * HEAD_DIM) ** -0.5 * jax.random.normal(ks[9], (N_LAYERS_B, N_HEADS * HEAD_DIM, d), f32),
        'w_up': sd * jax.random.normal(ks[10], (DEPTH, d, 2 * D_FF), f32),
        'conv_w': CONV_WIDTH ** -0.5 * jax.random.normal(ks[11], (DEPTH, CONV_WIDTH, 2 * D_FF), f32),
        'conv_b': 0.01 * jax.random.normal(ks[12], (DEPTH, 2 * D_FF), f32),
        'w_down': D_FF ** -0.5 * jax.random.normal(ks[13], (DEPTH, D_FF, d), f32),
        'final_g': 1.0 + 0.05 * jax.random.normal(ks[14], (d,), f32),
    }


def reference(x, c, norm_mix_g, norm_ffn_g, ada_w, ada_b, w_in_a, w_out_a, w_in_b, w_out_b,
              w_up, conv_w, conv_b, w_down, final_g):
    for i in range(DEPTH):
        mod = (c @ ada_w[i] + ada_b[i])[:, None, :]
        sh1, sc1, g1, sh2, sc2, g2 = jnp.split(mod, N_MOD, axis=-1)
        h = rms_norm(x, norm_mix_g[i]) * (1 + sc1) + sh1
        if i % N_MIXERS == 0:
            y = dilated_attention_mixer(h, w_in_a[i // N_MIXERS], w_out_a[i // N_MIXERS])
        else:
            y = stick_breaking_mixer(h, w_in_b[i // N_MIXERS], w_out_b[i // N_MIXERS])
        x = x + g1 * y
        h = rms_norm(x, norm_ffn_g[i]) * (1 + sc2) + sh2
        x = x + g2 * conv_ffn(h, w_up[i], conv_w[i], conv_b[i], w_down[i])
    return rms_norm(x, final_g)
```

```python
import functools

import numpy as np
import jax
import jax.numpy as jnp
from jax import lax
from jax.experimental import pallas as pl
from jax.experimental.pallas import tpu as pltpu

HEAD_DIM = 128
DILATED_CONFIGS = ((128, 1), (512, 4), (2048, 16))
ATTN_BLOCK = 128
CONV_WIDTH = 3
EPS = 1e-6
N_MOD = 6
NEG_BIG = -1e30

LANES = 128
SUBLANES = 8
V7X_VMEM_BYTES = 64 * 1024 * 1024
VMEM_HEADROOM_BYTES = 6 * 1024 * 1024

F32_EXP_UNDERFLOW = 104.0

F32 = jnp.float32
BF16 = jnp.bfloat16


def _params(semantics, vmem_estimate_bytes):
    limit = min(int(vmem_estimate_bytes) + VMEM_HEADROOM_BYTES, V7X_VMEM_BYTES - VMEM_HEADROOM_BYTES)
    return pltpu.CompilerParams(dimension_semantics=semantics, vmem_limit_bytes=limit)


def _modulated_rms_norm(x, g, sc, sh):
    y = x * lax.rsqrt(jnp.mean(x * x, axis=-1, keepdims=True) + EPS)
    return (y * g) * (1.0 + sc) + sh


ADA_TN = 1024
ADA_ROWS = 256


def _ada_kernel(c_ref, w_ref, b_ref, o_ref):
    d = w_ref.shape[1]
    tn = w_ref.shape[2]

    def body(k, acc):
        r = pl.multiple_of(k * ADA_ROWS, ADA_ROWS)
        prod = c_ref[pl.ds(r, ADA_ROWS), :] * w_ref[0, pl.ds(r, ADA_ROWS), :]
        return acc + prod.reshape(ADA_ROWS // SUBLANES, SUBLANES, tn).sum(axis=0)

    acc = lax.fori_loop(0, d // ADA_ROWS, body, jnp.zeros((SUBLANES, tn), F32))
    o_ref[0] = acc.sum(axis=0, keepdims=True) + b_ref[0]


def _ada_modulation(c, ada_w, ada_b):
    depth, d, n = ada_w.shape
    c_col = c.reshape(d, 1)
    return pl.pallas_call(
        _ada_kernel,
        name="ada_modulation",
        grid=(depth, n // ADA_TN),
        in_specs=[
            pl.BlockSpec((d, 1), lambda l, j: (0, 0)),
            pl.BlockSpec((1, d, ADA_TN), lambda l, j: (l, 0, j)),
            pl.BlockSpec((1, 1, ADA_TN), lambda l, j: (l, 0, j)),
        ],
        out_specs=pl.BlockSpec((1, 1, ADA_TN), lambda l, j: (l, 0, j)),
        out_shape=jax.ShapeDtypeStruct((depth, 1, n), F32),
        compiler_params=_params(("arbitrary", "arbitrary"),
                                2 * d * ADA_TN * 4 + d * LANES * 4),
    )(c_col, ada_w, ada_b.reshape(depth, 1, n))


QKV_TM = 1024
QKV_TN = 1024
NORM_ROWS = 64


def _qkv_kernel(x_ref, g_ref, sc_ref, sh_ref, w_ref, o_ref, h_ref, *, head_major):
    tm = x_ref.shape[0]

    @pl.when(pl.program_id(1) == 0)
    def _():
        def body(k, carry):
            r = pl.multiple_of(k * NORM_ROWS, NORM_ROWS)
            h = _modulated_rms_norm(x_ref[pl.ds(r, NORM_ROWS), :], g_ref[...], sc_ref[...], sh_ref[...])
            h_ref[pl.ds(r, NORM_ROWS), :] = h.astype(BF16)
            return carry
        lax.fori_loop(0, tm // NORM_ROWS, body, 0)

    res = jnp.dot(h_ref[...], w_ref[...], preferred_element_type=F32)
    if head_major:
        for cb in range(o_ref.shape[0]):
            o_ref[cb] = res[:, cb * HEAD_DIM:(cb + 1) * HEAD_DIM].astype(BF16)
    else:
        o_ref[...] = res.astype(BF16)


def _qkv_projection(x, g, sc, sh, w, *, head_major):
    s, d = x.shape
    n = w.shape[1]
    tm, tn = QKV_TM, QKV_TN
    vec = pl.BlockSpec((1, d), lambda i, j: (0, 0))
    if head_major:
        out_shape = jax.ShapeDtypeStruct((n // HEAD_DIM, s, HEAD_DIM), BF16)
        out_spec = pl.BlockSpec((tn // HEAD_DIM, tm, HEAD_DIM), lambda i, j: (j, i, 0))
    else:
        out_shape = jax.ShapeDtypeStruct((s, n), BF16)
        out_spec = pl.BlockSpec((tm, tn), lambda i, j: (i, j))
    vmem = 2 * tm * d * 4 + tm * d * 2 + 2 * d * tn * 2 + 2 * tm * tn * 2 + tm * tn * 4
    return pl.pallas_call(
        functools.partial(_qkv_kernel, head_major=head_major),
        name="qkv_projection",
        grid=(s // tm, n // tn),
        in_specs=[pl.BlockSpec((tm, d), lambda i, j: (i, 0)), vec, vec, vec,
                  pl.BlockSpec((d, tn), lambda i, j: (0, j))],
        out_specs=out_spec,
        out_shape=out_shape,
        scratch_shapes=[pltpu.VMEM((tm, d), BF16)],
        compiler_params=_params(("arbitrary", "arbitrary"), vmem),
    )(x, g, sc, sh, w)


def _alibi_slopes(n):
    return [float(np.float32(2.0 ** (-8.0 * (i + 1) / n))) for i in range(n)]


def _dilated_kernel(q_ref, kp_ref, kc_ref, vp_ref, vc_ref, o_ref, lse_ref, *, dilation, span, n_heads):
    blk = ATTN_BLOCK
    n = pl.program_id(1)
    qi = lax.broadcasted_iota(jnp.int32, (blk, blk), 0)
    ki = lax.broadcasted_iota(jnp.int32, (blk, blk), 1)
    j_prev = qi + blk - ki
    j_cur = qi - ki
    mask_prev = j_prev <= jnp.where(n > 0, span, 0)
    mask_cur = j_cur >= 0
    dist_prev = (j_prev * dilation).astype(F32)
    dist_cur = (j_cur * dilation).astype(F32)
    lane = lax.broadcasted_iota(jnp.int32, (blk, LANES), 1)
    scale = HEAD_DIM ** -0.5
    slopes = _alibi_slopes(n_heads)
    contract_last = (((1,), (1,)), ((), ()))

    lse_tile = jnp.zeros((blk, LANES), F32)
    for h in range(n_heads):
        cols = slice(h * HEAD_DIM, (h + 1) * HEAD_DIM)
        q = q_ref[:, cols]
        s_prev = lax.dot_general(q, kp_ref[:, cols], contract_last, preferred_element_type=F32)
        s_cur = lax.dot_general(q, kc_ref[:, cols], contract_last, preferred_element_type=F32)
        s_prev = jnp.where(mask_prev, s_prev * scale - slopes[h] * dist_prev, NEG_BIG)
        s_cur = jnp.where(mask_cur, s_cur * scale - slopes[h] * dist_cur, NEG_BIG)
        m = jnp.maximum(jnp.max(s_prev, axis=-1, keepdims=True), jnp.max(s_cur, axis=-1, keepdims=True))
        p_prev = jnp.exp(s_prev - m)
        p_cur = jnp.exp(s_cur - m)
        denom = jnp.sum(p_prev, axis=-1, keepdims=True) + jnp.sum(p_cur, axis=-1, keepdims=True)
        acc = jnp.dot(p_prev.astype(BF16), vp_ref[:, cols], preferred_element_type=F32)
        acc = acc + jnp.dot(p_cur.astype(BF16), vc_ref[:, cols], preferred_element_type=F32)
        o_ref[:, cols] = (acc / denom).astype(o_ref.dtype)
        lse_tile = jnp.where(lane == h, m + jnp.log(denom), lse_tile)
    lse_ref[...] = lse_tile


def _dilated_group_attention(qkv, group, *, n_groups, n_heads):
    window, dilation = DILATED_CONFIGS[group]
    s, n = qkv.shape
    dm = n_heads * HEAD_DIM
    blk = ATTN_BLOCK
    assert window % dilation == 0 and window // dilation <= blk
    assert s % (dilation * blk) == 0 and n_heads <= LANES
    nb = s // (dilation * blk)
    cols_per_pos = n // dm
    view = qkv.reshape(s // dilation, dilation * n)
    base = group * 3

    def spec(which, prev):
        def index_map(p, b):
            row = jnp.maximum(b - 1, 0) if prev else b
            return (row, p * cols_per_pos + base + which)
        return pl.BlockSpec((blk, dm), index_map)

    out, lse = pl.pallas_call(
        functools.partial(_dilated_kernel, dilation=dilation, span=window // dilation, n_heads=n_heads),
        name=f"dilated_attention_g{group}",
        grid=(dilation, nb),
        in_specs=[spec(0, False), spec(1, True), spec(1, False), spec(2, True), spec(2, False)],
        out_specs=[pl.BlockSpec((blk, dm), lambda p, b: (b, p)),
                   pl.BlockSpec((blk, LANES), lambda p, b: (b, p))],
        out_shape=[jax.ShapeDtypeStruct((s // dilation, dilation * dm), BF16),
                   jax.ShapeDtypeStruct((s // dilation, dilation * LANES), F32)],
        compiler_params=_params(("arbitrary", "arbitrary"), 2 * 6 * blk * dm * 2 + 2 * blk * LANES * 4),
    )(view, view, view, view, view)
    return out.reshape(s, dm), lse.reshape(s, LANES)


SB_TQ = 256


def _sb_kernel(q_ref, k_ref, v_ref, tri_ref, o_ref, acc_ref, c_ref):
    tq = q_ref.shape[1]
    near = 2 * tq
    i = pl.program_id(1)
    q = q_ref[0]
    scale = HEAD_DIM ** -0.5
    contract_last = (((1,), (1,)), ((), ()))

    def window(kstart, width, causal_mask):
        k = k_ref[0, pl.ds(kstart, width), :]
        v = v_ref[0, pl.ds(kstart, width), :]
        z = lax.dot_general(q, k, contract_last, preferred_element_type=F32) * scale
        log_beta = jnp.minimum(z, 0.0) - jnp.log(1.0 + jnp.exp(-jnp.abs(z)))
        log_keep = log_beta - z
        if causal_mask is not None:
            log_keep = jnp.where(causal_mask, log_keep, 0.0)
        hi = log_keep.astype(BF16)
        lo = (log_keep - hi.astype(F32)).astype(BF16)
        tri = tri_ref[0:width, 0:width]
        after = (jnp.dot(hi, tri, preferred_element_type=F32)
                 + jnp.dot(lo, tri, preferred_element_type=F32)) + c_ref[...]
        a = jnp.exp(log_beta + after)
        if causal_mask is not None:
            a = jnp.where(causal_mask, a, 0.0)
        acc_ref[...] += jnp.dot(a.astype(BF16), v, preferred_element_type=F32)
        c_ref[...] = after[:, 0:1] + log_keep[:, 0:1]

    acc_ref[...] = jnp.zeros_like(acc_ref)
    c_ref[...] = jnp.zeros_like(c_ref)

    near_start = pl.multiple_of(jnp.maximum(i - 1, 0) * tq, tq)
    q_pos = i * tq + lax.broadcasted_iota(jnp.int32, (tq, near), 0)
    k_pos = near_start + lax.broadcasted_iota(jnp.int32, (tq, near), 1)
    window(near_start, near, k_pos < q_pos)

    def cond(carry):
        b, c_max = carry
        return (b >= 0) & (c_max > -F32_EXP_UNDERFLOW)

    def body(carry):
        b, _ = carry
        window(pl.multiple_of(b * tq, tq), tq, None)
        return b - 1, jnp.max(c_ref[...])

    lax.while_loop(cond, body, (i - 2, jnp.max(c_ref[...])))
    o_ref[...] = acc_ref[...].astype(o_ref.dtype)


def _stick_breaking_attention(qkv, *, n_heads):
    _, s, _ = qkv.shape
    tq = SB_TQ
    assert s % tq == 0 and s >= 2 * tq
    tri = jnp.tril(jnp.ones((2 * tq, 2 * tq), F32), k=-1).astype(BF16)
    vmem = (2 * 2 * s * HEAD_DIM * 2 + 2 * (2 * tq) ** 2 * 2 + 4 * tq * HEAD_DIM * 2
            + 12 * tq * 2 * tq * 4)
    return pl.pallas_call(
        _sb_kernel,
        name="stick_breaking_attention",
        grid=(n_heads, s // tq),
        in_specs=[pl.BlockSpec((1, tq, HEAD_DIM), lambda h, i: (h, i, 0)),
                  pl.BlockSpec((1, s, HEAD_DIM), lambda h, i: (n_heads + h, 0, 0)),
                  pl.BlockSpec((1, s, HEAD_DIM), lambda h, i: (2 * n_heads + h, 0, 0)),
                  pl.BlockSpec((2 * tq, 2 * tq), lambda h, i: (0, 0))],
        out_specs=pl.BlockSpec((tq, HEAD_DIM), lambda h, i: (i, h)),
        out_shape=jax.ShapeDtypeStruct((s, n_heads * HEAD_DIM), BF16),
        scratch_shapes=[pltpu.VMEM((tq, HEAD_DIM), F32), pltpu.VMEM((tq, 1), F32)],
        compiler_params=_params(("arbitrary", "arbitrary"), vmem),
    )(qkv, qkv, qkv, tri)


OUT_TM = 512
MERGE_ROWS = 128


def _out_kernel(*refs, n_groups, n_heads):
    o_refs = refs[:n_groups]
    lse_refs = refs[n_groups:2 * n_groups] if n_groups > 1 else ()
    rest = refs[len(o_refs) + len(lse_refs):]
    w_ref, x_ref, gate_ref, g_ref, sc_ref, sh_ref, xo_ref, h_ref = rest[:8]
    tm = x_ref.shape[0]

    if n_groups == 1:
        mixed = o_refs[0][...]
    else:
        merged_ref = rest[8]

        def merge(k, carry):
            r = pl.multiple_of(k * MERGE_ROWS, MERGE_ROWS)
            rows = pl.ds(r, MERGE_ROWS)
            lses = [lr[rows, :] for lr in lse_refs]
            m = functools.reduce(jnp.maximum, lses)
            es = [jnp.exp(l - m) for l in lses]
            inv = 1.0 / functools.reduce(jnp.add, es)
            wts = [e * inv for e in es]
            for h in range(n_heads):
                cols = slice(h * HEAD_DIM, (h + 1) * HEAD_DIM)
                acc = wts[0][:, h:h + 1] * o_refs[0][rows, cols].astype(F32)
                for g in range(1, n_groups):
                    acc = acc + wts[g][:, h:h + 1] * o_refs[g][rows, cols].astype(F32)
                merged_ref[rows, cols] = acc.astype(BF16)
            return carry
        lax.fori_loop(0, tm // MERGE_ROWS, merge, 0)
        mixed = merged_ref[...]

    y = jnp.dot(mixed, w_ref[...], preferred_element_type=F32)
    xo_ref[...] = x_ref[...] + gate_ref[...] * y

    def norm(k, carry):
        r = pl.multiple_of(k * NORM_ROWS, NORM_ROWS)
        h = _modulated_rms_norm(xo_ref[pl.ds(r, NORM_ROWS), :], g_ref[...], sc_ref[...], sh_ref[...])
        h_ref[pl.ds(r, NORM_ROWS), :] = h.astype(BF16)
        return carry
    lax.fori_loop(0, tm // NORM_ROWS, norm, 0)


def _out_projection(outs, lses, w, x, gate, g, sc, sh, *, n_heads):
    s, d = x.shape
    dm = w.shape[0]
    n_groups = len(outs)
    tm = OUT_TM
    row_dm = pl.BlockSpec((tm, dm), lambda i: (i, 0))
    row_d = pl.BlockSpec((tm, d), lambda i: (i, 0))
    vec = pl.BlockSpec((1, d), lambda i: (0, 0))
    in_specs = [row_dm] * n_groups
    args = list(outs)
    scratch = []
    if n_groups > 1:
        in_specs += [pl.BlockSpec((tm, LANES), lambda i: (i, 0))] * n_groups
        args += list(lses)
        scratch = [pltpu.VMEM((tm, dm), BF16)]
    in_specs += [pl.BlockSpec((dm, d), lambda i: (0, 0)), row_d, vec, vec, vec, vec]
    args += [w, x, gate, g, sc, sh]
    vmem = (2 * n_groups * tm * dm * 2 + 2 * dm * d * 2 + 2 * tm * d * 4 + 2 * tm * d * 4
            + 2 * tm * d * 2 + tm * dm * 2 + tm * d * 4)
    return pl.pallas_call(
        functools.partial(_out_kernel, n_groups=n_groups, n_heads=n_heads),
        name="mixer_out_projection",
        grid=(s // tm,),
        in_specs=in_specs,
        out_specs=[row_d, row_d],
        out_shape=[jax.ShapeDtypeStruct((s, d), F32), jax.ShapeDtypeStruct((s, d), BF16)],
        scratch_shapes=scratch,
        compiler_params=_params(("arbitrary",), vmem),
    )(*args)


FFN_TM = 1024
FFN_TF = 512


def _causal_conv(u, cw_ref, cb_ref, tail):
    rows8 = lax.broadcasted_iota(jnp.int32, (SUBLANES, u.shape[1]), 0)

    def shifted(k):
        body = pltpu.roll(u, k, axis=0)
        head = jnp.where(rows8 < k, pltpu.roll(tail, k, axis=0), body[:SUBLANES])
        return jnp.concatenate([head, body[SUBLANES:]], axis=0)

    y = cb_ref[...] + cw_ref[0:1, :] * shifted(2)
    y = y + cw_ref[1:2, :] * shifted(1)
    return y + cw_ref[2:3, :] * u


def _ffn_kernel(h_ref, wa_ref, wg_ref, cwa_ref, cwg_ref, cba_ref, cbg_ref, wd_ref, x_ref, gate_ref,
                fg_ref, o_ref, tail_a_ref, tail_g_ref, *, final_norm):
    i = pl.program_id(0)
    j = pl.program_id(1)
    tm = h_ref.shape[0]

    @pl.when(i == 0)
    def _():
        tail_a_ref[j] = jnp.zeros(tail_a_ref.shape[1:], F32)
        tail_g_ref[j] = jnp.zeros(tail_g_ref.shape[1:], F32)

    h = h_ref[...]
    ua = jnp.dot(h, wa_ref[...], preferred_element_type=F32)
    ug = jnp.dot(h, wg_ref[...], preferred_element_type=F32)
    ya = _causal_conv(ua, cwa_ref, cba_ref, tail_a_ref[j])
    yg = _causal_conv(ug, cwg_ref, cbg_ref, tail_g_ref[j])
    tail_a_ref[j] = ua[tm - SUBLANES:, :]
    tail_g_ref[j] = ug[tm - SUBLANES:, :]
    act = (yg * (1.0 / (1.0 + jnp.exp(-yg))) * ya).astype(BF16)
    part = jnp.dot(act, wd_ref[...], preferred_element_type=F32)

    @pl.when(j == 0)
    def _():
        o_ref[...] = part

    @pl.when(j > 0)
    def _():
        o_ref[...] += part

    @pl.when(j == pl.num_programs(1) - 1)
    def _():
        def fin(k, carry):
            r = pl.multiple_of(k * NORM_ROWS, NORM_ROWS)
            rows = pl.ds(r, NORM_ROWS)
            xn = x_ref[rows, :] + gate_ref[...] * o_ref[rows, :]
            if final_norm:
                xn = (xn * lax.rsqrt(jnp.mean(xn * xn, axis=-1, keepdims=True) + EPS)) * fg_ref[...]
            o_ref[rows, :] = xn
            return carry
        lax.fori_loop(0, tm // NORM_ROWS, fin, 0)


def _pad_halves(a, d_ff, d_ff_pad):
    pad = [(0, 0)] * (a.ndim - 1) + [(0, d_ff_pad - d_ff)]
    return jnp.concatenate([jnp.pad(a[..., :d_ff], pad), jnp.pad(a[..., d_ff:], pad)], axis=-1)


def _conv_ffn(h, w_up, conv_w, conv_b, w_down, x, gate, final_g, *, final_norm):
    s, d = x.shape
    d_ff = w_down.shape[0]
    tm, tf = FFN_TM, FFN_TF
    d_ff_pad = pl.cdiv(d_ff, tf) * tf
    nj = d_ff_pad // tf
    w_up_p = _pad_halves(w_up, d_ff, d_ff_pad).astype(BF16)
    conv_w_p = _pad_halves(conv_w, d_ff, d_ff_pad)
    conv_b_p = _pad_halves(conv_b.reshape(1, -1), d_ff, d_ff_pad)
    w_down_p = jnp.pad(w_down, ((0, d_ff_pad - d_ff), (0, 0))).astype(BF16)
    vec = pl.BlockSpec((1, d), lambda i, j: (0, 0))
    vmem = (2 * tm * d * 2 + 2 * 2 * d * tf * 2 + 2 * tf * d * 2 + tm * d * 4 + 2 * tm * d * 4
            + 2 * nj * SUBLANES * tf * 4 + 8 * tm * tf * 4)
    return pl.pallas_call(
        functools.partial(_ffn_kernel, final_norm=final_norm),
        name="conv_ffn",
        grid=(s // tm, nj),
        in_specs=[pl.BlockSpec((tm, d), lambda i, j: (i, 0)),
                  pl.BlockSpec((d, tf), lambda i, j: (0, j)),
                  pl.BlockSpec((d, tf), lambda i, j: (0, nj + j)),
                  pl.BlockSpec((CONV_WIDTH, tf), lambda i, j: (0, j)),
                  pl.BlockSpec((CONV_WIDTH, tf), lambda i, j: (0, nj + j)),
                  pl.BlockSpec((1, tf), lambda i, j: (0, j)),
                  pl.BlockSpec((1, tf), lambda i, j: (0, nj + j)),
                  pl.BlockSpec((tf, d), lambda i, j: (j, 0)),
                  pl.BlockSpec((tm, d), lambda i, j: (i, 0), pipeline_mode=pl.Buffered(1)),
                  vec, vec],
        out_specs=pl.BlockSpec((tm, d), lambda i, j: (i, 0)),
        out_shape=jax.ShapeDtypeStruct((s, d), F32),
        scratch_shapes=[pltpu.VMEM((nj, SUBLANES, tf), F32), pltpu.VMEM((nj, SUBLANES, tf), F32)],
        compiler_params=_params(("arbitrary", "arbitrary"), vmem),
    )(h, w_up_p, w_up_p, conv_w_p, conv_w_p, conv_b_p, conv_b_p, w_down_p, x, gate, final_g.reshape(1, d))


def kernel(x, c, norm_mix_g, norm_ffn_g, ada_w, ada_b, w_in_a, w_out_a, w_in_b, w_out_b,
           w_up, conv_w, conv_b, w_down, final_g):
    batch, s, d = x.shape
    assert batch == 1, "the sequence is processed as one (S, D) slab"
    depth = ada_w.shape[0]
    n_heads = w_out_a.shape[1] // HEAD_DIM
    n_groups = len(DILATED_CONFIGS)

    xs = x.reshape(s, d)
    mod = _ada_modulation(c, ada_w, ada_b)
    for i in range(depth):
        sh1, sc1, g1, sh2, sc2, g2 = [mod[i, :, k * d:(k + 1) * d] for k in range(N_MOD)]
        norm_g = norm_mix_g[i].reshape(1, d)
        ffn_g = norm_ffn_g[i].reshape(1, d)
        if i % 2 == 0:
            qkv = _qkv_projection(xs, norm_g, sc1, sh1, w_in_a[i // 2].astype(BF16), head_major=False)
            outs, lses = zip(*[_dilated_group_attention(qkv, g, n_groups=n_groups, n_heads=n_heads)
                               for g in range(n_groups)])
            w_out = w_out_a[i // 2]
        else:
            qkv = _qkv_projection(xs, norm_g, sc1, sh1, w_in_b[i // 2].astype(BF16), head_major=True)
            outs, lses = [_stick_breaking_attention(qkv, n_heads=n_heads)], None
            w_out = w_out_b[i // 2]
        xs, h2 = _out_projection(outs, lses, w_out.astype(BF16), xs, g1, ffn_g, sc2, sh2, n_heads=n_heads)
        xs = _conv_ffn(h2, w_up[i], conv_w[i], conv_b[i], w_down[i], xs, g2, final_g,
                       final_norm=(i == depth - 1))
    return xs.reshape(batch, s, d)
```

```python
import functools

import numpy as np
import jax
import jax.numpy as jnp
from jax import lax
from jax.experimental import pallas as pl
from jax.experimental.pallas import tpu as pltpu

HEAD_DIM = 128
DILATED_CONFIGS = ((128, 1), (512, 4), (2048, 16))
ATTN_SPAN_MAX = 128
CONV_WIDTH = 3
EPS = 1e-6
N_MOD = 6
NEG_BIG = -1e30
N_RESIDUES = 16
MID_DILATION = 4

LANES = 128
SUBLANES = 8
BF16_SUBLANES = 16
V7X_VMEM_BYTES = 64 * 1024 * 1024
VMEM_HEADROOM_BYTES = 6 * 1024 * 1024

F32_EXP_UNDERFLOW = 104.0

F32 = jnp.float32
BF16 = jnp.bfloat16


def _params(semantics, vmem_estimate_bytes):
    limit = min(int(vmem_estimate_bytes) + VMEM_HEADROOM_BYTES, V7X_VMEM_BYTES - VMEM_HEADROOM_BYTES)
    return pltpu.CompilerParams(dimension_semantics=semantics, vmem_limit_bytes=limit)


def _modulated_rms_norm(x, g, sc, sh):
    y = x * lax.rsqrt(jnp.mean(x * x, axis=-1, keepdims=True) + EPS)
    return (y * g) * (1.0 + sc) + sh


ADA_TN = 1024
ADA_ROWS = 256


def _ada_kernel(c_ref, w_ref, b_ref, o_ref):
    d = w_ref.shape[1]
    tn = w_ref.shape[2]

    def body(k, acc):
        r = pl.multiple_of(k * ADA_ROWS, ADA_ROWS)
        prod = c_ref[pl.ds(r, ADA_ROWS), :] * w_ref[0, pl.ds(r, ADA_ROWS), :]
        return acc + prod.reshape(ADA_ROWS // SUBLANES, SUBLANES, tn).sum(axis=0)

    acc = lax.fori_loop(0, d // ADA_ROWS, body, jnp.zeros((SUBLANES, tn), F32))
    o_ref[0] = acc.sum(axis=0, keepdims=True) + b_ref[0]


def _ada_modulation(c, ada_w, ada_b):
    depth, d, n = ada_w.shape
    c_col = c.reshape(d, 1)
    return pl.pallas_call(
        _ada_kernel,
        name="ada_modulation",
        grid=(depth, n // ADA_TN),
        in_specs=[
            pl.BlockSpec((d, 1), lambda l, j: (0, 0)),
            pl.BlockSpec((1, d, ADA_TN), lambda l, j: (l, 0, j)),
            pl.BlockSpec((1, 1, ADA_TN), lambda l, j: (l, 0, j)),
        ],
        out_specs=pl.BlockSpec((1, 1, ADA_TN), lambda l, j: (l, 0, j)),
        out_shape=jax.ShapeDtypeStruct((depth, 1, n), F32),
        compiler_params=_params(("arbitrary", "arbitrary"),
                                2 * d * ADA_TN * 4 + d * LANES * 4),
    )(c_col, ada_w, ada_b.reshape(depth, 1, n))


QKV_TM = 1024
QKV_TN = 1024
NORM_ROWS = 64


def _residue_of_slot(slot):
    per = N_RESIDUES // MID_DILATION
    return MID_DILATION * (slot % per) + slot // per


def _residue_major_permutation(rows):
    rho = np.arange(rows)
    rows_per = rows // N_RESIDUES
    src = N_RESIDUES * (rho % rows_per) + _residue_of_slot(rho // rows_per)
    perm = np.zeros((rows, rows), np.float32)
    perm[rho, src] = 1.0
    return perm


def _qkv_kernel(x_ref, g_ref, sc_ref, sh_ref, w_ref, *rest, layout):
    if layout == "residue_major":
        perm_ref, o_ref, h_ref = rest
    else:
        o_ref, h_ref = rest
    tm = x_ref.shape[0]

    @pl.when(pl.program_id(1) == 0)
    def _():
        def body(k, carry):
            r = pl.multiple_of(k * NORM_ROWS, NORM_ROWS)
            h = _modulated_rms_norm(x_ref[pl.ds(r, NORM_ROWS), :], g_ref[...], sc_ref[...], sh_ref[...])
            h_ref[pl.ds(r, NORM_ROWS), :] = h.astype(BF16)
            return carry
        lax.fori_loop(0, tm // NORM_ROWS, body, 0)
        if layout == "residue_major":
            h_ref[...] = jnp.dot(perm_ref[...], h_ref[...], preferred_element_type=F32).astype(BF16)

    res = jnp.dot(h_ref[...], w_ref[...], preferred_element_type=F32)
    if layout == "head_major":
        for cb in range(o_ref.shape[0]):
            o_ref[cb] = res[:, cb * HEAD_DIM:(cb + 1) * HEAD_DIM].astype(BF16)
    else:
        o_ref[...] = res.reshape(o_ref.shape).astype(BF16)


def _qkv_projection(x, g, sc, sh, w, *, layout):
    s, d = x.shape
    n = w.shape[1]
    tm, tn = QKV_TM, QKV_TN
    vec = pl.BlockSpec((1, d), lambda i, j: (0, 0))
    in_specs = [pl.BlockSpec((tm, d), lambda i, j: (i, 0)), vec, vec, vec,
                pl.BlockSpec((d, tn), lambda i, j: (0, j))]
    args = [x, g, sc, sh, w]
    vmem = 2 * tm * d * 4 + tm * d * 2 + 2 * d * tn * 2 + 2 * tm * tn * 2 + tm * tn * 4
    if layout == "head_major":
        out_shape = jax.ShapeDtypeStruct((n // HEAD_DIM, s, HEAD_DIM), BF16)
        out_spec = pl.BlockSpec((tn // HEAD_DIM, tm, HEAD_DIM), lambda i, j: (j, i, 0))
    else:
        assert layout == "residue_major"
        out_shape = jax.ShapeDtypeStruct((N_RESIDUES, s // N_RESIDUES, n), BF16)
        out_spec = pl.BlockSpec((N_RESIDUES, tm // N_RESIDUES, tn), lambda i, j: (0, i, j))
        in_specs.append(pl.BlockSpec((tm, tm), lambda i, j: (0, 0)))
        args.append(jnp.asarray(_residue_major_permutation(tm), BF16))
        vmem += 2 * tm * tm * 2 + tm * d * (4 + 2)
    return pl.pallas_call(
        functools.partial(_qkv_kernel, layout=layout),
        name="qkv_projection",
        grid=(s // tm, n // tn),
        in_specs=in_specs,
        out_specs=out_spec,
        out_shape=out_shape,
        scratch_shapes=[pltpu.VMEM((tm, d), BF16)],
        compiler_params=_params(("arbitrary", "arbitrary"), vmem),
    )(*args)


def _alibi_slopes(n):
    return [float(np.float32(2.0 ** (-8.0 * (i + 1) / n))) for i in range(n)]


def _dilated_kernel(q_ref, kp_ref, kc_ref, vp_ref, vc_ref, dp_ref, dc_ref, o_ref, lse_ref, *, n_heads):
    slots, rows_per, _ = q_ref.shape
    rows = slots * rows_per
    first_block = pl.program_id(1) == 0
    dist_prev = dp_ref[...]
    dist_cur = dc_ref[...]
    ok_prev = dist_prev >= jnp.where(first_block, jnp.inf, 0.0)
    ok_cur = dist_cur >= 0.0
    lane = lax.broadcasted_iota(jnp.int32, (rows, LANES), 1)
    scale = HEAD_DIM ** -0.5
    slopes = _alibi_slopes(n_heads)
    contract_last = (((1,), (1,)), ((), ()))

    lse_tile = jnp.zeros((rows, LANES), F32)
    for h in range(n_heads):
        cols = slice(h * HEAD_DIM, (h + 1) * HEAD_DIM)
        q = q_ref[:, :, cols].reshape(rows, HEAD_DIM)
        k_prev = kp_ref[:, :, cols].reshape(rows, HEAD_DIM)
        k_cur = kc_ref[:, :, cols].reshape(rows, HEAD_DIM)
        s_prev = lax.dot_general(q, k_prev, contract_last, preferred_element_type=F32)
        s_cur = lax.dot_general(q, k_cur, contract_last, preferred_element_type=F32)
        s_prev = jnp.where(ok_prev, s_prev * scale - slopes[h] * dist_prev, NEG_BIG)
        s_cur = jnp.where(ok_cur, s_cur * scale - slopes[h] * dist_cur, NEG_BIG)
        m = jnp.maximum(jnp.max(s_prev, axis=-1, keepdims=True), jnp.max(s_cur, axis=-1, keepdims=True))
        p_prev = jnp.exp(s_prev - m)
        p_cur = jnp.exp(s_cur - m)
        denom = jnp.sum(p_prev, axis=-1, keepdims=True) + jnp.sum(p_cur, axis=-1, keepdims=True)
        acc = jnp.dot(p_prev.astype(BF16), vp_ref[:, :, cols].reshape(rows, HEAD_DIM), preferred_element_type=F32)
        acc = acc + jnp.dot(p_cur.astype(BF16), vc_ref[:, :, cols].reshape(rows, HEAD_DIM),
                            preferred_element_type=F32)
        o_ref[:, :, cols] = (acc / denom).reshape(slots, rows_per, HEAD_DIM).astype(o_ref.dtype)
        lse_tile = jnp.where(lane == h, m + jnp.log(denom), lse_tile)
    lse_ref[...] = lse_tile.reshape(lse_ref.shape)


def _window_distances(dilation, span, slots, rows_per):
    rows = slots * rows_per
    rho = np.arange(rows)
    pos = N_RESIDUES * (rho % rows_per) + _residue_of_slot(rho // rows_per)
    m = pos // dilation
    j_cur = m[:, None] - m[None, :]
    j_prev = j_cur + rows

    def dist(j):
        return np.where((j >= 0) & (j <= span), j * dilation, -1).astype(np.float32)
    return jnp.asarray(dist(j_prev)), jnp.asarray(dist(j_cur))


def _dilated_group_attention(qkv, group, *, n_heads):
    window, dilation = DILATED_CONFIGS[group]
    n_res, s_per, _ = qkv.shape
    dm = n_heads * HEAD_DIM
    span = window // dilation
    assert n_res == N_RESIDUES and N_RESIDUES % dilation == 0 and n_heads <= LANES
    assert window % dilation == 0 and span <= ATTN_SPAN_MAX
    slots = N_RESIDUES // dilation
    rows_per = max(ATTN_SPAN_MAX // slots, BF16_SUBLANES)
    rows = slots * rows_per
    assert s_per % rows_per == 0 and rows >= span
    dist_prev, dist_cur = _window_distances(dilation, span, slots, rows_per)
    base = group * 3

    def spec(which, prev):
        def index_map(p, b):
            return (p, jnp.maximum(b - 1, 0) if prev else b, base + which)
        return pl.BlockSpec((slots, rows_per, dm), index_map)

    const = pl.BlockSpec((rows, rows), lambda p, b: (0, 0))
    return pl.pallas_call(
        functools.partial(_dilated_kernel, n_heads=n_heads),
        name=f"dilated_attention_g{group}",
        grid=(N_RESIDUES // slots, s_per // rows_per),
        in_specs=[spec(0, False), spec(1, True), spec(1, False), spec(2, True), spec(2, False), const, const],
        out_specs=[pl.BlockSpec((slots, rows_per, dm), lambda p, b: (p, b, 0)),
                   pl.BlockSpec((slots, rows_per, LANES), lambda p, b: (p, b, 0))],
        out_shape=[jax.ShapeDtypeStruct((N_RESIDUES, s_per, dm), BF16),
                   jax.ShapeDtypeStruct((N_RESIDUES, s_per, LANES), F32)],
        compiler_params=_params(("arbitrary", "arbitrary"),
                                2 * 6 * rows * dm * 2 + 2 * rows * LANES * 4 + 12 * rows * rows * 4),
    )(qkv, qkv, qkv, qkv, qkv, dist_prev, dist_cur)


SB_TQ = 256


def _sb_kernel(q_ref, k_ref, v_ref, tri_ref, o_ref, acc_ref, c_ref):
    tq = q_ref.shape[1]
    near = 2 * tq
    i = pl.program_id(1)
    q = q_ref[0]
    scale = HEAD_DIM ** -0.5
    contract_last = (((1,), (1,)), ((), ()))

    def window(kstart, width, causal_mask):
        k = k_ref[0, pl.ds(kstart, width), :]
        v = v_ref[0, pl.ds(kstart, width), :]
        z = lax.dot_general(q, k, contract_last, preferred_element_type=F32) * scale
        log_beta = jnp.minimum(z, 0.0) - jnp.log(1.0 + jnp.exp(-jnp.abs(z)))
        log_keep = log_beta - z
        if causal_mask is not None:
            log_keep = jnp.where(causal_mask, log_keep, 0.0)
        hi = log_keep.astype(BF16)
        lo = (log_keep - hi.astype(F32)).astype(BF16)
        tri = tri_ref[0:width, 0:width]
        after = (jnp.dot(hi, tri, preferred_element_type=F32)
                 + jnp.dot(lo, tri, preferred_element_type=F32)) + c_ref[...]
        a = jnp.exp(log_beta + after)
        if causal_mask is not None:
            a = jnp.where(causal_mask, a, 0.0)
        acc_ref[...] += jnp.dot(a.astype(BF16), v, preferred_element_type=F32)
        c_ref[...] = after[:, 0:1] + log_keep[:, 0:1]

    acc_ref[...] = jnp.zeros_like(acc_ref)
    c_ref[...] = jnp.zeros_like(c_ref)

    near_start = pl.multiple_of(jnp.maximum(i - 1, 0) * tq, tq)
    q_pos = i * tq + lax.broadcasted_iota(jnp.int32, (tq, near), 0)
    k_pos = near_start + lax.broadcasted_iota(jnp.int32, (tq, near), 1)
    window(near_start, near, k_pos < q_pos)

    def cond(carry):
        b, c_max = carry
        return (b >= 0) & (c_max > -F32_EXP_UNDERFLOW)

    def body(carry):
        b, _ = carry
        window(pl.multiple_of(b * tq, tq), tq, None)
        return b - 1, jnp.max(c_ref[...])

    lax.while_loop(cond, body, (i - 2, jnp.max(c_ref[...])))
    o_ref[...] = acc_ref[...].astype(o_ref.dtype)


def _stick_breaking_attention(qkv, *, n_heads):
    _, s, _ = qkv.shape
    tq = SB_TQ
    assert s % tq == 0 and s >= 2 * tq
    tri = jnp.tril(jnp.ones((2 * tq, 2 * tq), F32), k=-1).astype(BF16)
    vmem = (2 * 2 * s * HEAD_DIM * 2 + 2 * (2 * tq) ** 2 * 2 + 4 * tq * HEAD_DIM * 2
            + 12 * tq * 2 * tq * 4)
    return pl.pallas_call(
        _sb_kernel,
        name="stick_breaking_attention",
        grid=(n_heads, s // tq),
        in_specs=[pl.BlockSpec((1, tq, HEAD_DIM), lambda h, i: (h, i, 0)),
                  pl.BlockSpec((1, s, HEAD_DIM), lambda h, i: (n_heads + h, 0, 0)),
                  pl.BlockSpec((1, s, HEAD_DIM), lambda h, i: (2 * n_heads + h, 0, 0)),
                  pl.BlockSpec((2 * tq, 2 * tq), lambda h, i: (0, 0))],
        out_specs=pl.BlockSpec((tq, HEAD_DIM), lambda h, i: (i, h)),
        out_shape=jax.ShapeDtypeStruct((s, n_heads * HEAD_DIM), BF16),
        scratch_shapes=[pltpu.VMEM((tq, HEAD_DIM), F32), pltpu.VMEM((tq, 1), F32)],
        compiler_params=_params(("arbitrary", "arbitrary"), vmem),
    )(qkv, qkv, qkv, tri)


OUT_TM = 512


def _out_kernel(*refs, n_groups, n_heads):
    o_refs = refs[:n_groups]
    lse_refs = refs[n_groups:2 * n_groups] if n_groups > 1 else ()
    rest = refs[len(o_refs) + len(lse_refs):]
    w_ref, x_ref, gate_ref, g_ref, sc_ref, sh_ref = rest[:6]
    if n_groups == 1:
        xo_ref, h_ref = rest[6:]
    else:
        unperm_ref, xo_ref, h_ref, merged_ref = rest[6:]
    tm = x_ref.shape[0]

    if n_groups == 1:
        mixed = o_refs[0][...]
    else:
        rows_per = tm // N_RESIDUES

        def merge(k, carry):
            lses = [lr[k] for lr in lse_refs]
            m = functools.reduce(jnp.maximum, lses)
            es = [jnp.exp(l - m) for l in lses]
            inv = 1.0 / functools.reduce(jnp.add, es)
            wts = [e * inv for e in es]
            rows = pl.ds(pl.multiple_of(k * rows_per, rows_per), rows_per)
            for h in range(n_heads):
                cols = slice(h * HEAD_DIM, (h + 1) * HEAD_DIM)
                acc = wts[0][:, h:h + 1] * o_refs[0][k, :, cols].astype(F32)
                for g in range(1, n_groups):
                    acc = acc + wts[g][:, h:h + 1] * o_refs[g][k, :, cols].astype(F32)
                merged_ref[rows, cols] = acc.astype(BF16)
            return carry
        lax.fori_loop(0, N_RESIDUES, merge, 0)
        mixed = jnp.dot(unperm_ref[...], merged_ref[...], preferred_element_type=F32).astype(BF16)

    y = jnp.dot(mixed, w_ref[...], preferred_element_type=F32)
    xo_ref[...] = x_ref[...] + gate_ref[...] * y

    def norm(k, carry):
        r = pl.multiple_of(k * NORM_ROWS, NORM_ROWS)
        h = _modulated_rms_norm(xo_ref[pl.ds(r, NORM_ROWS), :], g_ref[...], sc_ref[...], sh_ref[...])
        h_ref[pl.ds(r, NORM_ROWS), :] = h.astype(BF16)
        return carry
    lax.fori_loop(0, tm // NORM_ROWS, norm, 0)


def _out_projection(outs, lses, w, x, gate, g, sc, sh, *, n_heads):
    s, d = x.shape
    dm = w.shape[0]
    n_groups = len(outs)
    tm = OUT_TM
    row_d = pl.BlockSpec((tm, d), lambda i: (i, 0))
    vec = pl.BlockSpec((1, d), lambda i: (0, 0))
    scratch = []
    if n_groups == 1:
        in_specs = [pl.BlockSpec((tm, dm), lambda i: (i, 0))]
        args = list(outs)
    else:
        rows_per = tm // N_RESIDUES
        in_specs = ([pl.BlockSpec((N_RESIDUES, rows_per, dm), lambda i: (0, i, 0))] * n_groups
                    + [pl.BlockSpec((N_RESIDUES, rows_per, LANES), lambda i: (0, i, 0))] * n_groups)
        args = list(outs) + list(lses)
        scratch = [pltpu.VMEM((tm, dm), BF16)]
    in_specs += [pl.BlockSpec((dm, d), lambda i: (0, 0)), row_d, vec, vec, vec, vec]
    args += [w, x, gate, g, sc, sh]
    if n_groups > 1:
        in_specs.append(pl.BlockSpec((tm, tm), lambda i: (0, 0)))
        args.append(jnp.asarray(_residue_major_permutation(tm).T, BF16))
    vmem = (2 * n_groups * tm * dm * 2 + 2 * dm * d * 2 + 2 * tm * d * 4 + 2 * tm * d * 4
            + 2 * tm * d * 2 + tm * dm * 2 + tm * d * 4)
    return pl.pallas_call(
        functools.partial(_out_kernel, n_groups=n_groups, n_heads=n_heads),
        name="mixer_out_projection",
        grid=(s // tm,),
        in_specs=in_specs,
        out_specs=[row_d, row_d],
        out_shape=[jax.ShapeDtypeStruct((s, d), F32), jax.ShapeDtypeStruct((s, d), BF16)],
        scratch_shapes=scratch,
        compiler_params=_params(("arbitrary",), vmem),
    )(*args)


FFN_TM = 1024
FFN_TF = 512


def _causal_conv(u, cw_ref, cb_ref, tail):
    rows8 = lax.broadcasted_iota(jnp.int32, (SUBLANES, u.shape[1]), 0)

    def shifted(k):
        body = pltpu.roll(u, k, axis=0)
        head = jnp.where(rows8 < k, pltpu.roll(tail, k, axis=0), body[:SUBLANES])
        return jnp.concatenate([head, body[SUBLANES:]], axis=0)

    y = cb_ref[...] + cw_ref[0:1, :] * shifted(2)
    y = y + cw_ref[1:2, :] * shifted(1)
    return y + cw_ref[2:3, :] * u


def _ffn_kernel(h_ref, wa_ref, wg_ref, cwa_ref, cwg_ref, cba_ref, cbg_ref, wd_ref, x_ref, gate_ref,
                fg_ref, o_ref, tail_a_ref, tail_g_ref, *, final_norm):
    i = pl.program_id(0)
    j = pl.program_id(1)
    tm = h_ref.shape[0]

    @pl.when(i == 0)
    def _():
        tail_a_ref[j] = jnp.zeros(tail_a_ref.shape[1:], F32)
        tail_g_ref[j] = jnp.zeros(tail_g_ref.shape[1:], F32)

    h = h_ref[...]
    ua = jnp.dot(h, wa_ref[...], preferred_element_type=F32)
    ug = jnp.dot(h, wg_ref[...], preferred_element_type=F32)
    ya = _causal_conv(ua, cwa_ref, cba_ref, tail_a_ref[j])
    yg = _causal_conv(ug, cwg_ref, cbg_ref, tail_g_ref[j])
    tail_a_ref[j] = ua[tm - SUBLANES:, :]
    tail_g_ref[j] = ug[tm - SUBLANES:, :]
    act = (yg * (1.0 / (1.0 + jnp.exp(-yg))) * ya).astype(BF16)
    part = jnp.dot(act, wd_ref[...], preferred_element_type=F32)

    @pl.when(j == 0)
    def _():
        o_ref[...] = part

    @pl.when(j > 0)
    def _():
        o_ref[...] += part

    @pl.when(j == pl.num_programs(1) - 1)
    def _():
        def fin(k, carry):
            r = pl.multiple_of(k * NORM_ROWS, NORM_ROWS)
            rows = pl.ds(r, NORM_ROWS)
            xn = x_ref[rows, :] + gate_ref[...] * o_ref[rows, :]
            if final_norm:
                xn = (xn * lax.rsqrt(jnp.mean(xn * xn, axis=-1, keepdims=True) + EPS)) * fg_ref[...]
            o_ref[rows, :] = xn
            return carry
        lax.fori_loop(0, tm // NORM_ROWS, fin, 0)


def _pad_halves(a, d_ff, d_ff_pad):
    pad = [(0, 0)] * (a.ndim - 1) + [(0, d_ff_pad - d_ff)]
    return jnp.concatenate([jnp.pad(a[..., :d_ff], pad), jnp.pad(a[..., d_ff:], pad)], axis=-1)


def _conv_ffn(h, w_up, conv_w, conv_b, w_down, x, gate, final_g, *, final_norm):
    s, d = x.shape
    d_ff = w_down.shape[0]
    tm, tf = FFN_TM, FFN_TF
    d_ff_pad = pl.cdiv(d_ff, tf) * tf
    nj = d_ff_pad // tf
    w_up_p = _pad_halves(w_up, d_ff, d_ff_pad).astype(BF16)
    conv_w_p = _pad_halves(conv_w, d_ff, d_ff_pad)
    conv_b_p = _pad_halves(conv_b.reshape(1, -1), d_ff, d_ff_pad)
    w_down_p = jnp.pad(w_down, ((0, d_ff_pad - d_ff), (0, 0))).astype(BF16)
    vec = pl.BlockSpec((1, d), lambda i, j: (0, 0))
    vmem = (2 * tm * d * 2 + 2 * 2 * d * tf * 2 + 2 * tf * d * 2 + tm * d * 4 + 2 * tm * d * 4
            + 2 * nj * SUBLANES * tf * 4 + 8 * tm * tf * 4)
    return pl.pallas_call(
        functools.partial(_ffn_kernel, final_norm=final_norm),
        name="conv_ffn",
        grid=(s // tm, nj),
        in_specs=[pl.BlockSpec((tm, d), lambda i, j: (i, 0)),
                  pl.BlockSpec((d, tf), lambda i, j: (0, j)),
                  pl.BlockSpec((d, tf), lambda i, j: (0, nj + j)),
                  pl.BlockSpec((CONV_WIDTH, tf), lambda i, j: (0, j)),
                  pl.BlockSpec((CONV_WIDTH, tf), lambda i, j: (0, nj + j)),
                  pl.BlockSpec((1, tf), lambda i, j: (0, j)),
                  pl.BlockSpec((1, tf), lambda i, j: (0, nj + j)),
                  pl.BlockSpec((tf, d), lambda i, j: (j, 0)),
                  pl.BlockSpec((tm, d), lambda i, j: (i, 0), pipeline_mode=pl.Buffered(1)),
                  vec, vec],
        out_specs=pl.BlockSpec((tm, d), lambda i, j: (i, 0)),
        out_shape=jax.ShapeDtypeStruct((s, d), F32),
        scratch_shapes=[pltpu.VMEM((nj, SUBLANES, tf), F32), pltpu.VMEM((nj, SUBLANES, tf), F32)],
        compiler_params=_params(("arbitrary", "arbitrary"), vmem),
    )(h, w_up_p, w_up_p, conv_w_p, conv_w_p, conv_b_p, conv_b_p, w_down_p, x, gate, final_g.reshape(1, d))


def kernel(x, c, norm_mix_g, norm_ffn_g, ada_w, ada_b, w_in_a, w_out_a, w_in_b, w_out_b,
           w_up, conv_w, conv_b, w_down, final_g):
    batch, s, d = x.shape
    assert batch == 1, "the sequence is processed as one (S, D) slab"
    depth = ada_w.shape[0]
    n_heads = w_out_a.shape[1] // HEAD_DIM
    n_groups = len(DILATED_CONFIGS)

    xs = x.reshape(s, d)
    mod = _ada_modulation(c, ada_w, ada_b)
    for i in range(depth):
        sh1, sc1, g1, sh2, sc2, g2 = [mod[i, :, k * d:(k + 1) * d] for k in range(N_MOD)]
        norm_g = norm_mix_g[i].reshape(1, d)
        ffn_g = norm_ffn_g[i].reshape(1, d)
        if i % 2 == 0:
            qkv = _qkv_projection(xs, norm_g, sc1, sh1, w_in_a[i // 2].astype(BF16), layout="residue_major")
            outs, lses = zip(*[_dilated_group_attention(qkv, g, n_heads=n_heads) for g in range(n_groups)])
            w_out = w_out_a[i // 2]
        else:
            qkv = _qkv_projection(xs, norm_g, sc1, sh1, w_in_b[i // 2].astype(BF16), layout="head_major")
            outs, lses = [_stick_breaking_attention(qkv, n_heads=n_heads)], None
            w_out = w_out_b[i // 2]
        xs, h2 = _out_projection(outs, lses, w_out.astype(BF16), xs, g1, ffn_g, sc2, sh2, n_heads=n_heads)
        xs = _conv_ffn(h2, w_up[i], conv_w[i], conv_b[i], w_down[i], xs, g2, final_g,
                       final_norm=(i == depth - 1))
    return xs.reshape(batch, s, d)
```

```python
import functools

import numpy as np
import jax
import jax.numpy as jnp
from jax import lax
from jax.experimental import pallas as pl
from jax.experimental.pallas import tpu as pltpu

HEAD_DIM = 128
DILATED_CONFIGS = ((128, 1), (512, 4), (2048, 16))
ATTN_SPAN_MAX = 128
CONV_WIDTH = 3
EPS = 1e-6
N_MOD = 6
NEG_BIG = -1e30
N_RESIDUES = 16
MID_DILATION = 4

LANES = 128
SUBLANES = 8
BF16_SUBLANES = 16
V7X_VMEM_BYTES = 64 * 1024 * 1024
VMEM_HEADROOM_BYTES = 6 * 1024 * 1024

F32_EXP2_UNDERFLOW = 150.0
LOG2_E = 1.4426950408889634

F32 = jnp.float32
BF16 = jnp.bfloat16


def _params(semantics, vmem_estimate_bytes):
    limit = min(int(vmem_estimate_bytes) + VMEM_HEADROOM_BYTES, V7X_VMEM_BYTES - VMEM_HEADROOM_BYTES)
    return pltpu.CompilerParams(dimension_semantics=semantics, vmem_limit_bytes=limit)


def _modulated_rms_norm(x, g, sc, sh):
    y = x * lax.rsqrt(jnp.mean(x * x, axis=-1, keepdims=True) + EPS)
    return (y * g) * (1.0 + sc) + sh


ADA_TN = 1024
ADA_ROWS = 256


def _ada_kernel(c_ref, w_ref, b_ref, o_ref):
    d = w_ref.shape[1]
    tn = w_ref.shape[2]

    def body(k, acc):
        r = pl.multiple_of(k * ADA_ROWS, ADA_ROWS)
        prod = c_ref[pl.ds(r, ADA_ROWS), :] * w_ref[0, pl.ds(r, ADA_ROWS), :]
        return acc + prod.reshape(ADA_ROWS // SUBLANES, SUBLANES, tn).sum(axis=0)

    acc = lax.fori_loop(0, d // ADA_ROWS, body, jnp.zeros((SUBLANES, tn), F32))
    o_ref[0] = acc.sum(axis=0, keepdims=True) + b_ref[0]


def _ada_modulation(c, ada_w, ada_b):
    depth, d, n = ada_w.shape
    c_col = c.reshape(d, 1)
    return pl.pallas_call(
        _ada_kernel,
        name="ada_modulation",
        grid=(depth, n // ADA_TN),
        in_specs=[
            pl.BlockSpec((d, 1), lambda l, j: (0, 0)),
            pl.BlockSpec((1, d, ADA_TN), lambda l, j: (l, 0, j)),
            pl.BlockSpec((1, 1, ADA_TN), lambda l, j: (l, 0, j)),
        ],
        out_specs=pl.BlockSpec((1, 1, ADA_TN), lambda l, j: (l, 0, j)),
        out_shape=jax.ShapeDtypeStruct((depth, 1, n), F32),
        compiler_params=_params(("arbitrary", "arbitrary"),
                                2 * d * ADA_TN * 4 + d * LANES * 4),
    )(c_col, ada_w, ada_b.reshape(depth, 1, n))


QKV_TM = 1024
QKV_TN = 1024
NORM_ROWS = 64


def _residue_of_slot(slot):
    per = N_RESIDUES // MID_DILATION
    return MID_DILATION * (slot % per) + slot // per


def _residue_major_permutation(rows):
    rho = np.arange(rows)
    rows_per = rows // N_RESIDUES
    src = N_RESIDUES * (rho % rows_per) + _residue_of_slot(rho // rows_per)
    perm = np.zeros((rows, rows), np.float32)
    perm[rho, src] = 1.0
    return perm


def _qkv_kernel(x_ref, g_ref, sc_ref, sh_ref, w_ref, *rest, layout):
    if layout == "residue_major":
        perm_ref, o_ref, h_ref = rest
    else:
        o_ref, h_ref = rest
    tm = x_ref.shape[0]

    @pl.when(pl.program_id(1) == 0)
    def _():
        def body(k, carry):
            r = pl.multiple_of(k * NORM_ROWS, NORM_ROWS)
            h = _modulated_rms_norm(x_ref[pl.ds(r, NORM_ROWS), :], g_ref[...], sc_ref[...], sh_ref[...])
            h_ref[pl.ds(r, NORM_ROWS), :] = h.astype(BF16)
            return carry
        lax.fori_loop(0, tm // NORM_ROWS, body, 0)
        if layout == "residue_major":
            h_ref[...] = jnp.dot(perm_ref[...], h_ref[...], preferred_element_type=F32).astype(BF16)

    res = jnp.dot(h_ref[...], w_ref[...], preferred_element_type=F32)
    if layout == "head_major":
        for cb in range(o_ref.shape[0]):
            o_ref[cb] = res[:, cb * HEAD_DIM:(cb + 1) * HEAD_DIM].astype(BF16)
    else:
        o_ref[...] = res.reshape(o_ref.shape).astype(BF16)


def _qkv_projection(x, g, sc, sh, w, *, layout):
    s, d = x.shape
    n = w.shape[1]
    tm, tn = QKV_TM, QKV_TN
    vec = pl.BlockSpec((1, d), lambda i, j: (0, 0))
    in_specs = [pl.BlockSpec((tm, d), lambda i, j: (i, 0)), vec, vec, vec,
                pl.BlockSpec((d, tn), lambda i, j: (0, j))]
    args = [x, g, sc, sh, w]
    vmem = 2 * tm * d * 4 + tm * d * 2 + 2 * d * tn * 2 + 2 * tm * tn * 2 + tm * tn * 4
    if layout == "head_major":
        out_shape = jax.ShapeDtypeStruct((n // HEAD_DIM, s, HEAD_DIM), BF16)
        out_spec = pl.BlockSpec((tn // HEAD_DIM, tm, HEAD_DIM), lambda i, j: (j, i, 0))
    else:
        assert layout == "residue_major"
        out_shape = jax.ShapeDtypeStruct((N_RESIDUES, s // N_RESIDUES, n), BF16)
        out_spec = pl.BlockSpec((N_RESIDUES, tm // N_RESIDUES, tn), lambda i, j: (0, i, j))
        in_specs.append(pl.BlockSpec((tm, tm), lambda i, j: (0, 0)))
        args.append(jnp.asarray(_residue_major_permutation(tm), BF16))
        vmem += 2 * tm * tm * 2 + tm * d * (4 + 2)
    return pl.pallas_call(
        functools.partial(_qkv_kernel, layout=layout),
        name="qkv_projection",
        grid=(s // tm, n // tn),
        in_specs=in_specs,
        out_specs=out_spec,
        out_shape=out_shape,
        scratch_shapes=[pltpu.VMEM((tm, d), BF16)],
        compiler_params=_params(("arbitrary", "arbitrary"), vmem),
    )(*args)


def _alibi_slopes(n):
    return [float(np.float32(2.0 ** (-8.0 * (i + 1) / n))) for i in range(n)]


def _dilated_kernel(q_ref, kp_ref, kc_ref, vp_ref, vc_ref, dist_ref, o_ref, lse_ref, *, n_heads):
    slots, rows_per, _ = q_ref.shape
    rows = slots * rows_per
    dist = dist_ref[...]
    in_prev = lax.broadcasted_iota(jnp.int32, dist.shape, 1) < rows
    first_block = pl.program_id(1) == 0
    ok = (dist >= 0.0) & jnp.logical_not(in_prev & first_block)
    lane = lax.broadcasted_iota(jnp.int32, (rows, LANES), 1)
    scale = HEAD_DIM ** -0.5
    slopes = _alibi_slopes(n_heads)
    contract_last = (((1,), (1,)), ((), ()))

    def head(ref, h):
        return ref[:, :, h * HEAD_DIM:(h + 1) * HEAD_DIM].reshape(rows, HEAD_DIM)

    scores = []
    for h in range(n_heads):
        keys = jnp.concatenate([head(kp_ref, h), head(kc_ref, h)], axis=0)
        scores.append(lax.dot_general(head(q_ref, h), keys, contract_last, preferred_element_type=F32))

    lse_tile = jnp.zeros((rows, LANES), F32)
    for h in range(n_heads):
        s = jnp.where(ok, scores[h] * scale - slopes[h] * dist, NEG_BIG)
        m = jnp.max(s, axis=-1, keepdims=True)
        p = jnp.exp(s - m)
        denom = jnp.sum(p, axis=-1, keepdims=True)
        values = jnp.concatenate([head(vp_ref, h), head(vc_ref, h)], axis=0)
        acc = jnp.dot(p.astype(BF16), values, preferred_element_type=F32)
        o_ref[:, :, h * HEAD_DIM:(h + 1) * HEAD_DIM] = (
            (acc / denom).reshape(slots, rows_per, HEAD_DIM).astype(o_ref.dtype))
        lse_tile = jnp.where(lane == h, m + jnp.log(denom), lse_tile)
    lse_ref[...] = lse_tile.reshape(lse_ref.shape)


def _window_distances(dilation, span, slots, rows_per):
    rows = slots * rows_per
    rho = np.arange(rows)
    pos = N_RESIDUES * (rho % rows_per) + _residue_of_slot(rho // rows_per)
    m = pos // dilation
    j_cur = m[:, None] - m[None, :]
    j = np.concatenate([j_cur + rows, j_cur], axis=1)
    return jnp.asarray(np.where((j >= 0) & (j <= span), j * dilation, -1).astype(np.float32))


def _dilated_group_attention(qkv, group, *, n_heads):
    window, dilation = DILATED_CONFIGS[group]
    n_res, s_per, _ = qkv.shape
    dm = n_heads * HEAD_DIM
    span = window // dilation
    assert n_res == N_RESIDUES and N_RESIDUES % dilation == 0 and n_heads <= LANES
    assert window % dilation == 0 and span <= ATTN_SPAN_MAX
    slots = N_RESIDUES // dilation
    rows_per = max(ATTN_SPAN_MAX // slots, BF16_SUBLANES)
    rows = slots * rows_per
    assert s_per % rows_per == 0 and rows >= span
    dist = _window_distances(dilation, span, slots, rows_per)
    base = group * 3

    def spec(which, prev):
        def index_map(p, b):
            return (p, jnp.maximum(b - 1, 0) if prev else b, base + which)
        return pl.BlockSpec((slots, rows_per, dm), index_map)

    return pl.pallas_call(
        functools.partial(_dilated_kernel, n_heads=n_heads),
        name=f"dilated_attention_g{group}",
        grid=(N_RESIDUES // slots, s_per // rows_per),
        in_specs=[spec(0, False), spec(1, True), spec(1, False), spec(2, True), spec(2, False),
                  pl.BlockSpec((rows, 2 * rows), lambda p, b: (0, 0))],
        out_specs=[pl.BlockSpec((slots, rows_per, dm), lambda p, b: (p, b, 0)),
                   pl.BlockSpec((slots, rows_per, LANES), lambda p, b: (p, b, 0))],
        out_shape=[jax.ShapeDtypeStruct((N_RESIDUES, s_per, dm), BF16),
                   jax.ShapeDtypeStruct((N_RESIDUES, s_per, LANES), F32)],
        compiler_params=_params(("arbitrary", "arbitrary"),
                                2 * 6 * rows * dm * 2 + 2 * rows * LANES * 4
                                + (4 + 2 * n_heads) * rows * 2 * rows * 4),
    )(qkv, qkv, qkv, qkv, qkv, dist)


SB_TQ = 1024
SB_SUB = 128
SB_BACK = 256


def _sb_kernel(q_ref, k_ref, v_ref, tri2_ref, o_ref, acc_ref, c_ref):
    tq = q_ref.shape[1]
    sub = SB_SUB
    n_sub = tq // sub
    near = SB_BACK + sub
    i = pl.program_id(1)
    to_log2 = HEAD_DIM ** -0.5 * LOG2_E
    contract_last = (((1,), (1,)), ((), ()))

    def scores(q, kstart, width):
        k = k_ref[0, pl.ds(kstart, width), :]
        return lax.dot_general(q, k, contract_last, preferred_element_type=F32) * to_log2

    def log_terms(y):
        neg_abs = pltpu.bitcast(pltpu.bitcast(y, jnp.uint32) | jnp.uint32(0x80000000), F32)
        log_beta = jnp.minimum(y, 0.0) - jnp.log2(1.0 + jnp.exp2(neg_abs))
        log_keep = log_beta - y
        hi = log_keep.astype(BF16)
        lo = (log_keep - hi.astype(F32)).astype(BF16)
        return log_beta, log_keep, hi, lo

    col_minus_row = (lax.broadcasted_iota(jnp.int32, (sub, near), 1)
                     - lax.broadcasted_iota(jnp.int32, (sub, near), 0))
    kstarts, ys = [], []
    for r in range(n_sub):
        q_start = i * tq + r * sub
        kstart = pl.multiple_of(jnp.maximum(q_start - SB_BACK, 0), sub)
        y = scores(q_ref[0, r * sub:(r + 1) * sub, :], kstart, near)
        kstarts.append(kstart)
        ys.append(jnp.where(col_minus_row < q_start - kstart, y, NEG_BIG))
    stage = []
    for r in range(n_sub):
        log_beta, log_keep, hi, lo = log_terms(ys[r])
        after = jnp.dot(jnp.concatenate([hi, lo], axis=1), tri2_ref[...], preferred_element_type=F32)
        stage.append((log_beta, log_keep, after))
    for r in range(n_sub):
        log_beta, log_keep, after = stage[r]
        a = jnp.exp2(log_beta + after)
        v = v_ref[0, pl.ds(kstarts[r], near), :]
        acc_ref[r * sub:(r + 1) * sub, :] = jnp.dot(a.astype(BF16), v, preferred_element_type=F32)
        c_ref[r * sub:(r + 1) * sub, :] = after[:, 0:1] + log_keep[:, 0:1]

    @pl.when(jnp.max(c_ref[...]) > -F32_EXP2_UNDERFLOW)
    def _():
        tri = tri2_ref[0:sub, 0:sub]
        for r in range(n_sub):
            rows = slice(r * sub, (r + 1) * sub)
            q = q_ref[0, rows, :]

            def cond(carry):
                b, c_max = carry
                return (b >= 0) & (c_max > -F32_EXP2_UNDERFLOW)

            def body(carry):
                b, _ = carry
                kstart = pl.multiple_of(b * sub, sub)
                log_beta, log_keep, hi, lo = log_terms(scores(q, kstart, sub))
                after = (jnp.dot(hi, tri, preferred_element_type=F32)
                         + jnp.dot(lo, tri, preferred_element_type=F32)) + c_ref[rows, :]
                a = jnp.exp2(log_beta + after)
                v = v_ref[0, pl.ds(kstart, sub), :]
                acc_ref[rows, :] += jnp.dot(a.astype(BF16), v, preferred_element_type=F32)
                c_new = after[:, 0:1] + log_keep[:, 0:1]
                c_ref[rows, :] = c_new
                return b - 1, jnp.max(c_new)

            first_far = jnp.maximum(i * n_sub + r - SB_BACK // sub, 0) - 1
            lax.while_loop(cond, body, (first_far, jnp.max(c_ref[rows, :])))

    o_ref[...] = acc_ref[...].astype(o_ref.dtype)


def _stick_breaking_attention(qkv, *, n_heads):
    _, s, _ = qkv.shape
    tq, sub = SB_TQ, SB_SUB
    near = SB_BACK + sub
    assert s % tq == 0 and tq % sub == 0 and SB_BACK % sub == 0 and s >= near
    tri = np.tril(np.ones((near, near), np.float32), k=-1)
    tri2 = jnp.asarray(np.concatenate([tri, tri], axis=0), BF16)
    vmem = (2 * 2 * s * HEAD_DIM * 2 + 2 * 2 * near * near * 2 + 4 * tq * HEAD_DIM * 2
            + tq * LANES * 4 * 2 + 10 * tq * near * 4)
    return pl.pallas_call(
        _sb_kernel,
        name="stick_breaking_attention",
        grid=(n_heads, s // tq),
        in_specs=[pl.BlockSpec((1, tq, HEAD_DIM), lambda h, i: (h, i, 0)),
                  pl.BlockSpec((1, s, HEAD_DIM), lambda h, i: (n_heads + h, 0, 0)),
                  pl.BlockSpec((1, s, HEAD_DIM), lambda h, i: (2 * n_heads + h, 0, 0)),
                  pl.BlockSpec((2 * near, near), lambda h, i: (0, 0))],
        out_specs=pl.BlockSpec((tq, HEAD_DIM), lambda h, i: (i, h)),
        out_shape=jax.ShapeDtypeStruct((s, n_heads * HEAD_DIM), BF16),
        scratch_shapes=[pltpu.VMEM((tq, HEAD_DIM), F32), pltpu.VMEM((tq, 1), F32)],
        compiler_params=_params(("arbitrary", "arbitrary"), vmem),
    )(qkv, qkv, qkv, tri2)


OUT_TM = 512


def _out_kernel(*refs, n_groups, n_heads):
    o_refs = refs[:n_groups]
    lse_refs = refs[n_groups:2 * n_groups] if n_groups > 1 else ()
    rest = refs[len(o_refs) + len(lse_refs):]
    w_ref, x_ref, gate_ref, g_ref, sc_ref, sh_ref = rest[:6]
    if n_groups == 1:
        xo_ref, h_ref = rest[6:]
    else:
        unperm_ref, xo_ref, h_ref, merged_ref = rest[6:]
    tm = x_ref.shape[0]

    if n_groups == 1:
        mixed = o_refs[0][...]
    else:
        rows_per = tm // N_RESIDUES

        def merge(k, carry):
            lses = [lr[k] for lr in lse_refs]
            m = functools.reduce(jnp.maximum, lses)
            es = [jnp.exp(l - m) for l in lses]
            inv = 1.0 / functools.reduce(jnp.add, es)
            wts = [e * inv for e in es]
            rows = pl.ds(pl.multiple_of(k * rows_per, rows_per), rows_per)
            for h in range(n_heads):
                cols = slice(h * HEAD_DIM, (h + 1) * HEAD_DIM)
                acc = wts[0][:, h:h + 1] * o_refs[0][k, :, cols].astype(F32)
                for g in range(1, n_groups):
                    acc = acc + wts[g][:, h:h + 1] * o_refs[g][k, :, cols].astype(F32)
                merged_ref[rows, cols] = acc.astype(BF16)
            return carry
        lax.fori_loop(0, N_RESIDUES, merge, 0)
        mixed = jnp.dot(unperm_ref[...], merged_ref[...], preferred_element_type=F32).astype(BF16)

    y = jnp.dot(mixed, w_ref[...], preferred_element_type=F32)
    xo_ref[...] = x_ref[...] + gate_ref[...] * y

    def norm(k, carry):
        r = pl.multiple_of(k * NORM_ROWS, NORM_ROWS)
        h = _modulated_rms_norm(xo_ref[pl.ds(r, NORM_ROWS), :], g_ref[...], sc_ref[...], sh_ref[...])
        h_ref[pl.ds(r, NORM_ROWS), :] = h.astype(BF16)
        return carry
    lax.fori_loop(0, tm // NORM_ROWS, norm, 0)


def _out_projection(outs, lses, w, x, gate, g, sc, sh, *, n_heads):
    s, d = x.shape
    dm = w.shape[0]
    n_groups = len(outs)
    tm = OUT_TM
    row_d = pl.BlockSpec((tm, d), lambda i: (i, 0))
    vec = pl.BlockSpec((1, d), lambda i: (0, 0))
    scratch = []
    if n_groups == 1:
        in_specs = [pl.BlockSpec((tm, dm), lambda i: (i, 0))]
        args = list(outs)
    else:
        rows_per = tm // N_RESIDUES
        in_specs = ([pl.BlockSpec((N_RESIDUES, rows_per, dm), lambda i: (0, i, 0))] * n_groups
                    + [pl.BlockSpec((N_RESIDUES, rows_per, LANES), lambda i: (0, i, 0))] * n_groups)
        args = list(outs) + list(lses)
        scratch = [pltpu.VMEM((tm, dm), BF16)]
    in_specs += [pl.BlockSpec((dm, d), lambda i: (0, 0)), row_d, vec, vec, vec, vec]
    args += [w, x, gate, g, sc, sh]
    if n_groups > 1:
        in_specs.append(pl.BlockSpec((tm, tm), lambda i: (0, 0)))
        args.append(jnp.asarray(_residue_major_permutation(tm).T, BF16))
    vmem = (2 * n_groups * tm * dm * 2 + 2 * dm * d * 2 + 2 * tm * d * 4 + 2 * tm * d * 4
            + 2 * tm * d * 2 + tm * dm * 2 + tm * d * 4)
    return pl.pallas_call(
        functools.partial(_out_kernel, n_groups=n_groups, n_heads=n_heads),
        name="mixer_out_projection",
        grid=(s // tm,),
        in_specs=in_specs,
        out_specs=[row_d, row_d],
        out_shape=[jax.ShapeDtypeStruct((s, d), F32), jax.ShapeDtypeStruct((s, d), BF16)],
        scratch_shapes=scratch,
        compiler_params=_params(("arbitrary",), vmem),
    )(*args)


FFN_TM = 1024
FFN_TF = 512
FFN_ROW_SPLIT = 2


def _causal_conv(u, cw_ref, cb_ref, tail):
    rows8 = lax.broadcasted_iota(jnp.int32, (SUBLANES, u.shape[1]), 0)

    def shifted(k):
        body = pltpu.roll(u, k, axis=0)
        head = jnp.where(rows8 < k, pltpu.roll(tail, k, axis=0), body[:SUBLANES])
        return jnp.concatenate([head, body[SUBLANES:]], axis=0)

    y = cb_ref[...] + cw_ref[0:1, :] * shifted(2)
    y = y + cw_ref[1:2, :] * shifted(1)
    return y + cw_ref[2:3, :] * u


def _ffn_kernel(h_ref, wa_ref, wg_ref, cwa_ref, cwg_ref, cba_ref, cbg_ref, wd_ref, x_ref, gate_ref,
                fg_ref, o_ref, tail_a_ref, tail_g_ref, *, final_norm):
    i = pl.program_id(0)
    j = pl.program_id(1)
    tm = h_ref.shape[0]

    @pl.when(i == 0)
    def _():
        tail_a_ref[j] = jnp.zeros(tail_a_ref.shape[1:], F32)
        tail_g_ref[j] = jnp.zeros(tail_g_ref.shape[1:], F32)

    @pl.when(j == 0)
    def _():
        o_ref[...] = jnp.zeros_like(o_ref)

    sub = tm // FFN_ROW_SPLIT
    ups = []
    for r in range(FFN_ROW_SPLIT):
        h = h_ref[r * sub:(r + 1) * sub, :]
        ups.append((jnp.dot(h, wa_ref[...], preferred_element_type=F32),
                    jnp.dot(h, wg_ref[...], preferred_element_type=F32)))
    tail_a = tail_a_ref[j]
    tail_g = tail_g_ref[j]
    for r, (ua, ug) in enumerate(ups):
        ya = _causal_conv(ua, cwa_ref, cba_ref, tail_a)
        yg = _causal_conv(ug, cwg_ref, cbg_ref, tail_g)
        tail_a = ua[sub - SUBLANES:, :]
        tail_g = ug[sub - SUBLANES:, :]
        act = (yg * (1.0 / (1.0 + jnp.exp(-yg))) * ya).astype(BF16)
        o_ref[r * sub:(r + 1) * sub, :] += jnp.dot(act, wd_ref[...], preferred_element_type=F32)
    tail_a_ref[j] = tail_a
    tail_g_ref[j] = tail_g

    @pl.when(j == pl.num_programs(1) - 1)
    def _():
        def fin(k, carry):
            r = pl.multiple_of(k * NORM_ROWS, NORM_ROWS)
            rows = pl.ds(r, NORM_ROWS)
            xn = x_ref[rows, :] + gate_ref[...] * o_ref[rows, :]
            if final_norm:
                xn = (xn * lax.rsqrt(jnp.mean(xn * xn, axis=-1, keepdims=True) + EPS)) * fg_ref[...]
            o_ref[rows, :] = xn
            return carry
        lax.fori_loop(0, tm // NORM_ROWS, fin, 0)


def _pad_halves(a, d_ff, d_ff_pad):
    pad = [(0, 0)] * (a.ndim - 1) + [(0, d_ff_pad - d_ff)]
    return jnp.concatenate([jnp.pad(a[..., :d_ff], pad), jnp.pad(a[..., d_ff:], pad)], axis=-1)


def _conv_ffn(h, w_up, conv_w, conv_b, w_down, x, gate, final_g, *, final_norm):
    s, d = x.shape
    d_ff = w_down.shape[0]
    tm, tf = FFN_TM, FFN_TF
    d_ff_pad = pl.cdiv(d_ff, tf) * tf
    nj = d_ff_pad // tf
    w_up_p = _pad_halves(w_up, d_ff, d_ff_pad).astype(BF16)
    conv_w_p = _pad_halves(conv_w, d_ff, d_ff_pad)
    conv_b_p = _pad_halves(conv_b.reshape(1, -1), d_ff, d_ff_pad)
    w_down_p = jnp.pad(w_down, ((0, d_ff_pad - d_ff), (0, 0))).astype(BF16)
    vec = pl.BlockSpec((1, d), lambda i, j: (0, 0))
    vmem = (2 * tm * d * 2 + 2 * 2 * d * tf * 2 + 2 * tf * d * 2 + tm * d * 4 + 2 * tm * d * 4
            + 2 * nj * SUBLANES * tf * 4 + 8 * tm * tf * 4)
    return pl.pallas_call(
        functools.partial(_ffn_kernel, final_norm=final_norm),
        name="conv_ffn",
        grid=(s // tm, nj),
        in_specs=[pl.BlockSpec((tm, d), lambda i, j: (i, 0)),
                  pl.BlockSpec((d, tf), lambda i, j: (0, j)),
                  pl.BlockSpec((d, tf), lambda i, j: (0, nj + j)),
                  pl.BlockSpec((CONV_WIDTH, tf), lambda i, j: (0, j)),
                  pl.BlockSpec((CONV_WIDTH, tf), lambda i, j: (0, nj + j)),
                  pl.BlockSpec((1, tf), lambda i, j: (0, j)),
                  pl.BlockSpec((1, tf), lambda i, j: (0, nj + j)),
                  pl.BlockSpec((tf, d), lambda i, j: (j, 0)),
                  pl.BlockSpec((tm, d), lambda i, j: (i, 0), pipeline_mode=pl.Buffered(1)),
                  vec, vec],
        out_specs=pl.BlockSpec((tm, d), lambda i, j: (i, 0)),
        out_shape=jax.ShapeDtypeStruct((s, d), F32),
        scratch_shapes=[pltpu.VMEM((nj, SUBLANES, tf), F32), pltpu.VMEM((nj, SUBLANES, tf), F32)],
        compiler_params=_params(("arbitrary", "arbitrary"), vmem),
    )(h, w_up_p, w_up_p, conv_w_p, conv_w_p, conv_b_p, conv_b_p, w_down_p, x, gate, final_g.reshape(1, d))


def kernel(x, c, norm_mix_g, norm_ffn_g, ada_w, ada_b, w_in_a, w_out_a, w_in_b, w_out_b,
           w_up, conv_w, conv_b, w_down, final_g):
    batch, s, d = x.shape
    assert batch == 1, "the sequence is processed as one (S, D) slab"
    depth = ada_w.shape[0]
    n_heads = w_out_a.shape[1] // HEAD_DIM
    n_groups = len(DILATED_CONFIGS)

    xs = x.reshape(s, d)
    mod = _ada_modulation(c, ada_w, ada_b)
    for i in range(depth):
        sh1, sc1, g1, sh2, sc2, g2 = [mod[i, :, k * d:(k + 1) * d] for k in range(N_MOD)]
        norm_g = norm_mix_g[i].reshape(1, d)
        ffn_g = norm_ffn_g[i].reshape(1, d)
        if i % 2 == 0:
            qkv = _qkv_projection(xs, norm_g, sc1, sh1, w_in_a[i // 2].astype(BF16), layout="residue_major")
            outs, lses = zip(*[_dilated_group_attention(qkv, g, n_heads=n_heads) for g in range(n_groups)])
            w_out = w_out_a[i // 2]
        else:
            qkv = _qkv_projection(xs, norm_g, sc1, sh1, w_in_b[i // 2].astype(BF16), layout="head_major")
            outs, lses = [_stick_breaking_attention(qkv, n_heads=n_heads)], None
            w_out = w_out_b[i // 2]
        xs, h2 = _out_projection(outs, lses, w_out.astype(BF16), xs, g1, ffn_g, sc2, sh2, n_heads=n_heads)
        xs = _conv_ffn(h2, w_up[i], conv_w[i], conv_b[i], w_down[i], xs, g2, final_g,
                       final_norm=(i == depth - 1))
    return xs.reshape(batch, s, d)
```

```python
import functools

import numpy as np
import jax
import jax.numpy as jnp
from jax import lax
from jax.experimental import pallas as pl
from jax.experimental.pallas import tpu as pltpu

HEAD_DIM = 128
DILATED_CONFIGS = ((128, 1), (512, 4), (2048, 16))
ATTN_SPAN_MAX = 128
CONV_WIDTH = 3
EPS = 1e-6
N_MOD = 6
NEG_BIG = -1e30
N_RESIDUES = 16
MID_DILATION = 4

LANES = 128
SUBLANES = 8
BF16_SUBLANES = 16
V7X_VMEM_BYTES = 64 * 1024 * 1024
VMEM_HEADROOM_BYTES = 6 * 1024 * 1024

F32_EXP2_UNDERFLOW = 150.0
LOG2_E = 1.4426950408889634

F32 = jnp.float32
BF16 = jnp.bfloat16


def _params(semantics, vmem_estimate_bytes):
    limit = min(int(vmem_estimate_bytes) + VMEM_HEADROOM_BYTES, V7X_VMEM_BYTES - VMEM_HEADROOM_BYTES)
    return pltpu.CompilerParams(dimension_semantics=semantics, vmem_limit_bytes=limit)


def _modulated_rms_norm(x, g, sc, sh):
    y = x * lax.rsqrt(jnp.mean(x * x, axis=-1, keepdims=True) + EPS)
    return (y * g) * (1.0 + sc) + sh


ADA_TN = 1024
ADA_ROWS = 256


def _ada_kernel(c_ref, w_ref, b_ref, o_ref):
    d = w_ref.shape[1]
    tn = w_ref.shape[2]

    def body(k, acc):
        r = pl.multiple_of(k * ADA_ROWS, ADA_ROWS)
        prod = c_ref[pl.ds(r, ADA_ROWS), :] * w_ref[0, pl.ds(r, ADA_ROWS), :]
        return acc + prod.reshape(ADA_ROWS // SUBLANES, SUBLANES, tn).sum(axis=0)

    acc = lax.fori_loop(0, d // ADA_ROWS, body, jnp.zeros((SUBLANES, tn), F32))
    o_ref[0] = acc.sum(axis=0, keepdims=True) + b_ref[0]


def _ada_modulation(c, ada_w, ada_b):
    depth, d, n = ada_w.shape
    c_col = c.reshape(d, 1)
    return pl.pallas_call(
        _ada_kernel,
        name="ada_modulation",
        grid=(depth, n // ADA_TN),
        in_specs=[
            pl.BlockSpec((d, 1), lambda l, j: (0, 0)),
            pl.BlockSpec((1, d, ADA_TN), lambda l, j: (l, 0, j)),
            pl.BlockSpec((1, 1, ADA_TN), lambda l, j: (l, 0, j)),
        ],
        out_specs=pl.BlockSpec((1, 1, ADA_TN), lambda l, j: (l, 0, j)),
        out_shape=jax.ShapeDtypeStruct((depth, 1, n), F32),
        compiler_params=_params(("arbitrary", "arbitrary"),
                                2 * d * ADA_TN * 4 + d * LANES * 4),
    )(c_col, ada_w, ada_b.reshape(depth, 1, n))


QKV_TM = 1024
QKV_TN = 1024
NORM_ROWS = 64


def _residue_of_slot(slot):
    per = N_RESIDUES // MID_DILATION
    return MID_DILATION * (slot % per) + slot // per


def _residue_major_permutation(rows):
    rho = np.arange(rows)
    rows_per = rows // N_RESIDUES
    src = N_RESIDUES * (rho % rows_per) + _residue_of_slot(rho // rows_per)
    perm = np.zeros((rows, rows), np.float32)
    perm[rho, src] = 1.0
    return perm


def _qkv_kernel(x_ref, g_ref, sc_ref, sh_ref, w_ref, *rest, layout):
    if layout == "residue_major":
        perm_ref, o_ref, h_ref = rest
    else:
        o_ref, h_ref = rest
    tm = x_ref.shape[0]

    @pl.when(pl.program_id(1) == 0)
    def _():
        def body(k, carry):
            r = pl.multiple_of(k * NORM_ROWS, NORM_ROWS)
            h = _modulated_rms_norm(x_ref[pl.ds(r, NORM_ROWS), :], g_ref[...], sc_ref[...], sh_ref[...])
            h_ref[pl.ds(r, NORM_ROWS), :] = h.astype(BF16)
            return carry
        lax.fori_loop(0, tm // NORM_ROWS, body, 0)
        if layout == "residue_major":
            h_ref[...] = jnp.dot(perm_ref[...], h_ref[...], preferred_element_type=F32).astype(BF16)

    res = jnp.dot(h_ref[...], w_ref[...], preferred_element_type=F32)
    if layout == "head_major":
        for cb in range(o_ref.shape[0]):
            o_ref[cb] = res[:, cb * HEAD_DIM:(cb + 1) * HEAD_DIM].astype(BF16)
    else:
        o_ref[...] = res.reshape(o_ref.shape).astype(BF16)


def _qkv_projection(x, g, sc, sh, w, *, layout):
    s, d = x.shape
    n = w.shape[1]
    tm, tn = QKV_TM, QKV_TN
    vec = pl.BlockSpec((1, d), lambda i, j: (0, 0))
    in_specs = [pl.BlockSpec((tm, d), lambda i, j: (i, 0)), vec, vec, vec,
                pl.BlockSpec((d, tn), lambda i, j: (0, j))]
    args = [x, g, sc, sh, w]
    vmem = 2 * tm * d * 4 + tm * d * 2 + 2 * d * tn * 2 + 2 * tm * tn * 2 + tm * tn * 4
    if layout == "head_major":
        out_shape = jax.ShapeDtypeStruct((n // HEAD_DIM, s, HEAD_DIM), BF16)
        out_spec = pl.BlockSpec((tn // HEAD_DIM, tm, HEAD_DIM), lambda i, j: (j, i, 0))
    else:
        assert layout == "residue_major"
        out_shape = jax.ShapeDtypeStruct((N_RESIDUES, s // N_RESIDUES, n), BF16)
        out_spec = pl.BlockSpec((N_RESIDUES, tm // N_RESIDUES, tn), lambda i, j: (0, i, j))
        in_specs.append(pl.BlockSpec((tm, tm), lambda i, j: (0, 0)))
        args.append(jnp.asarray(_residue_major_permutation(tm), BF16))
        vmem += 2 * tm * tm * 2 + tm * d * (4 + 2)
    return pl.pallas_call(
        functools.partial(_qkv_kernel, layout=layout),
        name="qkv_projection",
        grid=(s // tm, n // tn),
        in_specs=in_specs,
        out_specs=out_spec,
        out_shape=out_shape,
        scratch_shapes=[pltpu.VMEM((tm, d), BF16)],
        compiler_params=_params(("arbitrary", "arbitrary"), vmem),
    )(*args)


def _alibi_slopes(n):
    return [float(np.float32(2.0 ** (-8.0 * (i + 1) / n))) for i in range(n)]


def _dilated_kernel(q_ref, kp_ref, kc_ref, vp_ref, vc_ref, dist_ref, o_ref, lse_ref, *, n_heads):
    slots, rows_per, _ = q_ref.shape
    rows = slots * rows_per
    dist = dist_ref[...]
    in_prev = lax.broadcasted_iota(jnp.int32, dist.shape, 1) < rows
    first_block = pl.program_id(1) == 0
    ok = (dist >= 0.0) & jnp.logical_not(in_prev & first_block)
    lane = lax.broadcasted_iota(jnp.int32, (rows, LANES), 1)
    scale = HEAD_DIM ** -0.5
    slopes = _alibi_slopes(n_heads)
    contract_last = (((1,), (1,)), ((), ()))

    def head(ref, h):
        return ref[:, :, h * HEAD_DIM:(h + 1) * HEAD_DIM].reshape(rows, HEAD_DIM)

    scores = []
    for h in range(n_heads):
        keys = jnp.concatenate([head(kp_ref, h), head(kc_ref, h)], axis=0)
        scores.append(lax.dot_general(head(q_ref, h), keys, contract_last, preferred_element_type=F32))

    lse_tile = jnp.zeros((rows, LANES), F32)
    for h in range(n_heads):
        s = jnp.where(ok, scores[h] * scale - slopes[h] * dist, NEG_BIG)
        m = jnp.max(s, axis=-1, keepdims=True)
        p = jnp.exp(s - m)
        denom = jnp.sum(p, axis=-1, keepdims=True)
        values = jnp.concatenate([head(vp_ref, h), head(vc_ref, h)], axis=0)
        acc = jnp.dot(p.astype(BF16), values, preferred_element_type=F32)
        o_ref[:, :, h * HEAD_DIM:(h + 1) * HEAD_DIM] = (
            (acc / denom).reshape(slots, rows_per, HEAD_DIM).astype(o_ref.dtype))
        lse_tile = jnp.where(lane == h, m + jnp.log(denom), lse_tile)
    lse_ref[...] = lse_tile.reshape(lse_ref.shape)


def _window_distances(dilation, span, slots, rows_per):
    rows = slots * rows_per
    rho = np.arange(rows)
    pos = N_RESIDUES * (rho % rows_per) + _residue_of_slot(rho // rows_per)
    m = pos // dilation
    j_cur = m[:, None] - m[None, :]
    j = np.concatenate([j_cur + rows, j_cur], axis=1)
    return jnp.asarray(np.where((j >= 0) & (j <= span), j * dilation, -1).astype(np.float32))


def _dilated_group_attention(qkv, group, *, n_heads):
    window, dilation = DILATED_CONFIGS[group]
    n_res, s_per, _ = qkv.shape
    dm = n_heads * HEAD_DIM
    span = window // dilation
    assert n_res == N_RESIDUES and N_RESIDUES % dilation == 0 and n_heads <= LANES
    assert window % dilation == 0 and span <= ATTN_SPAN_MAX
    slots = N_RESIDUES // dilation
    rows_per = max(ATTN_SPAN_MAX // slots, BF16_SUBLANES)
    rows = slots * rows_per
    assert s_per % rows_per == 0 and rows >= span
    dist = _window_distances(dilation, span, slots, rows_per)
    base = group * 3

    def spec(which, prev):
        def index_map(p, b):
            return (p, jnp.maximum(b - 1, 0) if prev else b, base + which)
        return pl.BlockSpec((slots, rows_per, dm), index_map)

    return pl.pallas_call(
        functools.partial(_dilated_kernel, n_heads=n_heads),
        name=f"dilated_attention_g{group}",
        grid=(N_RESIDUES // slots, s_per // rows_per),
        in_specs=[spec(0, False), spec(1, True), spec(1, False), spec(2, True), spec(2, False),
                  pl.BlockSpec((rows, 2 * rows), lambda p, b: (0, 0))],
        out_specs=[pl.BlockSpec((slots, rows_per, dm), lambda p, b: (p, b, 0)),
                   pl.BlockSpec((slots, rows_per, LANES), lambda p, b: (p, b, 0))],
        out_shape=[jax.ShapeDtypeStruct((N_RESIDUES, s_per, dm), BF16),
                   jax.ShapeDtypeStruct((N_RESIDUES, s_per, LANES), F32)],
        compiler_params=_params(("arbitrary", "arbitrary"),
                                2 * 6 * rows * dm * 2 + 2 * rows * LANES * 4
                                + (4 + 2 * n_heads) * rows * 2 * rows * 4),
    )(qkv, qkv, qkv, qkv, qkv, dist)


SB_TQ = 1024
SB_SUB = 128
SB_BLK = 256
SB_NEAR = 512
SB_GROUP = 4


def _sb_kernel(q_ref, k_ref, v_ref, tri2_ref, o_ref, acc_ref, c_ref):
    tq = q_ref.shape[1]
    sub, blk, near = SB_SUB, SB_BLK, SB_NEAR
    n_sub = tq // sub
    i = pl.program_id(1)
    to_log2 = HEAD_DIM ** -0.5 * LOG2_E
    contract_last = (((1,), (1,)), ((), ()))

    def masked_scores(r, kstart, width, limit, *, causal):
        q = q_ref[0, r * sub:(r + 1) * sub, :]
        k = k_ref[0, pl.ds(kstart, width), :]
        y = lax.dot_general(q, k, contract_last, preferred_element_type=F32) * to_log2
        col = lax.broadcasted_iota(jnp.int32, (sub, width), 1)
        if causal:
            col = col - lax.broadcasted_iota(jnp.int32, (sub, width), 0)
        return jnp.where(col < limit, y, NEG_BIG)

    def log_terms(y):
        neg_abs = pltpu.bitcast(pltpu.bitcast(y, jnp.uint32) | jnp.uint32(0x80000000), F32)
        log_beta = jnp.minimum(y, 0.0) - jnp.log2(1.0 + jnp.exp2(neg_abs))
        log_keep = log_beta - y
        return log_beta, log_keep

    def staged_log_terms(ys):
        n_blk = ys[0].shape[1] // blk
        stage = []
        for g in range(0, len(ys), SB_GROUP):
            terms, lhs = [], []
            for y in ys[g:g + SB_GROUP]:
                log_beta, log_keep = log_terms(y)
                hi = log_keep.astype(BF16)
                lo = (log_keep - hi.astype(F32)).astype(BF16)
                lhs += [jnp.concatenate([hi[:, b * blk:(b + 1) * blk], lo[:, b * blk:(b + 1) * blk]], axis=1)
                        for b in range(n_blk)]
                terms.append((log_beta, log_keep))
            local = jnp.dot(jnp.concatenate(lhs, axis=0), tri2_ref[...], preferred_element_type=F32)
            rows = n_blk * sub
            stage += [(lb, lk, local[k * rows:(k + 1) * rows, :]) for k, (lb, lk) in enumerate(terms)]
        return stage

    n_blk = near // blk
    kstarts, ys = [], []
    for r in range(n_sub):
        q_start = i * tq + r * sub
        kstart = pl.multiple_of(jnp.maximum(q_start + sub - near, 0), sub)
        kstarts.append(kstart)
        ys.append(masked_scores(r, kstart, near, q_start - kstart, causal=True))
    stage = staged_log_terms(ys)
    for r in range(n_sub):
        log_beta, log_keep, local = stage[r]
        newer = jnp.zeros((sub, 1), F32)
        after = [None] * n_blk
        for b in reversed(range(n_blk)):
            loc = local[b * sub:(b + 1) * sub, :]
            after[b] = loc + newer if b < n_blk - 1 else loc
            newer = newer + (loc[:, 0:1] + log_keep[:, b * blk:b * blk + 1])
        a = jnp.exp2(log_beta + jnp.concatenate(after, axis=1))
        v = v_ref[0, pl.ds(kstarts[r], near), :]
        acc_ref[r * sub:(r + 1) * sub, :] = jnp.dot(a.astype(BF16), v, preferred_element_type=F32)
        c_ref[r * sub:(r + 1) * sub, :] = newer

    def cond(carry):
        m, c_max = carry
        return (kstarts[n_sub - 1] - m * blk > 0) & (c_max > -F32_EXP2_UNDERFLOW)

    def body(carry):
        m, _ = carry
        wins, ys = [], []
        for r in range(n_sub):
            kend = kstarts[r] - m * blk
            wstart = pl.multiple_of(jnp.maximum(kend - blk, 0), sub)
            wins.append(wstart)
            ys.append(masked_scores(r, wstart, blk, kend - wstart, causal=False))
        stage = staged_log_terms(ys)
        c_old = c_ref[...]
        outs, c_news = [], []
        for r in range(n_sub):
            log_beta, log_keep, local = stage[r]
            after = local + c_old[r * sub:(r + 1) * sub, :]
            a = jnp.exp2(log_beta + after)
            v = v_ref[0, pl.ds(wins[r], blk), :]
            outs.append(jnp.dot(a.astype(BF16), v, preferred_element_type=F32))
            c_news.append(after[:, 0:1] + log_keep[:, 0:1])
        c_new = jnp.concatenate(c_news, axis=0)
        acc_ref[...] += jnp.concatenate(outs, axis=0)
        c_ref[...] = c_new
        return m + 1, jnp.max(c_new)

    lax.while_loop(cond, body, (0, jnp.max(c_ref[...])))
    o_ref[...] = acc_ref[...].astype(o_ref.dtype)


def _stick_breaking_attention(qkv, *, n_heads):
    _, s, _ = qkv.shape
    tq, sub, blk, near = SB_TQ, SB_SUB, SB_BLK, SB_NEAR
    assert s % tq == 0 and tq % sub == 0 and near % blk == 0 and blk % sub == 0 and s >= near
    tri = np.tril(np.ones((blk, blk), np.float32), k=-1)
    tri2 = jnp.asarray(np.concatenate([tri, tri], axis=0), BF16)
    vmem = (2 * 2 * s * HEAD_DIM * 2 + 2 * 2 * blk * blk * 2 + 4 * tq * HEAD_DIM * 2
            + tq * LANES * 4 * 2 + 10 * tq * near * 4)
    return pl.pallas_call(
        _sb_kernel,
        name="stick_breaking_attention",
        grid=(n_heads, s // tq),
        in_specs=[pl.BlockSpec((1, tq, HEAD_DIM), lambda h, i: (h, i, 0)),
                  pl.BlockSpec((1, s, HEAD_DIM), lambda h, i: (n_heads + h, 0, 0)),
                  pl.BlockSpec((1, s, HEAD_DIM), lambda h, i: (2 * n_heads + h, 0, 0)),
                  pl.BlockSpec((2 * blk, blk), lambda h, i: (0, 0))],
        out_specs=pl.BlockSpec((tq, HEAD_DIM), lambda h, i: (i, h)),
        out_shape=jax.ShapeDtypeStruct((s, n_heads * HEAD_DIM), BF16),
        scratch_shapes=[pltpu.VMEM((tq, HEAD_DIM), F32), pltpu.VMEM((tq, 1), F32)],
        compiler_params=_params(("arbitrary", "arbitrary"), vmem),
    )(qkv, qkv, qkv, tri2)


OUT_TM = 512


def _out_kernel(*refs, n_groups, n_heads):
    o_refs = refs[:n_groups]
    lse_refs = refs[n_groups:2 * n_groups] if n_groups > 1 else ()
    rest = refs[len(o_refs) + len(lse_refs):]
    w_ref, x_ref, gate_ref, g_ref, sc_ref, sh_ref = rest[:6]
    if n_groups == 1:
        xo_ref, h_ref = rest[6:]
    else:
        unperm_ref, xo_ref, h_ref, merged_ref = rest[6:]
    tm = x_ref.shape[0]

    if n_groups == 1:
        mixed = o_refs[0][...]
    else:
        rows_per = tm // N_RESIDUES

        def merge(k, carry):
            lses = [lr[k] for lr in lse_refs]
            m = functools.reduce(jnp.maximum, lses)
            es = [jnp.exp(l - m) for l in lses]
            inv = 1.0 / functools.reduce(jnp.add, es)
            wts = [e * inv for e in es]
            rows = pl.ds(pl.multiple_of(k * rows_per, rows_per), rows_per)
            for h in range(n_heads):
                cols = slice(h * HEAD_DIM, (h + 1) * HEAD_DIM)
                acc = wts[0][:, h:h + 1] * o_refs[0][k, :, cols].astype(F32)
                for g in range(1, n_groups):
                    acc = acc + wts[g][:, h:h + 1] * o_refs[g][k, :, cols].astype(F32)
                merged_ref[rows, cols] = acc.astype(BF16)
            return carry
        lax.fori_loop(0, N_RESIDUES, merge, 0)
        mixed = jnp.dot(unperm_ref[...], merged_ref[...], preferred_element_type=F32).astype(BF16)

    y = jnp.dot(mixed, w_ref[...], preferred_element_type=F32)
    xo_ref[...] = x_ref[...] + gate_ref[...] * y

    def norm(k, carry):
        r = pl.multiple_of(k * NORM_ROWS, NORM_ROWS)
        h = _modulated_rms_norm(xo_ref[pl.ds(r, NORM_ROWS), :], g_ref[...], sc_ref[...], sh_ref[...])
        h_ref[pl.ds(r, NORM_ROWS), :] = h.astype(BF16)
        return carry
    lax.fori_loop(0, tm // NORM_ROWS, norm, 0)


def _out_projection(outs, lses, w, x, gate, g, sc, sh, *, n_heads):
    s, d = x.shape
    dm = w.shape[0]
    n_groups = len(outs)
    tm = OUT_TM
    row_d = pl.BlockSpec((tm, d), lambda i: (i, 0))
    vec = pl.BlockSpec((1, d), lambda i: (0, 0))
    scratch = []
    if n_groups == 1:
        in_specs = [pl.BlockSpec((tm, dm), lambda i: (i, 0))]
        args = list(outs)
    else:
        rows_per = tm // N_RESIDUES
        in_specs = ([pl.BlockSpec((N_RESIDUES, rows_per, dm), lambda i: (0, i, 0))] * n_groups
                    + [pl.BlockSpec((N_RESIDUES, rows_per, LANES), lambda i: (0, i, 0))] * n_groups)
        args = list(outs) + list(lses)
        scratch = [pltpu.VMEM((tm, dm), BF16)]
    in_specs += [pl.BlockSpec((dm, d), lambda i: (0, 0)), row_d, vec, vec, vec, vec]
    args += [w, x, gate, g, sc, sh]
    if n_groups > 1:
        in_specs.append(pl.BlockSpec((tm, tm), lambda i: (0, 0)))
        args.append(jnp.asarray(_residue_major_permutation(tm).T, BF16))
    vmem = (2 * n_groups * tm * dm * 2 + 2 * dm * d * 2 + 2 * tm * d * 4 + 2 * tm * d * 4
            + 2 * tm * d * 2 + tm * dm * 2 + tm * d * 4)
    return pl.pallas_call(
        functools.partial(_out_kernel, n_groups=n_groups, n_heads=n_heads),
        name="mixer_out_projection",
        grid=(s // tm,),
        in_specs=in_specs,
        out_specs=[row_d, row_d],
        out_shape=[jax.ShapeDtypeStruct((s, d), F32), jax.ShapeDtypeStruct((s, d), BF16)],
        scratch_shapes=scratch,
        compiler_params=_params(("arbitrary",), vmem),
    )(*args)


FFN_TM = 1024
FFN_TF = 512
FFN_ROW_SPLIT = 2


def _causal_conv(u, cw_ref, cb_ref, tail):
    rows8 = lax.broadcasted_iota(jnp.int32, (SUBLANES, u.shape[1]), 0)

    def shifted(k):
        body = pltpu.roll(u, k, axis=0)
        head = jnp.where(rows8 < k, pltpu.roll(tail, k, axis=0), body[:SUBLANES])
        return jnp.concatenate([head, body[SUBLANES:]], axis=0)

    y = cb_ref[...] + cw_ref[0:1, :] * shifted(2)
    y = y + cw_ref[1:2, :] * shifted(1)
    return y + cw_ref[2:3, :] * u


def _ffn_kernel(h_ref, wa_ref, wg_ref, cwa_ref, cwg_ref, cba_ref, cbg_ref, wd_ref, x_ref, gate_ref,
                fg_ref, o_ref, tail_a_ref, tail_g_ref, *, final_norm):
    i = pl.program_id(0)
    j = pl.program_id(1)
    tm = h_ref.shape[0]

    @pl.when(i == 0)
    def _():
        tail_a_ref[j] = jnp.zeros(tail_a_ref.shape[1:], F32)
        tail_g_ref[j] = jnp.zeros(tail_g_ref.shape[1:], F32)

    @pl.when(j == 0)
    def _():
        o_ref[...] = jnp.zeros_like(o_ref)

    sub = tm // FFN_ROW_SPLIT
    ups = []
    for r in range(FFN_ROW_SPLIT):
        h = h_ref[r * sub:(r + 1) * sub, :]
        ups.append((jnp.dot(h, wa_ref[...], preferred_element_type=F32),
                    jnp.dot(h, wg_ref[...], preferred_element_type=F32)))
    tail_a = tail_a_ref[j]
    tail_g = tail_g_ref[j]
    for r, (ua, ug) in enumerate(ups):
        ya = _causal_conv(ua, cwa_ref, cba_ref, tail_a)
        yg = _causal_conv(ug, cwg_ref, cbg_ref, tail_g)
        tail_a = ua[sub - SUBLANES:, :]
        tail_g = ug[sub - SUBLANES:, :]
        act = (yg * (1.0 / (1.0 + jnp.exp(-yg))) * ya).astype(BF16)
        o_ref[r * sub:(r + 1) * sub, :] += jnp.dot(act, wd_ref[...], preferred_element_type=F32)
    tail_a_ref[j] = tail_a
    tail_g_ref[j] = tail_g

    @pl.when(j == pl.num_programs(1) - 1)
    def _():
        def fin(k, carry):
            r = pl.multiple_of(k * NORM_ROWS, NORM_ROWS)
            rows = pl.ds(r, NORM_ROWS)
            xn = x_ref[rows, :] + gate_ref[...] * o_ref[rows, :]
            if final_norm:
                xn = (xn * lax.rsqrt(jnp.mean(xn * xn, axis=-1, keepdims=True) + EPS)) * fg_ref[...]
            o_ref[rows, :] = xn
            return carry
        lax.fori_loop(0, tm // NORM_ROWS, fin, 0)


def _pad_halves(a, d_ff, d_ff_pad):
    pad = [(0, 0)] * (a.ndim - 1) + [(0, d_ff_pad - d_ff)]
    return jnp.concatenate([jnp.pad(a[..., :d_ff], pad), jnp.pad(a[..., d_ff:], pad)], axis=-1)


def _conv_ffn(h, w_up, conv_w, conv_b, w_down, x, gate, final_g, *, final_norm):
    s, d = x.shape
    d_ff = w_down.shape[0]
    tm, tf = FFN_TM, FFN_TF
    d_ff_pad = pl.cdiv(d_ff, tf) * tf
    nj = d_ff_pad // tf
    w_up_p = _pad_halves(w_up, d_ff, d_ff_pad).astype(BF16)
    conv_w_p = _pad_halves(conv_w, d_ff, d_ff_pad)
    conv_b_p = _pad_halves(conv_b.reshape(1, -1), d_ff, d_ff_pad)
    w_down_p = jnp.pad(w_down, ((0, d_ff_pad - d_ff), (0, 0))).astype(BF16)
    vec = pl.BlockSpec((1, d), lambda i, j: (0, 0))
    vmem = (2 * tm * d * 2 + 2 * 2 * d * tf * 2 + 2 * tf * d * 2 + tm * d * 4 + 2 * tm * d * 4
            + 2 * nj * SUBLANES * tf * 4 + 8 * tm * tf * 4)
    return pl.pallas_call(
        functools.partial(_ffn_kernel, final_norm=final_norm),
        name="conv_ffn",
        grid=(s // tm, nj),
        in_specs=[pl.BlockSpec((tm, d), lambda i, j: (i, 0)),
                  pl.BlockSpec((d, tf), lambda i, j: (0, j)),
                  pl.BlockSpec((d, tf), lambda i, j: (0, nj + j)),
                  pl.BlockSpec((CONV_WIDTH, tf), lambda i, j: (0, j)),
                  pl.BlockSpec((CONV_WIDTH, tf), lambda i, j: (0, nj + j)),
                  pl.BlockSpec((1, tf), lambda i, j: (0, j)),
                  pl.BlockSpec((1, tf), lambda i, j: (0, nj + j)),
                  pl.BlockSpec((tf, d), lambda i, j: (j, 0)),
                  pl.BlockSpec((tm, d), lambda i, j: (i, 0), pipeline_mode=pl.Buffered(1)),
                  vec, vec],
        out_specs=pl.BlockSpec((tm, d), lambda i, j: (i, 0)),
        out_shape=jax.ShapeDtypeStruct((s, d), F32),
        scratch_shapes=[pltpu.VMEM((nj, SUBLANES, tf), F32), pltpu.VMEM((nj, SUBLANES, tf), F32)],
        compiler_params=_params(("arbitrary", "arbitrary"), vmem),
    )(h, w_up_p, w_up_p, conv_w_p, conv_w_p, conv_b_p, conv_b_p, w_down_p, x, gate, final_g.reshape(1, d))


def kernel(x, c, norm_mix_g, norm_ffn_g, ada_w, ada_b, w_in_a, w_out_a, w_in_b, w_out_b,
           w_up, conv_w, conv_b, w_down, final_g):
    batch, s, d = x.shape
    assert batch == 1, "the sequence is processed as one (S, D) slab"
    depth = ada_w.shape[0]
    n_heads = w_out_a.shape[1] // HEAD_DIM
    n_groups = len(DILATED_CONFIGS)

    xs = x.reshape(s, d)
    mod = _ada_modulation(c, ada_w, ada_b)
    for i in range(depth):
        sh1, sc1, g1, sh2, sc2, g2 = [mod[i, :, k * d:(k + 1) * d] for k in range(N_MOD)]
        norm_g = norm_mix_g[i].reshape(1, d)
        ffn_g = norm_ffn_g[i].reshape(1, d)
        if i % 2 == 0:
            qkv = _qkv_projection(xs, norm_g, sc1, sh1, w_in_a[i // 2].astype(BF16), layout="residue_major")
            outs, lses = zip(*[_dilated_group_attention(qkv, g, n_heads=n_heads) for g in range(n_groups)])
            w_out = w_out_a[i // 2]
        else:
            qkv = _qkv_projection(xs, norm_g, sc1, sh1, w_in_b[i // 2].astype(BF16), layout="head_major")
            outs, lses = [_stick_breaking_attention(qkv, n_heads=n_heads)], None
            w_out = w_out_b[i // 2]
        xs, h2 = _out_projection(outs, lses, w_out.astype(BF16), xs, g1, ffn_g, sc2, sh2, n_heads=n_heads)
        xs = _conv_ffn(h2, w_up[i], conv_w[i], conv_b[i], w_down[i], xs, g2, final_g,
                       final_norm=(i == depth - 1))
    return xs.reshape(batch, s, d)
```

```python
import functools

import numpy as np
import jax
import jax.numpy as jnp
from jax import lax
from jax.experimental import pallas as pl
from jax.experimental.pallas import tpu as pltpu

HEAD_DIM = 128
DILATED_CONFIGS = ((128, 1), (512, 4), (2048, 16))
ATTN_SPAN_MAX = 128
CONV_WIDTH = 3
EPS = 1e-6
N_MOD = 6
NEG_BIG = -1e30
N_RESIDUES = 16
MID_DILATION = 4

LANES = 128
SUBLANES = 8
BF16_SUBLANES = 16
V7X_VMEM_BYTES = 64 * 1024 * 1024
VMEM_HEADROOM_BYTES = 6 * 1024 * 1024

F32_EXP2_UNDERFLOW = 150.0
LOG2_E = 1.4426950408889634

F32 = jnp.float32
BF16 = jnp.bfloat16


def _params(semantics, vmem_estimate_bytes):
    limit = min(int(vmem_estimate_bytes) + VMEM_HEADROOM_BYTES, V7X_VMEM_BYTES - VMEM_HEADROOM_BYTES)
    return pltpu.CompilerParams(dimension_semantics=semantics, vmem_limit_bytes=limit)


def _modulated_rms_norm(x, g, sc, sh):
    y = x * lax.rsqrt(jnp.mean(x * x, axis=-1, keepdims=True) + EPS)
    return (y * g) * (1.0 + sc) + sh


ADA_TN = 1024
ADA_ROWS = 256


def _ada_kernel(c_ref, w_ref, b_ref, o_ref):
    d = w_ref.shape[1]
    tn = w_ref.shape[2]

    def body(k, acc):
        r = pl.multiple_of(k * ADA_ROWS, ADA_ROWS)
        prod = c_ref[pl.ds(r, ADA_ROWS), :] * w_ref[0, pl.ds(r, ADA_ROWS), :]
        return acc + prod.reshape(ADA_ROWS // SUBLANES, SUBLANES, tn).sum(axis=0)

    acc = lax.fori_loop(0, d // ADA_ROWS, body, jnp.zeros((SUBLANES, tn), F32))
    o_ref[0] = acc.sum(axis=0, keepdims=True) + b_ref[0]


def _ada_modulation(c, ada_w, ada_b):
    depth, d, n = ada_w.shape
    c_col = c.reshape(d, 1)
    return pl.pallas_call(
        _ada_kernel,
        name="ada_modulation",
        grid=(depth, n // ADA_TN),
        in_specs=[
            pl.BlockSpec((d, 1), lambda l, j: (0, 0)),
            pl.BlockSpec((1, d, ADA_TN), lambda l, j: (l, 0, j)),
            pl.BlockSpec((1, 1, ADA_TN), lambda l, j: (l, 0, j)),
        ],
        out_specs=pl.BlockSpec((1, 1, ADA_TN), lambda l, j: (l, 0, j)),
        out_shape=jax.ShapeDtypeStruct((depth, 1, n), F32),
        compiler_params=_params(("arbitrary", "arbitrary"),
                                2 * d * ADA_TN * 4 + d * LANES * 4),
    )(c_col, ada_w, ada_b.reshape(depth, 1, n))


QKV_TM = 1024
QKV_TN = 1024
NORM_ROWS = 64


def _residue_of_slot(slot):
    per = N_RESIDUES // MID_DILATION
    return MID_DILATION * (slot % per) + slot // per


def _residue_major_permutation(rows):
    rho = np.arange(rows)
    rows_per = rows // N_RESIDUES
    src = N_RESIDUES * (rho % rows_per) + _residue_of_slot(rho // rows_per)
    perm = np.zeros((rows, rows), np.float32)
    perm[rho, src] = 1.0
    return perm


def _qkv_kernel(x_ref, g_ref, sc_ref, sh_ref, w_ref, *rest, layout):
    if layout == "residue_major":
        perm_ref, o_ref, h_ref = rest
    else:
        o_ref, h_ref = rest
    tm = x_ref.shape[0]

    @pl.when(pl.program_id(1) == 0)
    def _():
        def body(k, carry):
            r = pl.multiple_of(k * NORM_ROWS, NORM_ROWS)
            h = _modulated_rms_norm(x_ref[pl.ds(r, NORM_ROWS), :], g_ref[...], sc_ref[...], sh_ref[...])
            h_ref[pl.ds(r, NORM_ROWS), :] = h.astype(BF16)
            return carry
        lax.fori_loop(0, tm // NORM_ROWS, body, 0)
        if layout == "residue_major":
            h_ref[...] = jnp.dot(perm_ref[...], h_ref[...], preferred_element_type=F32).astype(BF16)

    res = jnp.dot(h_ref[...], w_ref[...], preferred_element_type=F32)
    if layout == "head_major":
        for cb in range(o_ref.shape[0]):
            o_ref[cb] = res[:, cb * HEAD_DIM:(cb + 1) * HEAD_DIM].astype(BF16)
    else:
        o_ref[...] = res.reshape(o_ref.shape).astype(BF16)


def _qkv_projection(x, g, sc, sh, w, *, layout):
    s, d = x.shape
    n = w.shape[1]
    tm, tn = QKV_TM, QKV_TN
    vec = pl.BlockSpec((1, d), lambda i, j: (0, 0))
    in_specs = [pl.BlockSpec((tm, d), lambda i, j: (i, 0)), vec, vec, vec,
                pl.BlockSpec((d, tn), lambda i, j: (0, j))]
    args = [x, g, sc, sh, w]
    vmem = 2 * tm * d * 4 + tm * d * 2 + 2 * d * tn * 2 + 2 * tm * tn * 2 + tm * tn * 4
    if layout == "head_major":
        out_shape = jax.ShapeDtypeStruct((n // HEAD_DIM, s, HEAD_DIM), BF16)
        out_spec = pl.BlockSpec((tn // HEAD_DIM, tm, HEAD_DIM), lambda i, j: (j, i, 0))
    else:
        assert layout == "residue_major"
        out_shape = jax.ShapeDtypeStruct((N_RESIDUES, s // N_RESIDUES, n), BF16)
        out_spec = pl.BlockSpec((N_RESIDUES, tm // N_RESIDUES, tn), lambda i, j: (0, i, j))
        in_specs.append(pl.BlockSpec((tm, tm), lambda i, j: (0, 0)))
        args.append(jnp.asarray(_residue_major_permutation(tm), BF16))
        vmem += 2 * tm * tm * 2 + tm * d * (4 + 2)
    return pl.pallas_call(
        functools.partial(_qkv_kernel, layout=layout),
        name="qkv_projection",
        grid=(s // tm, n // tn),
        in_specs=in_specs,
        out_specs=out_spec,
        out_shape=out_shape,
        scratch_shapes=[pltpu.VMEM((tm, d), BF16)],
        compiler_params=_params(("arbitrary", "arbitrary"), vmem),
    )(*args)


def _alibi_slopes(n):
    return [float(np.float32(2.0 ** (-8.0 * (i + 1) / n))) for i in range(n)]


def _dilated_kernel(q_ref, kp_ref, kc_ref, vp_ref, vc_ref, dist_ref, o_ref, lse_ref, *, n_heads):
    slots, rows_per, _ = q_ref.shape
    rows = slots * rows_per
    dist = dist_ref[...]
    in_prev = lax.broadcasted_iota(jnp.int32, dist.shape, 1) < rows
    first_block = pl.program_id(1) == 0
    ok = (dist >= 0.0) & jnp.logical_not(in_prev & first_block)
    lane = lax.broadcasted_iota(jnp.int32, (rows, LANES), 1)
    scale = HEAD_DIM ** -0.5
    slopes = _alibi_slopes(n_heads)
    contract_last = (((1,), (1,)), ((), ()))

    def head(ref, h):
        return ref[:, :, h * HEAD_DIM:(h + 1) * HEAD_DIM].reshape(rows, HEAD_DIM)

    scores = []
    for h in range(n_heads):
        keys = jnp.concatenate([head(kp_ref, h), head(kc_ref, h)], axis=0)
        scores.append(lax.dot_general(head(q_ref, h), keys, contract_last, preferred_element_type=F32))

    lse_tile = jnp.zeros((rows, LANES), F32)
    for h in range(n_heads):
        s = jnp.where(ok, scores[h] * scale - slopes[h] * dist, NEG_BIG)
        m = jnp.max(s, axis=-1, keepdims=True)
        p = jnp.exp(s - m)
        denom = jnp.sum(p, axis=-1, keepdims=True)
        values = jnp.concatenate([head(vp_ref, h), head(vc_ref, h)], axis=0)
        acc = jnp.dot(p.astype(BF16), values, preferred_element_type=F32)
        o_ref[:, :, h * HEAD_DIM:(h + 1) * HEAD_DIM] = (
            (acc / denom).reshape(slots, rows_per, HEAD_DIM).astype(o_ref.dtype))
        lse_tile = jnp.where(lane == h, m + jnp.log(denom), lse_tile)
    lse_ref[...] = lse_tile.reshape(lse_ref.shape)


def _window_distances(dilation, span, slots, rows_per):
    rows = slots * rows_per
    rho = np.arange(rows)
    pos = N_RESIDUES * (rho % rows_per) + _residue_of_slot(rho // rows_per)
    m = pos // dilation
    j_cur = m[:, None] - m[None, :]
    j = np.concatenate([j_cur + rows, j_cur], axis=1)
    return jnp.asarray(np.where((j >= 0) & (j <= span), j * dilation, -1).astype(np.float32))


def _dilated_group_attention(qkv, group, *, n_heads):
    window, dilation = DILATED_CONFIGS[group]
    n_res, s_per, _ = qkv.shape
    dm = n_heads * HEAD_DIM
    span = window // dilation
    assert n_res == N_RESIDUES and N_RESIDUES % dilation == 0 and n_heads <= LANES
    assert window % dilation == 0 and span <= ATTN_SPAN_MAX
    slots = N_RESIDUES // dilation
    rows_per = max(ATTN_SPAN_MAX // slots, BF16_SUBLANES)
    rows = slots * rows_per
    assert s_per % rows_per == 0 and rows >= span
    dist = _window_distances(dilation, span, slots, rows_per)
    base = group * 3

    def spec(which, prev):
        def index_map(p, b):
            return (p, jnp.maximum(b - 1, 0) if prev else b, base + which)
        return pl.BlockSpec((slots, rows_per, dm), index_map)

    return pl.pallas_call(
        functools.partial(_dilated_kernel, n_heads=n_heads),
        name=f"dilated_attention_g{group}",
        grid=(N_RESIDUES // slots, s_per // rows_per),
        in_specs=[spec(0, False), spec(1, True), spec(1, False), spec(2, True), spec(2, False),
                  pl.BlockSpec((rows, 2 * rows), lambda p, b: (0, 0))],
        out_specs=[pl.BlockSpec((slots, rows_per, dm), lambda p, b: (p, b, 0)),
                   pl.BlockSpec((slots, rows_per, LANES), lambda p, b: (p, b, 0))],
        out_shape=[jax.ShapeDtypeStruct((N_RESIDUES, s_per, dm), BF16),
                   jax.ShapeDtypeStruct((N_RESIDUES, s_per, LANES), F32)],
        compiler_params=_params(("arbitrary", "arbitrary"),
                                2 * 6 * rows * dm * 2 + 2 * rows * LANES * 4
                                + (4 + 2 * n_heads) * rows * 2 * rows * 4),
    )(qkv, qkv, qkv, qkv, qkv, dist)


SB_TQ = 1024
SB_SUB = 128
SB_BLK = 256
SB_NEAR = 512
SB_GROUP = 4


def _sb_kernel(q_ref, k_ref, v_ref, tri2_ref, o_ref, acc_ref, c_ref):
    tq = q_ref.shape[1]
    sub, blk, near = SB_SUB, SB_BLK, SB_NEAR
    n_sub = tq // sub
    i = pl.program_id(1)
    to_log2 = HEAD_DIM ** -0.5 * LOG2_E
    contract_last = (((1,), (1,)), ((), ()))

    def masked_scores(r, kstart, width, limit, *, causal):
        q = q_ref[0, r * sub:(r + 1) * sub, :]
        k = k_ref[0, pl.ds(kstart, width), :]
        y = lax.dot_general(q, k, contract_last, preferred_element_type=F32) * to_log2
        col = lax.broadcasted_iota(jnp.int32, (sub, width), 1)
        if causal:
            col = col - lax.broadcasted_iota(jnp.int32, (sub, width), 0)
        return jnp.where(col < limit, y, NEG_BIG)

    def log_terms(y):
        neg_abs = pltpu.bitcast(pltpu.bitcast(y, jnp.uint32) | jnp.uint32(0x80000000), F32)
        log_beta = jnp.minimum(y, 0.0) - jnp.log2(1.0 + jnp.exp2(neg_abs))
        log_keep = log_beta - y
        return log_beta, log_keep

    def staged_log_terms(ys):
        n_blk = ys[0].shape[1] // blk
        stage = []
        for g in range(0, len(ys), SB_GROUP):
            terms, lhs = [], []
            for y in ys[g:g + SB_GROUP]:
                log_beta, log_keep = log_terms(y)
                hi = log_keep.astype(BF16)
                lo = (log_keep - hi.astype(F32)).astype(BF16)
                lhs += [jnp.concatenate([hi[:, b * blk:(b + 1) * blk], lo[:, b * blk:(b + 1) * blk]], axis=1)
                        for b in range(n_blk)]
                terms.append((log_beta, log_keep))
            local = jnp.dot(jnp.concatenate(lhs, axis=0), tri2_ref[...], preferred_element_type=F32)
            rows = n_blk * sub
            stage += [(lb, lk, local[k * rows:(k + 1) * rows, :]) for k, (lb, lk) in enumerate(terms)]
        return stage

    n_blk = near // blk
    kstarts, ys = [], []
    for r in range(n_sub):
        q_start = i * tq + r * sub
        kstart = pl.multiple_of(jnp.maximum(q_start + sub - near, 0), sub)
        kstarts.append(kstart)
        ys.append(masked_scores(r, kstart, near, q_start - kstart, causal=True))
    stage = staged_log_terms(ys)
    for r in range(n_sub):
        log_beta, log_keep, local = stage[r]
        newer = jnp.zeros((sub, 1), F32)
        after = [None] * n_blk
        for b in reversed(range(n_blk)):
            loc = local[b * sub:(b + 1) * sub, :]
            after[b] = loc + newer if b < n_blk - 1 else loc
            newer = newer + (loc[:, 0:1] + log_keep[:, b * blk:b * blk + 1])
        a = jnp.exp2(log_beta + jnp.concatenate(after, axis=1))
        v = v_ref[0, pl.ds(kstarts[r], near), :]
        acc_ref[r * sub:(r + 1) * sub, :] = jnp.dot(a.astype(BF16), v, preferred_element_type=F32)
        c_ref[r * sub:(r + 1) * sub, :] = newer

    def cond(carry):
        m, c_max = carry
        return (kstarts[n_sub - 1] - m * blk > 0) & (c_max > -F32_EXP2_UNDERFLOW)

    def body(carry):
        m, _ = carry
        wins, ys = [], []
        for r in range(n_sub):
            kend = kstarts[r] - m * blk
            wstart = pl.multiple_of(jnp.maximum(kend - blk, 0), sub)
            wins.append(wstart)
            ys.append(masked_scores(r, wstart, blk, kend - wstart, causal=False))
        stage = staged_log_terms(ys)
        c_old = c_ref[...]
        outs, c_news = [], []
        for r in range(n_sub):
            log_beta, log_keep, local = stage[r]
            after = local + c_old[r * sub:(r + 1) * sub, :]
            a = jnp.exp2(log_beta + after)
            v = v_ref[0, pl.ds(wins[r], blk), :]
            outs.append(jnp.dot(a.astype(BF16), v, preferred_element_type=F32))
            c_news.append(after[:, 0:1] + log_keep[:, 0:1])
        c_new = jnp.concatenate(c_news, axis=0)
        acc_ref[...] += jnp.concatenate(outs, axis=0)
        c_ref[...] = c_new
        return m + 1, jnp.max(c_new)

    lax.while_loop(cond, body, (0, jnp.max(c_ref[...])))
    o_ref[...] = acc_ref[...].astype(o_ref.dtype)


def _stick_breaking_attention(qkv, *, n_heads):
    _, s, _ = qkv.shape
    tq, sub, blk, near = SB_TQ, SB_SUB, SB_BLK, SB_NEAR
    assert s % tq == 0 and tq % sub == 0 and near % blk == 0 and blk % sub == 0 and s >= near
    tri = np.tril(np.ones((blk, blk), np.float32), k=-1)
    tri2 = jnp.asarray(np.concatenate([tri, tri], axis=0), BF16)
    vmem = (2 * 2 * s * HEAD_DIM * 2 + 2 * 2 * blk * blk * 2 + 4 * tq * HEAD_DIM * 2
            + tq * LANES * 4 * 2 + 10 * tq * near * 4)
    return pl.pallas_call(
        _sb_kernel,
        name="stick_breaking_attention",
        grid=(n_heads, s // tq),
        in_specs=[pl.BlockSpec((1, tq, HEAD_DIM), lambda h, i: (h, i, 0)),
                  pl.BlockSpec((1, s, HEAD_DIM), lambda h, i: (n_heads + h, 0, 0)),
                  pl.BlockSpec((1, s, HEAD_DIM), lambda h, i: (2 * n_heads + h, 0, 0)),
                  pl.BlockSpec((2 * blk, blk), lambda h, i: (0, 0))],
        out_specs=pl.BlockSpec((tq, HEAD_DIM), lambda h, i: (i, h)),
        out_shape=jax.ShapeDtypeStruct((s, n_heads * HEAD_DIM), BF16),
        scratch_shapes=[pltpu.VMEM((tq, HEAD_DIM), F32), pltpu.VMEM((tq, 1), F32)],
        compiler_params=_params(("arbitrary", "arbitrary"), vmem),
    )(qkv, qkv, qkv, tri2)


OUT_TM = 512
OUT_HEAD_GROUPS = 4


def _out_kernel(*refs, n_groups, n_heads):
    o_refs = refs[:n_groups]
    lse_refs = refs[n_groups:2 * n_groups] if n_groups > 1 else ()
    rest = refs[len(o_refs) + len(lse_refs):]
    w_ref, x_ref, gate_ref, g_ref, sc_ref, sh_ref = rest[:6]
    if n_groups == 1:
        xo_ref, h_ref = rest[6:]
    else:
        unperm_ref, xo_ref, h_ref, merged_ref = rest[6:]
    tm = x_ref.shape[0]

    if n_groups == 1:
        y = jnp.dot(o_refs[0][...], w_ref[...], preferred_element_type=F32)
    else:
        rows_per = tm // N_RESIDUES
        wts = []
        for k in range(N_RESIDUES):
            lses = [lr[k] for lr in lse_refs]
            m = functools.reduce(jnp.maximum, lses)
            es = [jnp.exp(l - m) for l in lses]
            inv = 1.0 / functools.reduce(jnp.add, es)
            wts.append([e * inv for e in es[1:]])
        heads_per = n_heads // OUT_HEAD_GROUPS
        y = None
        for c in range(OUT_HEAD_GROUPS):
            for k in range(N_RESIDUES):
                rows = slice(k * rows_per, (k + 1) * rows_per)
                for h in range(c * heads_per, (c + 1) * heads_per):
                    cols = slice(h * HEAD_DIM, (h + 1) * HEAD_DIM)
                    base = o_refs[0][k, :, cols].astype(F32)
                    acc = base
                    for g in range(1, n_groups):
                        acc = acc + wts[k][g - 1][:, h:h + 1] * (o_refs[g][k, :, cols].astype(F32) - base)
                    merged_ref[rows, cols] = acc.astype(BF16)
            gcols = slice(c * heads_per * HEAD_DIM, (c + 1) * heads_per * HEAD_DIM)
            mixed = jnp.dot(unperm_ref[...], merged_ref[:, gcols], preferred_element_type=F32).astype(BF16)
            part = jnp.dot(mixed, w_ref[gcols, :], preferred_element_type=F32)
            y = part if y is None else y + part

    xo_ref[...] = x_ref[...] + gate_ref[...] * y

    def norm(k, carry):
        r = pl.multiple_of(k * NORM_ROWS, NORM_ROWS)
        h = _modulated_rms_norm(xo_ref[pl.ds(r, NORM_ROWS), :], g_ref[...], sc_ref[...], sh_ref[...])
        h_ref[pl.ds(r, NORM_ROWS), :] = h.astype(BF16)
        return carry
    lax.fori_loop(0, tm // NORM_ROWS, norm, 0, unroll=2)


def _out_projection(outs, lses, w, x, gate, g, sc, sh, *, n_heads):
    s, d = x.shape
    dm = w.shape[0]
    n_groups = len(outs)
    tm = OUT_TM
    row_d = pl.BlockSpec((tm, d), lambda i: (i, 0))
    vec = pl.BlockSpec((1, d), lambda i: (0, 0))
    scratch = []
    if n_groups == 1:
        in_specs = [pl.BlockSpec((tm, dm), lambda i: (i, 0))]
        args = list(outs)
    else:
        rows_per = tm // N_RESIDUES
        in_specs = ([pl.BlockSpec((N_RESIDUES, rows_per, dm), lambda i: (0, i, 0))] * n_groups
                    + [pl.BlockSpec((N_RESIDUES, rows_per, LANES), lambda i: (0, i, 0))] * n_groups)
        args = list(outs) + list(lses)
        scratch = [pltpu.VMEM((tm, dm), BF16)]
    in_specs += [pl.BlockSpec((dm, d), lambda i: (0, 0)), row_d, vec, vec, vec, vec]
    args += [w, x, gate, g, sc, sh]
    if n_groups > 1:
        in_specs.append(pl.BlockSpec((tm, tm), lambda i: (0, 0)))
        args.append(jnp.asarray(_residue_major_permutation(tm).T, BF16))
    vmem = (2 * n_groups * tm * dm * 2 + 2 * dm * d * 2 + 2 * tm * d * 4 + 2 * tm * d * 4
            + 2 * tm * d * 2 + tm * dm * 2 + tm * d * 4)
    return pl.pallas_call(
        functools.partial(_out_kernel, n_groups=n_groups, n_heads=n_heads),
        name="mixer_out_projection",
        grid=(s // tm,),
        in_specs=in_specs,
        out_specs=[row_d, row_d],
        out_shape=[jax.ShapeDtypeStruct((s, d), F32), jax.ShapeDtypeStruct((s, d), BF16)],
        scratch_shapes=scratch,
        compiler_params=_params(("arbitrary",), vmem),
    )(*args)


FFN_TM = 1024
FFN_TF = 512
FFN_ROW_SPLIT = 2


def _causal_conv(u_ref, row0, rows, cw_ref, cb_ref):
    def at(shift):
        return u_ref[SUBLANES + row0 - shift:SUBLANES + row0 - shift + rows, :]

    y = cb_ref[...] + cw_ref[0:1, :] * at(2)
    y = y + cw_ref[1:2, :] * at(1)
    return y + cw_ref[2:3, :] * at(0)


def _ffn_kernel(h_ref, wa_ref, wg_ref, cwa_ref, cwg_ref, cba_ref, cbg_ref, wd_ref, x_ref, gate_ref,
                fg_ref, o_ref, tail_a_ref, tail_g_ref, ua_ref, ug_ref, *, final_norm):
    i = pl.program_id(0)
    j = pl.program_id(1)
    tm = h_ref.shape[0]

    @pl.when(i == 0)
    def _():
        tail_a_ref[j] = jnp.zeros(tail_a_ref.shape[1:], F32)
        tail_g_ref[j] = jnp.zeros(tail_g_ref.shape[1:], F32)

    @pl.when(j == 0)
    def _():
        o_ref[...] = jnp.zeros_like(o_ref)

    ua_ref[0:SUBLANES, :] = tail_a_ref[j]
    ug_ref[0:SUBLANES, :] = tail_g_ref[j]

    sub = tm // FFN_ROW_SPLIT
    for r in range(FFN_ROW_SPLIT):
        h = h_ref[r * sub:(r + 1) * sub, :]
        rows = slice(SUBLANES + r * sub, SUBLANES + (r + 1) * sub)
        ua_ref[rows, :] = jnp.dot(h, wa_ref[...], preferred_element_type=F32)
        ug_ref[rows, :] = jnp.dot(h, wg_ref[...], preferred_element_type=F32)
    for r in range(FFN_ROW_SPLIT):
        ya = _causal_conv(ua_ref, r * sub, sub, cwa_ref, cba_ref)
        yg = _causal_conv(ug_ref, r * sub, sub, cwg_ref, cbg_ref)
        act = (yg * (1.0 / (1.0 + jnp.exp(-yg))) * ya).astype(BF16)
        o_ref[r * sub:(r + 1) * sub, :] += jnp.dot(act, wd_ref[...], preferred_element_type=F32)
    tail_a_ref[j] = ua_ref[tm:tm + SUBLANES, :]
    tail_g_ref[j] = ug_ref[tm:tm + SUBLANES, :]

    @pl.when(j == pl.num_programs(1) - 1)
    def _():
        def fin(k, carry):
            r = pl.multiple_of(k * NORM_ROWS, NORM_ROWS)
            rows = pl.ds(r, NORM_ROWS)
            xn = x_ref[rows, :] + gate_ref[...] * o_ref[rows, :]
            if final_norm:
                xn = (xn * lax.rsqrt(jnp.mean(xn * xn, axis=-1, keepdims=True) + EPS)) * fg_ref[...]
            o_ref[rows, :] = xn
            return carry
        lax.fori_loop(0, tm // NORM_ROWS, fin, 0)


def _pad_halves(a, d_ff, d_ff_pad):
    pad = [(0, 0)] * (a.ndim - 1) + [(0, d_ff_pad - d_ff)]
    return jnp.concatenate([jnp.pad(a[..., :d_ff], pad), jnp.pad(a[..., d_ff:], pad)], axis=-1)


PREP_ROWS = 256
PREP_COLS = 256
PREP_CHUNK = 32


def _prep_up_kernel(w_ref, o_ref, *, d_ff, d_ff_pad):
    def body(k, carry):
        rows = pl.ds(pl.multiple_of(k * PREP_CHUNK, PREP_CHUNK), PREP_CHUNK)
        for half in range(2):
            o_ref[0, rows, half * d_ff_pad:half * d_ff_pad + d_ff] = (
                w_ref[0, rows, half * d_ff:(half + 1) * d_ff].astype(BF16))
            if d_ff_pad > d_ff:
                o_ref[0, rows, half * d_ff_pad + d_ff:(half + 1) * d_ff_pad] = (
                    jnp.zeros((PREP_CHUNK, d_ff_pad - d_ff), BF16))
        return carry
    lax.fori_loop(0, w_ref.shape[1] // PREP_CHUNK, body, 0)


def _prep_down_kernel(w_ref, o_ref, *, d_ff, d_ff_pad):
    def body(k, carry):
        rows = pl.ds(pl.multiple_of(k * PREP_CHUNK, PREP_CHUNK), PREP_CHUNK)
        o_ref[0, rows, :] = w_ref[0, rows, :].astype(BF16)
        return carry
    lax.fori_loop(0, d_ff // PREP_CHUNK, body, 0)
    if d_ff_pad > d_ff:
        o_ref[0, d_ff:, :] = jnp.zeros((d_ff_pad - d_ff, o_ref.shape[2]), BF16)


def _prep_ffn_weights(w_up, w_down):
    depth, d, _ = w_up.shape
    d_ff = w_down.shape[1]
    d_ff_pad = pl.cdiv(d_ff, FFN_TF) * FFN_TF
    assert d_ff % LANES == 0 and d_ff % PREP_CHUNK == 0 and d % PREP_ROWS == 0 and d % PREP_COLS == 0
    w_up_p = pl.pallas_call(
        functools.partial(_prep_up_kernel, d_ff=d_ff, d_ff_pad=d_ff_pad),
        name="prep_w_up",
        grid=(depth, d // PREP_ROWS),
        in_specs=[pl.BlockSpec((1, PREP_ROWS, 2 * d_ff), lambda l, i: (l, i, 0))],
        out_specs=pl.BlockSpec((1, PREP_ROWS, 2 * d_ff_pad), lambda l, i: (l, i, 0)),
        out_shape=jax.ShapeDtypeStruct((depth, d, 2 * d_ff_pad), BF16),
        compiler_params=_params(("arbitrary", "arbitrary"),
                                2 * PREP_ROWS * 2 * (d_ff * 4 + d_ff_pad * 2)),
    )(w_up)
    w_down_p = pl.pallas_call(
        functools.partial(_prep_down_kernel, d_ff=d_ff, d_ff_pad=d_ff_pad),
        name="prep_w_down",
        grid=(depth, d // PREP_COLS),
        in_specs=[pl.BlockSpec((1, d_ff, PREP_COLS), lambda l, i: (l, 0, i))],
        out_specs=pl.BlockSpec((1, d_ff_pad, PREP_COLS), lambda l, i: (l, 0, i)),
        out_shape=jax.ShapeDtypeStruct((depth, d_ff_pad, d), BF16),
        compiler_params=_params(("arbitrary", "arbitrary"),
                                2 * PREP_COLS * (d_ff * 4 + d_ff_pad * 2)),
    )(w_down)
    return w_up_p, w_down_p


def _conv_ffn(h, layer, w_up_p, conv_w, conv_b, w_down_p, x, gate, final_g, *, final_norm):
    s, d = x.shape
    d_ff = conv_w.shape[1] // 2
    tm, tf = FFN_TM, FFN_TF
    d_ff_pad = w_down_p.shape[1]
    nj = d_ff_pad // tf
    conv_w_p = _pad_halves(conv_w, d_ff, d_ff_pad)
    conv_b_p = _pad_halves(conv_b.reshape(1, -1), d_ff, d_ff_pad)
    vec = pl.BlockSpec((1, d), lambda i, j: (0, 0))
    vmem = (2 * tm * d * 2 + 2 * 2 * d * tf * 2 + 2 * tf * d * 2 + tm * d * 4 + 2 * tm * d * 4
            + 2 * nj * SUBLANES * tf * 4 + 8 * tm * tf * 4)
    return pl.pallas_call(
        functools.partial(_ffn_kernel, final_norm=final_norm),
        name="conv_ffn",
        grid=(s // tm, nj),
        in_specs=[pl.BlockSpec((tm, d), lambda i, j: (i, 0)),
                  pl.BlockSpec((None, d, tf), lambda i, j: (layer, 0, j)),
                  pl.BlockSpec((None, d, tf), lambda i, j: (layer, 0, nj + j)),
                  pl.BlockSpec((CONV_WIDTH, tf), lambda i, j: (0, j)),
                  pl.BlockSpec((CONV_WIDTH, tf), lambda i, j: (0, nj + j)),
                  pl.BlockSpec((1, tf), lambda i, j: (0, j)),
                  pl.BlockSpec((1, tf), lambda i, j: (0, nj + j)),
                  pl.BlockSpec((None, tf, d), lambda i, j: (layer, j, 0)),
                  pl.BlockSpec((tm, d), lambda i, j: (i, 0), pipeline_mode=pl.Buffered(1)),
                  vec, vec],
        out_specs=pl.BlockSpec((tm, d), lambda i, j: (i, 0)),
        out_shape=jax.ShapeDtypeStruct((s, d), F32),
        scratch_shapes=[pltpu.VMEM((nj, SUBLANES, tf), F32), pltpu.VMEM((nj, SUBLANES, tf), F32),
                        pltpu.VMEM((SUBLANES + tm, tf), F32), pltpu.VMEM((SUBLANES + tm, tf), F32)],
        compiler_params=_params(("arbitrary", "arbitrary"), vmem),
    )(h, w_up_p, w_up_p, conv_w_p, conv_w_p, conv_b_p, conv_b_p, w_down_p, x, gate, final_g.reshape(1, d))


def kernel(x, c, norm_mix_g, norm_ffn_g, ada_w, ada_b, w_in_a, w_out_a, w_in_b, w_out_b,
           w_up, conv_w, conv_b, w_down, final_g):
    batch, s, d = x.shape
    assert batch == 1, "the sequence is processed as one (S, D) slab"
    depth = ada_w.shape[0]
    n_heads = w_out_a.shape[1] // HEAD_DIM
    n_groups = len(DILATED_CONFIGS)

    xs = x.reshape(s, d)
    mod = _ada_modulation(c, ada_w, ada_b)
    w_up_p, w_down_p = _prep_ffn_weights(w_up, w_down)
    for i in range(depth):
        sh1, sc1, g1, sh2, sc2, g2 = [mod[i, :, k * d:(k + 1) * d] for k in range(N_MOD)]
        norm_g = norm_mix_g[i].reshape(1, d)
        ffn_g = norm_ffn_g[i].reshape(1, d)
        if i % 2 == 0:
            qkv = _qkv_projection(xs, norm_g, sc1, sh1, w_in_a[i // 2].astype(BF16), layout="residue_major")
            outs, lses = zip(*[_dilated_group_attention(qkv, g, n_heads=n_heads) for g in range(n_groups)])
            w_out = w_out_a[i // 2]
        else:
            qkv = _qkv_projection(xs, norm_g, sc1, sh1, w_in_b[i // 2].astype(BF16), layout="head_major")
            outs, lses = [_stick_breaking_attention(qkv, n_heads=n_heads)], None
            w_out = w_out_b[i // 2]
        xs, h2 = _out_projection(outs, lses, w_out.astype(BF16), xs, g1, ffn_g, sc2, sh2, n_heads=n_heads)
        xs = _conv_ffn(h2, i, w_up_p, conv_w[i], conv_b[i], w_down_p, xs, g2, final_g,
                       final_norm=(i == depth - 1))
    return xs.reshape(batch, s, d)
```

```python
import functools

import numpy as np
import jax
import jax.numpy as jnp
from jax import lax
from jax.experimental import pallas as pl
from jax.experimental.pallas import tpu as pltpu

HEAD_DIM = 128
DILATED_CONFIGS = ((128, 1), (512, 4), (2048, 16))
ATTN_SPAN_MAX = 128
CONV_WIDTH = 3
EPS = 1e-6
N_MOD = 6
NEG_BIG = -1e30
N_RESIDUES = 16
MID_DILATION = 4

LANES = 128
SUBLANES = 8
BF16_SUBLANES = 16
V7X_VMEM_BYTES = 64 * 1024 * 1024
VMEM_HEADROOM_BYTES = 6 * 1024 * 1024

F32_EXP2_UNDERFLOW = 150.0
LOG2_E = 1.4426950408889634
LN_2 = 0.6931471805599453

F32 = jnp.float32
BF16 = jnp.bfloat16


def _params(semantics, vmem_estimate_bytes):
    limit = min(int(vmem_estimate_bytes) + VMEM_HEADROOM_BYTES, V7X_VMEM_BYTES - VMEM_HEADROOM_BYTES)
    return pltpu.CompilerParams(dimension_semantics=semantics, vmem_limit_bytes=limit)


def _modulated_rms_norm(x, g, sc, sh):
    y = x * lax.rsqrt(jnp.mean(x * x, axis=-1, keepdims=True) + EPS)
    return (y * g) * (1.0 + sc) + sh


ADA_TN = 1024
ADA_ROWS = 256


def _ada_kernel(c_ref, w_ref, b_ref, o_ref):
    d = w_ref.shape[1]
    tn = w_ref.shape[2]

    def body(k, acc):
        r = pl.multiple_of(k * ADA_ROWS, ADA_ROWS)
        prod = c_ref[pl.ds(r, ADA_ROWS), :] * w_ref[0, pl.ds(r, ADA_ROWS), :]
        return acc + prod.reshape(ADA_ROWS // SUBLANES, SUBLANES, tn).sum(axis=0)

    acc = lax.fori_loop(0, d // ADA_ROWS, body, jnp.zeros((SUBLANES, tn), F32))
    o_ref[0] = acc.sum(axis=0, keepdims=True) + b_ref[0]


def _ada_modulation(c, ada_w, ada_b):
    depth, d, n = ada_w.shape
    c_col = c.reshape(d, 1)
    return pl.pallas_call(
        _ada_kernel,
        name="ada_modulation",
        grid=(depth, n // ADA_TN),
        in_specs=[
            pl.BlockSpec((d, 1), lambda l, j: (0, 0)),
            pl.BlockSpec((1, d, ADA_TN), lambda l, j: (l, 0, j)),
            pl.BlockSpec((1, 1, ADA_TN), lambda l, j: (l, 0, j)),
        ],
        out_specs=pl.BlockSpec((1, 1, ADA_TN), lambda l, j: (l, 0, j)),
        out_shape=jax.ShapeDtypeStruct((depth, 1, n), F32),
        compiler_params=_params(("arbitrary", "arbitrary"),
                                2 * d * ADA_TN * 4 + d * LANES * 4),
    )(c_col, ada_w, ada_b.reshape(depth, 1, n))


QKV_TM = 1024
QKV_TN = 1024
NORM_ROWS = 64


def _residue_of_slot(slot):
    per = N_RESIDUES // MID_DILATION
    return MID_DILATION * (slot % per) + slot // per


def _residue_major_permutation(rows):
    rho = np.arange(rows)
    rows_per = rows // N_RESIDUES
    src = N_RESIDUES * (rho % rows_per) + _residue_of_slot(rho // rows_per)
    perm = np.zeros((rows, rows), np.float32)
    perm[rho, src] = 1.0
    return perm


def _qkv_kernel(x_ref, g_ref, sc_ref, sh_ref, w_ref, *rest, layout):
    if layout == "residue_major":
        perm_ref, o_ref, h_ref = rest
    else:
        o_ref, h_ref = rest
    tm = x_ref.shape[0]

    @pl.when(pl.program_id(1) == 0)
    def _():
        def body(k, carry):
            r = pl.multiple_of(k * NORM_ROWS, NORM_ROWS)
            h = _modulated_rms_norm(x_ref[pl.ds(r, NORM_ROWS), :], g_ref[...], sc_ref[...], sh_ref[...])
            h_ref[pl.ds(r, NORM_ROWS), :] = h.astype(BF16)
            return carry
        lax.fori_loop(0, tm // NORM_ROWS, body, 0)
        if layout == "residue_major":
            h_ref[...] = jnp.dot(perm_ref[...], h_ref[...], preferred_element_type=F32).astype(BF16)

    res = jnp.dot(h_ref[...], w_ref[...], preferred_element_type=F32)
    if layout == "head_major":
        for cb in range(o_ref.shape[0]):
            o_ref[cb] = res[:, cb * HEAD_DIM:(cb + 1) * HEAD_DIM].astype(BF16)
    else:
        o_ref[...] = res.reshape(o_ref.shape).astype(BF16)


def _qkv_projection(x, g, sc, sh, w, *, layout):
    s, d = x.shape
    n = w.shape[1]
    tm, tn = QKV_TM, QKV_TN
    vec = pl.BlockSpec((1, d), lambda i, j: (0, 0))
    in_specs = [pl.BlockSpec((tm, d), lambda i, j: (i, 0)), vec, vec, vec,
                pl.BlockSpec((d, tn), lambda i, j: (0, j))]
    args = [x, g, sc, sh, w]
    vmem = 2 * tm * d * 4 + tm * d * 2 + 2 * d * tn * 2 + 2 * tm * tn * 2 + tm * tn * 4
    if layout == "head_major":
        out_shape = jax.ShapeDtypeStruct((n // HEAD_DIM, s, HEAD_DIM), BF16)
        out_spec = pl.BlockSpec((tn // HEAD_DIM, tm, HEAD_DIM), lambda i, j: (j, i, 0))
    else:
        assert layout == "residue_major"
        out_shape = jax.ShapeDtypeStruct((N_RESIDUES, s // N_RESIDUES, n), BF16)
        out_spec = pl.BlockSpec((N_RESIDUES, tm // N_RESIDUES, tn), lambda i, j: (0, i, j))
        in_specs.append(pl.BlockSpec((tm, tm), lambda i, j: (0, 0)))
        args.append(jnp.asarray(_residue_major_permutation(tm), BF16))
        vmem += 2 * tm * tm * 2 + tm * d * (4 + 2)
    return pl.pallas_call(
        functools.partial(_qkv_kernel, layout=layout),
        name="qkv_projection",
        grid=(s // tm, n // tn),
        in_specs=in_specs,
        out_specs=out_spec,
        out_shape=out_shape,
        scratch_shapes=[pltpu.VMEM((tm, d), BF16)],
        compiler_params=_params(("arbitrary", "arbitrary"), vmem),
    )(*args)


def _alibi_slopes(n):
    return [float(np.float32(2.0 ** (-8.0 * (i + 1) / n))) for i in range(n)]


def _dilated_kernel(q_ref, kp_ref, kc_ref, vp_ref, vc_ref, bias_ref, o_ref, lse_ref, *, n_heads):
    slots, rows_per, _ = q_ref.shape
    rows = slots * rows_per
    first_block = (pl.program_id(1) == 0).astype(jnp.int32)
    lane = lax.broadcasted_iota(jnp.int32, (rows, LANES), 1)
    to_log2 = HEAD_DIM ** -0.5 * LOG2_E
    contract_last = (((1,), (1,)), ((), ()))

    def head(ref, h):
        return ref[:, :, h * HEAD_DIM:(h + 1) * HEAD_DIM].reshape(rows, HEAD_DIM)

    scores = []
    for h in range(n_heads):
        keys = jnp.concatenate([head(kp_ref, h), head(kc_ref, h)], axis=0)
        scores.append(lax.dot_general(head(q_ref, h), keys, contract_last, preferred_element_type=F32))

    max_tile = jnp.zeros((rows, LANES), F32)
    den_tile = jnp.ones((rows, LANES), F32)
    for h in range(n_heads):
        s = scores[h] * to_log2 + bias_ref[first_block, h]
        m = jnp.max(s, axis=-1, keepdims=True)
        p = jnp.exp2(s - m)
        denom = jnp.sum(p, axis=-1, keepdims=True)
        values = jnp.concatenate([head(vp_ref, h), head(vc_ref, h)], axis=0)
        acc = jnp.dot(p.astype(BF16), values, preferred_element_type=F32)
        o_ref[:, :, h * HEAD_DIM:(h + 1) * HEAD_DIM] = (
            (acc / denom).reshape(slots, rows_per, HEAD_DIM).astype(o_ref.dtype))
        max_tile = jnp.where(lane == h, m, max_tile)
        den_tile = jnp.where(lane == h, denom, den_tile)
    lse_ref[...] = ((max_tile + jnp.log2(den_tile)) * LN_2).reshape(lse_ref.shape)


def _alibi_bias_tables(dilation, span, slots, rows_per, n_heads):
    rows = slots * rows_per
    rho = np.arange(rows)
    pos = N_RESIDUES * (rho % rows_per) + _residue_of_slot(rho // rows_per)
    m = pos // dilation
    j_cur = m[:, None] - m[None, :]
    j = np.concatenate([j_cur + rows, j_cur], axis=1)
    valid = (j >= 0) & (j <= span)
    valid = jnp.asarray(np.stack([valid, valid & (np.arange(2 * rows) >= rows)[None, :]]))
    dist = jnp.asarray((j * dilation).astype(np.float32))
    slopes = jnp.asarray(_alibi_slopes(n_heads), F32)
    bias = -(slopes[:, None, None] * dist[None]) * LOG2_E
    return jnp.where(valid[:, None], bias[None], NEG_BIG)


def _dilated_group_attention(qkv, group, *, n_heads):
    window, dilation = DILATED_CONFIGS[group]
    n_res, s_per, _ = qkv.shape
    dm = n_heads * HEAD_DIM
    span = window // dilation
    assert n_res == N_RESIDUES and N_RESIDUES % dilation == 0 and n_heads <= LANES
    assert window % dilation == 0 and span <= ATTN_SPAN_MAX
    slots = N_RESIDUES // dilation
    rows_per = max(ATTN_SPAN_MAX // slots, BF16_SUBLANES)
    rows = slots * rows_per
    assert s_per % rows_per == 0 and rows >= span
    bias = _alibi_bias_tables(dilation, span, slots, rows_per, n_heads)
    base = group * 3

    def spec(which, prev):
        def index_map(p, b):
            return (p, jnp.maximum(b - 1, 0) if prev else b, base + which)
        return pl.BlockSpec((slots, rows_per, dm), index_map)

    return pl.pallas_call(
        functools.partial(_dilated_kernel, n_heads=n_heads),
        name=f"dilated_attention_g{group}",
        grid=(N_RESIDUES // slots, s_per // rows_per),
        in_specs=[spec(0, False), spec(1, True), spec(1, False), spec(2, True), spec(2, False),
                  pl.BlockSpec((2, n_heads, rows, 2 * rows), lambda p, b: (0, 0, 0, 0),
                               pipeline_mode=pl.Buffered(1))],
        out_specs=[pl.BlockSpec((slots, rows_per, dm), lambda p, b: (p, b, 0)),
                   pl.BlockSpec((slots, rows_per, LANES), lambda p, b: (p, b, 0))],
        out_shape=[jax.ShapeDtypeStruct((N_RESIDUES, s_per, dm), BF16),
                   jax.ShapeDtypeStruct((N_RESIDUES, s_per, LANES), F32)],
        compiler_params=_params(("arbitrary", "arbitrary"),
                                2 * 6 * rows * dm * 2 + 2 * rows * LANES * 4
                                + (2 + 4 * n_heads) * rows * 2 * rows * 4),
    )(qkv, qkv, qkv, qkv, qkv, bias)


SB_TQ = 1024
SB_SUB = 128
SB_BLK = 256
SB_NEAR = 512
SB_GROUP = 4


def _sb_kernel(q_ref, k_ref, v_ref, tri2_ref, o_ref, acc_ref, c_ref):
    tq = q_ref.shape[1]
    sub, blk, near = SB_SUB, SB_BLK, SB_NEAR
    n_sub = tq // sub
    i = pl.program_id(1)
    to_log2 = HEAD_DIM ** -0.5 * LOG2_E
    contract_last = (((1,), (1,)), ((), ()))

    def masked_scores(r, kstart, width, limit, *, causal):
        q = q_ref[0, r * sub:(r + 1) * sub, :]
        k = k_ref[0, pl.ds(kstart, width), :]
        y = lax.dot_general(q, k, contract_last, preferred_element_type=F32) * to_log2
        col = lax.broadcasted_iota(jnp.int32, (sub, width), 1)
        if causal:
            col = col - lax.broadcasted_iota(jnp.int32, (sub, width), 0)
        return jnp.where(col < limit, y, NEG_BIG)

    def log_terms(y):
        neg_abs = pltpu.bitcast(pltpu.bitcast(y, jnp.uint32) | jnp.uint32(0x80000000), F32)
        log_beta = jnp.minimum(y, 0.0) - jnp.log2(1.0 + jnp.exp2(neg_abs))
        log_keep = log_beta - y
        return log_beta, log_keep

    def staged_log_terms(ys):
        n_blk = ys[0].shape[1] // blk
        stage = []
        for g in range(0, len(ys), SB_GROUP):
            terms, lhs = [], []
            for y in ys[g:g + SB_GROUP]:
                log_beta, log_keep = log_terms(y)
                hi = log_keep.astype(BF16)
                lo = (log_keep - hi.astype(F32)).astype(BF16)
                lhs += [jnp.concatenate([hi[:, b * blk:(b + 1) * blk], lo[:, b * blk:(b + 1) * blk]], axis=1)
                        for b in range(n_blk)]
                terms.append((log_beta, log_keep))
            local = jnp.dot(jnp.concatenate(lhs, axis=0), tri2_ref[...], preferred_element_type=F32)
            rows = n_blk * sub
            stage += [(lb, lk, local[k * rows:(k + 1) * rows, :]) for k, (lb, lk) in enumerate(terms)]
        return stage

    n_blk = near // blk
    kstarts, ys = [], []
    for r in range(n_sub):
        q_start = i * tq + r * sub
        kstart = pl.multiple_of(jnp.maximum(q_start + sub - near, 0), sub)
        kstarts.append(kstart)
        ys.append(masked_scores(r, kstart, near, q_start - kstart, causal=True))
    stage = staged_log_terms(ys)
    for r in range(n_sub):
        log_beta, log_keep, local = stage[r]
        newer = jnp.zeros((sub, 1), F32)
        after = [None] * n_blk
        for b in reversed(range(n_blk)):
            loc = local[b * sub:(b + 1) * sub, :]
            after[b] = loc + newer if b < n_blk - 1 else loc
            newer = newer + (loc[:, 0:1] + log_keep[:, b * blk:b * blk + 1])
        a = jnp.exp2(log_beta + jnp.concatenate(after, axis=1))
        v = v_ref[0, pl.ds(kstarts[r], near), :]
        acc_ref[r * sub:(r + 1) * sub, :] = jnp.dot(a.astype(BF16), v, preferred_element_type=F32)
        c_ref[r * sub:(r + 1) * sub, :] = newer

    def cond(carry):
        m, c_max = carry
        return (kstarts[n_sub - 1] - m * blk > 0) & (c_max > -F32_EXP2_UNDERFLOW)

    def body(carry):
        m, _ = carry
        wins, ys = [], []
        for r in range(n_sub):
            kend = kstarts[r] - m * blk
            wstart = pl.multiple_of(jnp.maximum(kend - blk, 0), sub)
            wins.append(wstart)
            ys.append(masked_scores(r, wstart, blk, kend - wstart, causal=False))
        stage = staged_log_terms(ys)
        c_old = c_ref[...]
        outs, c_news = [], []
        for r in range(n_sub):
            log_beta, log_keep, local = stage[r]
            after = local + c_old[r * sub:(r + 1) * sub, :]
            a = jnp.exp2(log_beta + after)
            v = v_ref[0, pl.ds(wins[r], blk), :]
            outs.append(jnp.dot(a.astype(BF16), v, preferred_element_type=F32))
            c_news.append(after[:, 0:1] + log_keep[:, 0:1])
        c_new = jnp.concatenate(c_news, axis=0)
        acc_ref[...] += jnp.concatenate(outs, axis=0)
        c_ref[...] = c_new
        return m + 1, jnp.max(c_new)

    lax.while_loop(cond, body, (0, jnp.max(c_ref[...])))
    o_ref[...] = acc_ref[...].astype(o_ref.dtype)


def _stick_breaking_attention(qkv, *, n_heads):
    _, s, _ = qkv.shape
    tq, sub, blk, near = SB_TQ, SB_SUB, SB_BLK, SB_NEAR
    assert s % tq == 0 and tq % sub == 0 and near % blk == 0 and blk % sub == 0 and s >= near
    tri = np.tril(np.ones((blk, blk), np.float32), k=-1)
    tri2 = jnp.asarray(np.concatenate([tri, tri], axis=0), BF16)
    vmem = (2 * 2 * s * HEAD_DIM * 2 + 2 * 2 * blk * blk * 2 + 4 * tq * HEAD_DIM * 2
            + tq * LANES * 4 * 2 + 10 * tq * near * 4)
    return pl.pallas_call(
        _sb_kernel,
        name="stick_breaking_attention",
        grid=(n_heads, s // tq),
        in_specs=[pl.BlockSpec((1, tq, HEAD_DIM), lambda h, i: (h, i, 0)),
                  pl.BlockSpec((1, s, HEAD_DIM), lambda h, i: (n_heads + h, 0, 0)),
                  pl.BlockSpec((1, s, HEAD_DIM), lambda h, i: (2 * n_heads + h, 0, 0)),
                  pl.BlockSpec((2 * blk, blk), lambda h, i: (0, 0))],
        out_specs=pl.BlockSpec((tq, HEAD_DIM), lambda h, i: (i, h)),
        out_shape=jax.ShapeDtypeStruct((s, n_heads * HEAD_DIM), BF16),
        scratch_shapes=[pltpu.VMEM((tq, HEAD_DIM), F32), pltpu.VMEM((tq, 1), F32)],
        compiler_params=_params(("arbitrary", "arbitrary"), vmem),
    )(qkv, qkv, qkv, tri2)


OUT_TM = 512
OUT_HEAD_GROUPS = 4


def _out_kernel(*refs, n_groups, n_heads):
    o_refs = refs[:n_groups]
    lse_refs = refs[n_groups:2 * n_groups] if n_groups > 1 else ()
    rest = refs[len(o_refs) + len(lse_refs):]
    w_ref, x_ref, gate_ref, g_ref, sc_ref, sh_ref = rest[:6]
    if n_groups == 1:
        xo_ref, h_ref = rest[6:]
    else:
        unperm_ref, xo_ref, h_ref, merged_ref = rest[6:]
    tm = x_ref.shape[0]

    if n_groups == 1:
        y = jnp.dot(o_refs[0][...], w_ref[...], preferred_element_type=F32)
    else:
        rows_per = tm // N_RESIDUES
        wts = []
        for k in range(N_RESIDUES):
            lses = [lr[k] for lr in lse_refs]
            m = functools.reduce(jnp.maximum, lses)
            es = [jnp.exp(l - m) for l in lses]
            inv = 1.0 / functools.reduce(jnp.add, es)
            wts.append([e * inv for e in es[1:]])
        heads_per = n_heads // OUT_HEAD_GROUPS
        y = None
        for c in range(OUT_HEAD_GROUPS):
            for k in range(N_RESIDUES):
                rows = slice(k * rows_per, (k + 1) * rows_per)
                for h in range(c * heads_per, (c + 1) * heads_per):
                    cols = slice(h * HEAD_DIM, (h + 1) * HEAD_DIM)
                    base = o_refs[0][k, :, cols].astype(F32)
                    acc = base
                    for g in range(1, n_groups):
                        acc = acc + wts[k][g - 1][:, h:h + 1] * (o_refs[g][k, :, cols].astype(F32) - base)
                    merged_ref[rows, cols] = acc.astype(BF16)
            gcols = slice(c * heads_per * HEAD_DIM, (c + 1) * heads_per * HEAD_DIM)
            mixed = jnp.dot(unperm_ref[...], merged_ref[:, gcols], preferred_element_type=F32).astype(BF16)
            part = jnp.dot(mixed, w_ref[gcols, :], preferred_element_type=F32)
            y = part if y is None else y + part

    xo_ref[...] = x_ref[...] + gate_ref[...] * y

    def norm(k, carry):
        r = pl.multiple_of(k * NORM_ROWS, NORM_ROWS)
        h = _modulated_rms_norm(xo_ref[pl.ds(r, NORM_ROWS), :], g_ref[...], sc_ref[...], sh_ref[...])
        h_ref[pl.ds(r, NORM_ROWS), :] = h.astype(BF16)
        return carry
    lax.fori_loop(0, tm // NORM_ROWS, norm, 0, unroll=2)


def _out_projection(outs, lses, w, x, gate, g, sc, sh, *, n_heads):
    s, d = x.shape
    dm = w.shape[0]
    n_groups = len(outs)
    tm = OUT_TM
    row_d = pl.BlockSpec((tm, d), lambda i: (i, 0))
    vec = pl.BlockSpec((1, d), lambda i: (0, 0))
    scratch = []
    if n_groups == 1:
        in_specs = [pl.BlockSpec((tm, dm), lambda i: (i, 0))]
        args = list(outs)
    else:
        rows_per = tm // N_RESIDUES
        in_specs = ([pl.BlockSpec((N_RESIDUES, rows_per, dm), lambda i: (0, i, 0))] * n_groups
                    + [pl.BlockSpec((N_RESIDUES, rows_per, LANES), lambda i: (0, i, 0))] * n_groups)
        args = list(outs) + list(lses)
        scratch = [pltpu.VMEM((tm, dm), BF16)]
    in_specs += [pl.BlockSpec((dm, d), lambda i: (0, 0)), row_d, vec, vec, vec, vec]
    args += [w, x, gate, g, sc, sh]
    if n_groups > 1:
        in_specs.append(pl.BlockSpec((tm, tm), lambda i: (0, 0)))
        args.append(jnp.asarray(_residue_major_permutation(tm).T, BF16))
    vmem = (2 * n_groups * tm * dm * 2 + 2 * dm * d * 2 + 2 * tm * d * 4 + 2 * tm * d * 4
            + 2 * tm * d * 2 + tm * dm * 2 + tm * d * 4)
    return pl.pallas_call(
        functools.partial(_out_kernel, n_groups=n_groups, n_heads=n_heads),
        name="mixer_out_projection",
        grid=(s // tm,),
        in_specs=in_specs,
        out_specs=[row_d, row_d],
        out_shape=[jax.ShapeDtypeStruct((s, d), F32), jax.ShapeDtypeStruct((s, d), BF16)],
        scratch_shapes=scratch,
        compiler_params=_params(("arbitrary",), vmem),
    )(*args)


FFN_TM = 1024
FFN_TF = 512
FFN_ROW_SPLIT = 2


def _causal_conv(u_ref, row0, rows, cw_ref, cb_ref):
    def at(shift):
        return u_ref[SUBLANES + row0 - shift:SUBLANES + row0 - shift + rows, :]

    y = cb_ref[...] + cw_ref[0:1, :] * at(2)
    y = y + cw_ref[1:2, :] * at(1)
    return y + cw_ref[2:3, :] * at(0)


def _ffn_kernel(h_ref, wa_ref, wg_ref, cwa_ref, cwg_ref, cba_ref, cbg_ref, wd_ref, x_ref, gate_ref,
                fg_ref, o_ref, tail_a_ref, tail_g_ref, ua_ref, ug_ref, *, final_norm):
    i = pl.program_id(0)
    j = pl.program_id(1)
    tm = h_ref.shape[0]

    @pl.when(i == 0)
    def _():
        tail_a_ref[j] = jnp.zeros(tail_a_ref.shape[1:], F32)
        tail_g_ref[j] = jnp.zeros(tail_g_ref.shape[1:], F32)

    @pl.when(j == 0)
    def _():
        o_ref[...] = jnp.zeros_like(o_ref)

    ua_ref[0:SUBLANES, :] = tail_a_ref[j]
    ug_ref[0:SUBLANES, :] = tail_g_ref[j]

    sub = tm // FFN_ROW_SPLIT
    for r in range(FFN_ROW_SPLIT):
        h = h_ref[r * sub:(r + 1) * sub, :]
        rows = slice(SUBLANES + r * sub, SUBLANES + (r + 1) * sub)
        ua_ref[rows, :] = jnp.dot(h, wa_ref[...], preferred_element_type=F32)
        ug_ref[rows, :] = jnp.dot(h, wg_ref[...], preferred_element_type=F32)
    for r in range(FFN_ROW_SPLIT):
        ya = _causal_conv(ua_ref, r * sub, sub, cwa_ref, cba_ref)
        yg = _causal_conv(ug_ref, r * sub, sub, cwg_ref, cbg_ref)
        act = (yg * (1.0 / (1.0 + jnp.exp(-yg))) * ya).astype(BF16)
        o_ref[r * sub:(r + 1) * sub, :] += jnp.dot(act, wd_ref[...], preferred_element_type=F32)
    tail_a_ref[j] = ua_ref[tm:tm + SUBLANES, :]
    tail_g_ref[j] = ug_ref[tm:tm + SUBLANES, :]

    @pl.when(j == pl.num_programs(1) - 1)
    def _():
        def fin(k, carry):
            r = pl.multiple_of(k * NORM_ROWS, NORM_ROWS)
            rows = pl.ds(r, NORM_ROWS)
            xn = x_ref[rows, :] + gate_ref[...] * o_ref[rows, :]
            if final_norm:
                xn = (xn * lax.rsqrt(jnp.mean(xn * xn, axis=-1, keepdims=True) + EPS)) * fg_ref[...]
            o_ref[rows, :] = xn
            return carry
        lax.fori_loop(0, tm // NORM_ROWS, fin, 0)


def _pad_halves(a, d_ff, d_ff_pad):
    pad = [(0, 0)] * (a.ndim - 1) + [(0, d_ff_pad - d_ff)]
    return jnp.concatenate([jnp.pad(a[..., :d_ff], pad), jnp.pad(a[..., d_ff:], pad)], axis=-1)


PREP_ROWS = 256
PREP_COLS = 256
PREP_CHUNK = 32


def _prep_up_kernel(w_ref, o_ref, *, d_ff, d_ff_pad):
    def body(k, carry):
        rows = pl.ds(pl.multiple_of(k * PREP_CHUNK, PREP_CHUNK), PREP_CHUNK)
        for half in range(2):
            o_ref[0, rows, half * d_ff_pad:half * d_ff_pad + d_ff] = (
                w_ref[0, rows, half * d_ff:(half + 1) * d_ff].astype(BF16))
            if d_ff_pad > d_ff:
                o_ref[0, rows, half * d_ff_pad + d_ff:(half + 1) * d_ff_pad] = (
                    jnp.zeros((PREP_CHUNK, d_ff_pad - d_ff), BF16))
        return carry
    lax.fori_loop(0, w_ref.shape[1] // PREP_CHUNK, body, 0)


def _prep_down_kernel(w_ref, o_ref, *, d_ff, d_ff_pad):
    def body(k, carry):
        rows = pl.ds(pl.multiple_of(k * PREP_CHUNK, PREP_CHUNK), PREP_CHUNK)
        o_ref[0, rows, :] = w_ref[0, rows, :].astype(BF16)
        return carry
    lax.fori_loop(0, d_ff // PREP_CHUNK, body, 0)
    if d_ff_pad > d_ff:
        o_ref[0, d_ff:, :] = jnp.zeros((d_ff_pad - d_ff, o_ref.shape[2]), BF16)


def _prep_ffn_weights(w_up, w_down):
    depth, d, _ = w_up.shape
    d_ff = w_down.shape[1]
    d_ff_pad = pl.cdiv(d_ff, FFN_TF) * FFN_TF
    assert d_ff % LANES == 0 and d_ff % PREP_CHUNK == 0 and d % PREP_ROWS == 0 and d % PREP_COLS == 0
    w_up_p = pl.pallas_call(
        functools.partial(_prep_up_kernel, d_ff=d_ff, d_ff_pad=d_ff_pad),
        name="prep_w_up",
        grid=(depth, d // PREP_ROWS),
        in_specs=[pl.BlockSpec((1, PREP_ROWS, 2 * d_ff), lambda l, i: (l, i, 0))],
        out_specs=pl.BlockSpec((1, PREP_ROWS, 2 * d_ff_pad), lambda l, i: (l, i, 0)),
        out_shape=jax.ShapeDtypeStruct((depth, d, 2 * d_ff_pad), BF16),
        compiler_params=_params(("arbitrary", "arbitrary"),
                                2 * PREP_ROWS * 2 * (d_ff * 4 + d_ff_pad * 2)),
    )(w_up)
    w_down_p = pl.pallas_call(
        functools.partial(_prep_down_kernel, d_ff=d_ff, d_ff_pad=d_ff_pad),
        name="prep_w_down",
        grid=(depth, d // PREP_COLS),
        in_specs=[pl.BlockSpec((1, d_ff, PREP_COLS), lambda l, i: (l, 0, i))],
        out_specs=pl.BlockSpec((1, d_ff_pad, PREP_COLS), lambda l, i: (l, 0, i)),
        out_shape=jax.ShapeDtypeStruct((depth, d_ff_pad, d), BF16),
        compiler_params=_params(("arbitrary", "arbitrary"),
                                2 * PREP_COLS * (d_ff * 4 + d_ff_pad * 2)),
    )(w_down)
    return w_up_p, w_down_p


def _conv_ffn(h, layer, w_up_p, conv_w, conv_b, w_down_p, x, gate, final_g, *, final_norm):
    s, d = x.shape
    d_ff = conv_w.shape[1] // 2
    tm, tf = FFN_TM, FFN_TF
    d_ff_pad = w_down_p.shape[1]
    nj = d_ff_pad // tf
    conv_w_p = _pad_halves(conv_w, d_ff, d_ff_pad)
    conv_b_p = _pad_halves(conv_b.reshape(1, -1), d_ff, d_ff_pad)
    vec = pl.BlockSpec((1, d), lambda i, j: (0, 0))
    vmem = (2 * tm * d * 2 + 2 * 2 * d * tf * 2 + 2 * tf * d * 2 + tm * d * 4 + 2 * tm * d * 4
            + 2 * nj * SUBLANES * tf * 4 + 8 * tm * tf * 4)
    return pl.pallas_call(
        functools.partial(_ffn_kernel, final_norm=final_norm),
        name="conv_ffn",
        grid=(s // tm, nj),
        in_specs=[pl.BlockSpec((tm, d), lambda i, j: (i, 0)),
                  pl.BlockSpec((None, d, tf), lambda i, j: (layer, 0, j)),
                  pl.BlockSpec((None, d, tf), lambda i, j: (layer, 0, nj + j)),
                  pl.BlockSpec((CONV_WIDTH, tf), lambda i, j: (0, j)),
                  pl.BlockSpec((CONV_WIDTH, tf), lambda i, j: (0, nj + j)),
                  pl.BlockSpec((1, tf), lambda i, j: (0, j)),
                  pl.BlockSpec((1, tf), lambda i, j: (0, nj + j)),
                  pl.BlockSpec((None, tf, d), lambda i, j: (layer, j, 0)),
                  pl.BlockSpec((tm, d), lambda i, j: (i, 0), pipeline_mode=pl.Buffered(1)),
                  vec, vec],
        out_specs=pl.BlockSpec((tm, d), lambda i, j: (i, 0)),
        out_shape=jax.ShapeDtypeStruct((s, d), F32),
        scratch_shapes=[pltpu.VMEM((nj, SUBLANES, tf), F32), pltpu.VMEM((nj, SUBLANES, tf), F32),
                        pltpu.VMEM((SUBLANES + tm, tf), F32), pltpu.VMEM((SUBLANES + tm, tf), F32)],
        compiler_params=_params(("arbitrary", "arbitrary"), vmem),
    )(h, w_up_p, w_up_p, conv_w_p, conv_w_p, conv_b_p, conv_b_p, w_down_p, x, gate, final_g.reshape(1, d))


def kernel(x, c, norm_mix_g, norm_ffn_g, ada_w, ada_b, w_in_a, w_out_a, w_in_b, w_out_b,
           w_up, conv_w, conv_b, w_down, final_g):
    batch, s, d = x.shape
    assert batch == 1, "the sequence is processed as one (S, D) slab"
    depth = ada_w.shape[0]
    n_heads = w_out_a.shape[1] // HEAD_DIM
    n_groups = len(DILATED_CONFIGS)

    xs = x.reshape(s, d)
    mod = _ada_modulation(c, ada_w, ada_b)
    w_up_p, w_down_p = _prep_ffn_weights(w_up, w_down)
    for i in range(depth):
        sh1, sc1, g1, sh2, sc2, g2 = [mod[i, :, k * d:(k + 1) * d] for k in range(N_MOD)]
        norm_g = norm_mix_g[i].reshape(1, d)
        ffn_g = norm_ffn_g[i].reshape(1, d)
        if i % 2 == 0:
            qkv = _qkv_projection(xs, norm_g, sc1, sh1, w_in_a[i // 2].astype(BF16), layout="residue_major")
            outs, lses = zip(*[_dilated_group_attention(qkv, g, n_heads=n_heads) for g in range(n_groups)])
            w_out = w_out_a[i // 2]
        else:
            qkv = _qkv_projection(xs, norm_g, sc1, sh1, w_in_b[i // 2].astype(BF16), layout="head_major")
            outs, lses = [_stick_breaking_attention(qkv, n_heads=n_heads)], None
            w_out = w_out_b[i // 2]
        xs, h2 = _out_projection(outs, lses, w_out.astype(BF16), xs, g1, ffn_g, sc2, sh2, n_heads=n_heads)
        xs = _conv_ffn(h2, i, w_up_p, conv_w[i], conv_b[i], w_down_p, xs, g2, final_g,
                       final_norm=(i == depth - 1))
    return xs.reshape(batch, s, d)
```

```python
import functools

import numpy as np
import jax
import jax.numpy as jnp
from jax import lax
from jax.experimental import pallas as pl
from jax.experimental.pallas import tpu as pltpu

HEAD_DIM = 128
DILATED_CONFIGS = ((128, 1), (512, 4), (2048, 16))
ATTN_SPAN_MAX = 128
CONV_WIDTH = 3
EPS = 1e-6
N_MOD = 6
NEG_BIG = -1e30
N_RESIDUES = 16
MID_DILATION = 4

LANES = 128
SUBLANES = 8
BF16_SUBLANES = 16
V7X_VMEM_BYTES = 64 * 1024 * 1024
VMEM_HEADROOM_BYTES = 6 * 1024 * 1024

F32_EXP2_UNDERFLOW = 150.0
LOG2_E = 1.4426950408889634
LN_2 = 0.6931471805599453

F32 = jnp.float32
BF16 = jnp.bfloat16


def _params(semantics, vmem_estimate_bytes):
    limit = min(int(vmem_estimate_bytes) + VMEM_HEADROOM_BYTES, V7X_VMEM_BYTES - VMEM_HEADROOM_BYTES)
    return pltpu.CompilerParams(dimension_semantics=semantics, vmem_limit_bytes=limit)


def _modulated_rms_norm(x, g, sc, sh):
    y = x * lax.rsqrt(jnp.mean(x * x, axis=-1, keepdims=True) + EPS)
    return (y * g) * (1.0 + sc) + sh


ADA_TN = 1024
ADA_ROWS = 256


def _ada_kernel(c_ref, w_ref, b_ref, o_ref):
    d = w_ref.shape[1]
    tn = w_ref.shape[2]

    def body(k, acc):
        r = pl.multiple_of(k * ADA_ROWS, ADA_ROWS)
        prod = c_ref[pl.ds(r, ADA_ROWS), :] * w_ref[0, pl.ds(r, ADA_ROWS), :]
        return acc + prod.reshape(ADA_ROWS // SUBLANES, SUBLANES, tn).sum(axis=0)

    acc = lax.fori_loop(0, d // ADA_ROWS, body, jnp.zeros((SUBLANES, tn), F32))
    o_ref[0] = acc.sum(axis=0, keepdims=True) + b_ref[0]


def _ada_modulation(c, ada_w, ada_b):
    depth, d, n = ada_w.shape
    c_col = c.reshape(d, 1)
    return pl.pallas_call(
        _ada_kernel,
        name="ada_modulation",
        grid=(depth, n // ADA_TN),
        in_specs=[
            pl.BlockSpec((d, 1), lambda l, j: (0, 0)),
            pl.BlockSpec((1, d, ADA_TN), lambda l, j: (l, 0, j)),
            pl.BlockSpec((1, 1, ADA_TN), lambda l, j: (l, 0, j)),
        ],
        out_specs=pl.BlockSpec((1, 1, ADA_TN), lambda l, j: (l, 0, j)),
        out_shape=jax.ShapeDtypeStruct((depth, 1, n), F32),
        compiler_params=_params(("arbitrary", "arbitrary"),
                                2 * d * ADA_TN * 4 + d * LANES * 4),
    )(c_col, ada_w, ada_b.reshape(depth, 1, n))


QKV_TM = 1024
QKV_TN = 1024
NORM_ROWS = 64


def _residue_of_slot(slot):
    per = N_RESIDUES // MID_DILATION
    return MID_DILATION * (slot % per) + slot // per


def _residue_major_permutation(rows):
    rho = np.arange(rows)
    rows_per = rows // N_RESIDUES
    src = N_RESIDUES * (rho % rows_per) + _residue_of_slot(rho // rows_per)
    perm = np.zeros((rows, rows), np.float32)
    perm[rho, src] = 1.0
    return perm


def _qkv_kernel(x_ref, g_ref, sc_ref, sh_ref, w_ref, *rest, layout):
    if layout == "residue_major":
        perm_ref, o_ref, h_ref = rest
    else:
        o_ref, h_ref = rest
    tm = x_ref.shape[0]

    @pl.when(pl.program_id(1) == 0)
    def _():
        def body(k, carry):
            r = pl.multiple_of(k * NORM_ROWS, NORM_ROWS)
            h = _modulated_rms_norm(x_ref[pl.ds(r, NORM_ROWS), :], g_ref[...], sc_ref[...], sh_ref[...])
            h_ref[pl.ds(r, NORM_ROWS), :] = h.astype(BF16)
            return carry
        lax.fori_loop(0, tm // NORM_ROWS, body, 0)
        if layout == "residue_major":
            h_ref[...] = jnp.dot(perm_ref[...], h_ref[...], preferred_element_type=F32).astype(BF16)

    res = jnp.dot(h_ref[...], w_ref[...], preferred_element_type=F32)
    if layout == "head_major":
        for cb in range(o_ref.shape[0]):
            o_ref[cb] = res[:, cb * HEAD_DIM:(cb + 1) * HEAD_DIM].astype(BF16)
    else:
        o_ref[...] = res.reshape(o_ref.shape).astype(BF16)


def _qkv_projection(x, g, sc, sh, w, *, layout):
    s, d = x.shape
    n = w.shape[1]
    tm, tn = QKV_TM, QKV_TN
    vec = pl.BlockSpec((1, d), lambda i, j: (0, 0))
    in_specs = [pl.BlockSpec((tm, d), lambda i, j: (i, 0)), vec, vec, vec,
                pl.BlockSpec((d, tn), lambda i, j: (0, j))]
    args = [x, g, sc, sh, w]
    vmem = 2 * tm * d * 4 + tm * d * 2 + 2 * d * tn * 2 + 2 * tm * tn * 2 + tm * tn * 4
    if layout == "head_major":
        out_shape = jax.ShapeDtypeStruct((n // HEAD_DIM, s, HEAD_DIM), BF16)
        out_spec = pl.BlockSpec((tn // HEAD_DIM, tm, HEAD_DIM), lambda i, j: (j, i, 0))
    else:
        assert layout == "residue_major"
        out_shape = jax.ShapeDtypeStruct((N_RESIDUES, s // N_RESIDUES, n), BF16)
        out_spec = pl.BlockSpec((N_RESIDUES, tm // N_RESIDUES, tn), lambda i, j: (0, i, j))
        in_specs.append(pl.BlockSpec((tm, tm), lambda i, j: (0, 0)))
        args.append(jnp.asarray(_residue_major_permutation(tm), BF16))
        vmem += 2 * tm * tm * 2 + tm * d * (4 + 2)
    return pl.pallas_call(
        functools.partial(_qkv_kernel, layout=layout),
        name="qkv_projection",
        grid=(s // tm, n // tn),
        in_specs=in_specs,
        out_specs=out_spec,
        out_shape=out_shape,
        scratch_shapes=[pltpu.VMEM((tm, d), BF16)],
        compiler_params=_params(("arbitrary", "arbitrary"), vmem),
    )(*args)


def _alibi_slopes(n):
    return [float(np.float32(2.0 ** (-8.0 * (i + 1) / n))) for i in range(n)]


def _dilated_kernel(q_ref, kc_ref, vc_ref, bias_ref, o_ref, lse_ref, kp_ref, vp_ref, *, n_heads):
    slots, rows_per, _ = q_ref.shape
    rows = slots * rows_per
    first_block = (pl.program_id(1) == 0).astype(jnp.int32)

    @pl.when(pl.program_id(1) == 0)
    def _():
        kp_ref[...] = jnp.zeros_like(kp_ref)
        vp_ref[...] = jnp.zeros_like(vp_ref)
    lane = lax.broadcasted_iota(jnp.int32, (rows, LANES), 1)
    to_log2 = HEAD_DIM ** -0.5 * LOG2_E
    contract_last = (((1,), (1,)), ((), ()))

    def head(ref, h):
        return ref[:, :, h * HEAD_DIM:(h + 1) * HEAD_DIM].reshape(rows, HEAD_DIM)

    scores = []
    for h in range(n_heads):
        keys = jnp.concatenate([head(kp_ref, h), head(kc_ref, h)], axis=0)
        scores.append(lax.dot_general(head(q_ref, h), keys, contract_last, preferred_element_type=F32))

    max_tile = jnp.zeros((rows, LANES), F32)
    den_tile = jnp.ones((rows, LANES), F32)
    for h in range(n_heads):
        s = scores[h] * to_log2 + bias_ref[first_block, h]
        m = jnp.max(s, axis=-1, keepdims=True)
        p = jnp.exp2(s - m)
        denom = jnp.sum(p, axis=-1, keepdims=True)
        values = jnp.concatenate([head(vp_ref, h), head(vc_ref, h)], axis=0)
        acc = jnp.dot(p.astype(BF16), values, preferred_element_type=F32)
        o_ref[:, :, h * HEAD_DIM:(h + 1) * HEAD_DIM] = (
            (acc / denom).reshape(slots, rows_per, HEAD_DIM).astype(o_ref.dtype))
        max_tile = jnp.where(lane == h, m, max_tile)
        den_tile = jnp.where(lane == h, denom, den_tile)
    lse_ref[...] = ((max_tile + jnp.log2(den_tile)) * LN_2).reshape(lse_ref.shape)
    kp_ref[...] = kc_ref[...]
    vp_ref[...] = vc_ref[...]


def _alibi_bias_tables(dilation, span, slots, rows_per, n_heads):
    rows = slots * rows_per
    rho = np.arange(rows)
    pos = N_RESIDUES * (rho % rows_per) + _residue_of_slot(rho // rows_per)
    m = pos // dilation
    j_cur = m[:, None] - m[None, :]
    j = np.concatenate([j_cur + rows, j_cur], axis=1)
    valid = (j >= 0) & (j <= span)
    valid = jnp.asarray(np.stack([valid, valid & (np.arange(2 * rows) >= rows)[None, :]]))
    dist = jnp.asarray((j * dilation).astype(np.float32))
    slopes = jnp.asarray(_alibi_slopes(n_heads), F32)
    bias = -(slopes[:, None, None] * dist[None]) * LOG2_E
    return jnp.where(valid[:, None], bias[None], NEG_BIG)


def _dilated_group_attention(qkv, group, *, n_heads):
    window, dilation = DILATED_CONFIGS[group]
    n_res, s_per, _ = qkv.shape
    dm = n_heads * HEAD_DIM
    span = window // dilation
    assert n_res == N_RESIDUES and N_RESIDUES % dilation == 0 and n_heads <= LANES
    assert window % dilation == 0 and span <= ATTN_SPAN_MAX
    slots = N_RESIDUES // dilation
    rows_per = max(ATTN_SPAN_MAX // slots, BF16_SUBLANES)
    rows = slots * rows_per
    assert s_per % rows_per == 0 and rows >= span
    bias = _alibi_bias_tables(dilation, span, slots, rows_per, n_heads)
    base = group * 3

    def spec(which):
        return pl.BlockSpec((slots, rows_per, dm), lambda p, b: (p, b, base + which))

    return pl.pallas_call(
        functools.partial(_dilated_kernel, n_heads=n_heads),
        name=f"dilated_attention_g{group}",
        grid=(N_RESIDUES // slots, s_per // rows_per),
        in_specs=[spec(0), spec(1), spec(2),
                  pl.BlockSpec((2, n_heads, rows, 2 * rows), lambda p, b: (0, 0, 0, 0),
                               pipeline_mode=pl.Buffered(1))],
        out_specs=[pl.BlockSpec((slots, rows_per, dm), lambda p, b: (p, b, 0)),
                   pl.BlockSpec((slots, rows_per, LANES), lambda p, b: (p, b, 0))],
        out_shape=[jax.ShapeDtypeStruct((N_RESIDUES, s_per, dm), BF16),
                   jax.ShapeDtypeStruct((N_RESIDUES, s_per, LANES), F32)],
        scratch_shapes=[pltpu.VMEM((slots, rows_per, dm), BF16), pltpu.VMEM((slots, rows_per, dm), BF16)],
        compiler_params=_params(("arbitrary", "arbitrary"),
                                (2 * 4 + 2) * rows * dm * 2 + 2 * rows * LANES * 4
                                + (2 + 4 * n_heads) * rows * 2 * rows * 4),
    )(qkv, qkv, qkv, bias)


SB_TQ = 1024
SB_SUB = 128
SB_BLK = 256
SB_NEAR = 512
SB_GROUP = 4


def _sb_kernel(q_ref, k_ref, v_ref, tri2_ref, o_ref, acc_ref, c_ref):
    tq = q_ref.shape[1]
    sub, blk, near = SB_SUB, SB_BLK, SB_NEAR
    n_sub = tq // sub
    i = pl.program_id(1)
    to_log2 = HEAD_DIM ** -0.5 * LOG2_E
    contract_last = (((1,), (1,)), ((), ()))

    def masked_scores(r, kstart, col, limit):
        q = q_ref[0, r * sub:(r + 1) * sub, :]
        k = k_ref[0, pl.ds(kstart, col.shape[1]), :]
        y = lax.dot_general(q, k, contract_last, preferred_element_type=F32) * to_log2
        return jnp.where(col < limit, y, NEG_BIG)

    def log_terms(y):
        neg_abs = pltpu.bitcast(pltpu.bitcast(y, jnp.uint32) | jnp.uint32(0x80000000), F32)
        log_beta = jnp.minimum(y, 0.0) - jnp.log2(1.0 + jnp.exp2(neg_abs))
        log_keep = log_beta - y
        return log_beta, log_keep

    def staged_log_terms(ys):
        n_blk = ys[0].shape[1] // blk
        stage = []
        for g in range(0, len(ys), SB_GROUP):
            terms, lhs = [], []
            for y in ys[g:g + SB_GROUP]:
                log_beta, log_keep = log_terms(y)
                hi = log_keep.astype(BF16)
                lo = (log_keep - hi.astype(F32)).astype(BF16)
                lhs += [jnp.concatenate([hi[:, b * blk:(b + 1) * blk], lo[:, b * blk:(b + 1) * blk]], axis=1)
                        for b in range(n_blk)]
                terms.append((log_beta, log_keep))
            local = jnp.dot(jnp.concatenate(lhs, axis=0), tri2_ref[...], preferred_element_type=F32)
            rows = n_blk * sub
            stage += [(lb, lk, local[k * rows:(k + 1) * rows, :]) for k, (lb, lk) in enumerate(terms)]
        return stage

    n_blk = near // blk
    col_minus_row = (lax.broadcasted_iota(jnp.int32, (sub, near), 1)
                     - lax.broadcasted_iota(jnp.int32, (sub, near), 0))
    kstarts, ys = [], []
    for r in range(n_sub):
        q_start = i * tq + r * sub
        kstart = pl.multiple_of(jnp.maximum(q_start + sub - near, 0), sub)
        kstarts.append(kstart)
        ys.append(masked_scores(r, kstart, col_minus_row, q_start - kstart))
    stage = staged_log_terms(ys)
    for r in range(n_sub):
        log_beta, log_keep, local = stage[r]
        newer = jnp.zeros((sub, 1), F32)
        after = [None] * n_blk
        for b in reversed(range(n_blk)):
            loc = local[b * sub:(b + 1) * sub, :]
            after[b] = loc + newer if b < n_blk - 1 else loc
            newer = newer + (loc[:, 0:1] + log_keep[:, b * blk:b * blk + 1])
        a = jnp.exp2(log_beta + jnp.concatenate(after, axis=1))
        v = v_ref[0, pl.ds(kstarts[r], near), :]
        acc_ref[r * sub:(r + 1) * sub, :] = jnp.dot(a.astype(BF16), v, preferred_element_type=F32)
        c_ref[r * sub:(r + 1) * sub, :] = newer

    def cond(carry):
        m, c_max = carry
        return (kstarts[n_sub - 1] - m * blk > 0) & (c_max > -F32_EXP2_UNDERFLOW)

    def body(carry):
        m, _ = carry
        col = lax.broadcasted_iota(jnp.int32, (sub, blk), 1)
        wins, ys = [], []
        for r in range(n_sub):
            kend = kstarts[r] - m * blk
            wstart = pl.multiple_of(jnp.maximum(kend - blk, 0), sub)
            wins.append(wstart)
            ys.append(masked_scores(r, wstart, col, kend - wstart))
        stage = staged_log_terms(ys)
        c_old = c_ref[...]
        outs, c_news = [], []
        for r in range(n_sub):
            log_beta, log_keep, local = stage[r]
            after = local + c_old[r * sub:(r + 1) * sub, :]
            a = jnp.exp2(log_beta + after)
            v = v_ref[0, pl.ds(wins[r], blk), :]
            outs.append(jnp.dot(a.astype(BF16), v, preferred_element_type=F32))
            c_news.append(after[:, 0:1] + log_keep[:, 0:1])
        c_new = jnp.concatenate(c_news, axis=0)
        acc_ref[...] += jnp.concatenate(outs, axis=0)
        c_ref[...] = c_new
        return m + 1, jnp.max(c_new)

    lax.while_loop(cond, body, (0, jnp.max(c_ref[...])))
    o_ref[...] = acc_ref[...].astype(o_ref.dtype)


def _stick_breaking_attention(qkv, *, n_heads):
    _, s, _ = qkv.shape
    tq, sub, blk, near = SB_TQ, SB_SUB, SB_BLK, SB_NEAR
    assert s % tq == 0 and tq % sub == 0 and near % blk == 0 and blk % sub == 0 and s >= near
    tri = np.tril(np.ones((blk, blk), np.float32), k=-1)
    tri2 = jnp.asarray(np.concatenate([tri, tri], axis=0), BF16)
    vmem = (2 * 2 * s * HEAD_DIM * 2 + 2 * 2 * blk * blk * 2 + 4 * tq * HEAD_DIM * 2
            + tq * LANES * 4 * 2 + 10 * tq * near * 4)
    return pl.pallas_call(
        _sb_kernel,
        name="stick_breaking_attention",
        grid=(n_heads, s // tq),
        in_specs=[pl.BlockSpec((1, tq, HEAD_DIM), lambda h, i: (h, i, 0)),
                  pl.BlockSpec((1, s, HEAD_DIM), lambda h, i: (n_heads + h, 0, 0)),
                  pl.BlockSpec((1, s, HEAD_DIM), lambda h, i: (2 * n_heads + h, 0, 0)),
                  pl.BlockSpec((2 * blk, blk), lambda h, i: (0, 0))],
        out_specs=pl.BlockSpec((tq, HEAD_DIM), lambda h, i: (i, h)),
        out_shape=jax.ShapeDtypeStruct((s, n_heads * HEAD_DIM), BF16),
        scratch_shapes=[pltpu.VMEM((tq, HEAD_DIM), F32), pltpu.VMEM((tq, 1), F32)],
        compiler_params=_params(("arbitrary", "arbitrary"), vmem),
    )(qkv, qkv, qkv, tri2)


OUT_TM = 512
OUT_HEAD_GROUPS = 4


def _out_kernel(*refs, n_groups, n_heads):
    o_refs = refs[:n_groups]
    lse_refs = refs[n_groups:2 * n_groups] if n_groups > 1 else ()
    rest = refs[len(o_refs) + len(lse_refs):]
    w_ref, x_ref, gate_ref, g_ref, sc_ref, sh_ref = rest[:6]
    if n_groups == 1:
        xo_ref, h_ref = rest[6:]
    else:
        unperm_ref, xo_ref, h_ref, merged_ref = rest[6:]
    tm = x_ref.shape[0]

    if n_groups == 1:
        y = jnp.dot(o_refs[0][...], w_ref[...], preferred_element_type=F32)
    else:
        rows_per = tm // N_RESIDUES
        wts = []
        for k in range(N_RESIDUES):
            lses = [lr[k] for lr in lse_refs]
            m = functools.reduce(jnp.maximum, lses)
            es = [jnp.exp(l - m) for l in lses]
            inv = 1.0 / functools.reduce(jnp.add, es)
            wts.append([e * inv for e in es[1:]])
        heads_per = n_heads // OUT_HEAD_GROUPS
        y = None
        for c in range(OUT_HEAD_GROUPS):
            for k in range(N_RESIDUES):
                rows = slice(k * rows_per, (k + 1) * rows_per)
                for h in range(c * heads_per, (c + 1) * heads_per):
                    cols = slice(h * HEAD_DIM, (h + 1) * HEAD_DIM)
                    base = o_refs[0][k, :, cols].astype(F32)
                    acc = base
                    for g in range(1, n_groups):
                        acc = acc + wts[k][g - 1][:, h:h + 1] * (o_refs[g][k, :, cols].astype(F32) - base)
                    merged_ref[rows, cols] = acc.astype(BF16)
            gcols = slice(c * heads_per * HEAD_DIM, (c + 1) * heads_per * HEAD_DIM)
            mixed = jnp.dot(unperm_ref[...], merged_ref[:, gcols], preferred_element_type=F32).astype(BF16)
            part = jnp.dot(mixed, w_ref[gcols, :], preferred_element_type=F32)
            y = part if y is None else y + part

    xo_ref[...] = x_ref[...] + gate_ref[...] * y

    def norm(k, carry):
        r = pl.multiple_of(k * NORM_ROWS, NORM_ROWS)
        h = _modulated_rms_norm(xo_ref[pl.ds(r, NORM_ROWS), :], g_ref[...], sc_ref[...], sh_ref[...])
        h_ref[pl.ds(r, NORM_ROWS), :] = h.astype(BF16)
        return carry
    lax.fori_loop(0, tm // NORM_ROWS, norm, 0, unroll=2)


def _out_projection(outs, lses, w, x, gate, g, sc, sh, *, n_heads):
    s, d = x.shape
    dm = w.shape[0]
    n_groups = len(outs)
    tm = OUT_TM
    row_d = pl.BlockSpec((tm, d), lambda i: (i, 0))
    vec = pl.BlockSpec((1, d), lambda i: (0, 0))
    scratch = []
    if n_groups == 1:
        in_specs = [pl.BlockSpec((tm, dm), lambda i: (i, 0))]
        args = list(outs)
    else:
        rows_per = tm // N_RESIDUES
        in_specs = ([pl.BlockSpec((N_RESIDUES, rows_per, dm), lambda i: (0, i, 0))] * n_groups
                    + [pl.BlockSpec((N_RESIDUES, rows_per, LANES), lambda i: (0, i, 0))] * n_groups)
        args = list(outs) + list(lses)
        scratch = [pltpu.VMEM((tm, dm), BF16)]
    in_specs += [pl.BlockSpec((dm, d), lambda i: (0, 0)), row_d, vec, vec, vec, vec]
    args += [w, x, gate, g, sc, sh]
    if n_groups > 1:
        in_specs.append(pl.BlockSpec((tm, tm), lambda i: (0, 0)))
        args.append(jnp.asarray(_residue_major_permutation(tm).T, BF16))
    vmem = (2 * n_groups * tm * dm * 2 + 2 * dm * d * 2 + 2 * tm * d * 4 + 2 * tm * d * 4
            + 2 * tm * d * 2 + tm * dm * 2 + tm * d * 4)
    return pl.pallas_call(
        functools.partial(_out_kernel, n_groups=n_groups, n_heads=n_heads),
        name="mixer_out_projection",
        grid=(s // tm,),
        in_specs=in_specs,
        out_specs=[row_d, row_d],
        out_shape=[jax.ShapeDtypeStruct((s, d), F32), jax.ShapeDtypeStruct((s, d), BF16)],
        scratch_shapes=scratch,
        compiler_params=_params(("arbitrary",), vmem),
    )(*args)


FFN_TM = 1024
FFN_TF = 512
FFN_ROW_SPLIT = 2


def _causal_conv(u_ref, row0, rows, cw_ref, cb_ref):
    def at(shift):
        return u_ref[SUBLANES + row0 - shift:SUBLANES + row0 - shift + rows, :]

    y = cb_ref[...] + cw_ref[0:1, :] * at(2)
    y = y + cw_ref[1:2, :] * at(1)
    return y + cw_ref[2:3, :] * at(0)


def _ffn_kernel(h_ref, wa_ref, wg_ref, cwa_ref, cwg_ref, cba_ref, cbg_ref, wd_ref, x_ref, gate_ref,
                fg_ref, o_ref, tail_a_ref, tail_g_ref, ua_ref, ug_ref, *, final_norm):
    i = pl.program_id(0)
    j = pl.program_id(1)
    tm = h_ref.shape[0]

    @pl.when(i == 0)
    def _():
        tail_a_ref[j] = jnp.zeros(tail_a_ref.shape[1:], F32)
        tail_g_ref[j] = jnp.zeros(tail_g_ref.shape[1:], F32)

    @pl.when(j == 0)
    def _():
        o_ref[...] = jnp.zeros_like(o_ref)

    ua_ref[0:SUBLANES, :] = tail_a_ref[j]
    ug_ref[0:SUBLANES, :] = tail_g_ref[j]

    sub = tm // FFN_ROW_SPLIT
    for r in range(FFN_ROW_SPLIT):
        h = h_ref[r * sub:(r + 1) * sub, :]
        rows = slice(SUBLANES + r * sub, SUBLANES + (r + 1) * sub)
        ua_ref[rows, :] = jnp.dot(h, wa_ref[...], preferred_element_type=F32)
        ug_ref[rows, :] = jnp.dot(h, wg_ref[...], preferred_element_type=F32)
    for r in range(FFN_ROW_SPLIT):
        ya = _causal_conv(ua_ref, r * sub, sub, cwa_ref, cba_ref)
        yg = _causal_conv(ug_ref, r * sub, sub, cwg_ref, cbg_ref)
        act = (yg * (1.0 / (1.0 + jnp.exp(-yg))) * ya).astype(BF16)
        o_ref[r * sub:(r + 1) * sub, :] += jnp.dot(act, wd_ref[...], preferred_element_type=F32)
    tail_a_ref[j] = ua_ref[tm:tm + SUBLANES, :]
    tail_g_ref[j] = ug_ref[tm:tm + SUBLANES, :]

    @pl.when(j == pl.num_programs(1) - 1)
    def _():
        def fin(k, carry):
            r = pl.multiple_of(k * NORM_ROWS, NORM_ROWS)
            rows = pl.ds(r, NORM_ROWS)
            xn = x_ref[rows, :] + gate_ref[...] * o_ref[rows, :]
            if final_norm:
                xn = (xn * lax.rsqrt(jnp.mean(xn * xn, axis=-1, keepdims=True) + EPS)) * fg_ref[...]
            o_ref[rows, :] = xn
            return carry
        lax.fori_loop(0, tm // NORM_ROWS, fin, 0)


def _pad_halves(a, d_ff, d_ff_pad):
    pad = [(0, 0)] * (a.ndim - 1) + [(0, d_ff_pad - d_ff)]
    return jnp.concatenate([jnp.pad(a[..., :d_ff], pad), jnp.pad(a[..., d_ff:], pad)], axis=-1)


PREP_ROWS = 256
PREP_COLS = 256
PREP_CHUNK = 32


def _prep_up_kernel(w_ref, o_ref, *, d_ff, d_ff_pad):
    def body(k, carry):
        rows = pl.ds(pl.multiple_of(k * PREP_CHUNK, PREP_CHUNK), PREP_CHUNK)
        for half in range(2):
            o_ref[0, rows, half * d_ff_pad:half * d_ff_pad + d_ff] = (
                w_ref[0, rows, half * d_ff:(half + 1) * d_ff].astype(BF16))
            if d_ff_pad > d_ff:
                o_ref[0, rows, half * d_ff_pad + d_ff:(half + 1) * d_ff_pad] = (
                    jnp.zeros((PREP_CHUNK, d_ff_pad - d_ff), BF16))
        return carry
    lax.fori_loop(0, w_ref.shape[1] // PREP_CHUNK, body, 0)


def _prep_down_kernel(w_ref, o_ref, *, d_ff, d_ff_pad):
    def body(k, carry):
        rows = pl.ds(pl.multiple_of(k * PREP_CHUNK, PREP_CHUNK), PREP_CHUNK)
        o_ref[0, rows, :] = w_ref[0, rows, :].astype(BF16)
        return carry
    lax.fori_loop(0, d_ff // PREP_CHUNK, body, 0)
    if d_ff_pad > d_ff:
        o_ref[0, d_ff:, :] = jnp.zeros((d_ff_pad - d_ff, o_ref.shape[2]), BF16)


def _prep_ffn_weights(w_up, w_down):
    depth, d, _ = w_up.shape
    d_ff = w_down.shape[1]
    d_ff_pad = pl.cdiv(d_ff, FFN_TF) * FFN_TF
    assert d_ff % LANES == 0 and d_ff % PREP_CHUNK == 0 and d % PREP_ROWS == 0 and d % PREP_COLS == 0
    w_up_p = pl.pallas_call(
        functools.partial(_prep_up_kernel, d_ff=d_ff, d_ff_pad=d_ff_pad),
        name="prep_w_up",
        grid=(depth, d // PREP_ROWS),
        in_specs=[pl.BlockSpec((1, PREP_ROWS, 2 * d_ff), lambda l, i: (l, i, 0))],
        out_specs=pl.BlockSpec((1, PREP_ROWS, 2 * d_ff_pad), lambda l, i: (l, i, 0)),
        out_shape=jax.ShapeDtypeStruct((depth, d, 2 * d_ff_pad), BF16),
        compiler_params=_params(("arbitrary", "arbitrary"),
                                2 * PREP_ROWS * 2 * (d_ff * 4 + d_ff_pad * 2)),
    )(w_up)
    w_down_p = pl.pallas_call(
        functools.partial(_prep_down_kernel, d_ff=d_ff, d_ff_pad=d_ff_pad),
        name="prep_w_down",
        grid=(depth, d // PREP_COLS),
        in_specs=[pl.BlockSpec((1, d_ff, PREP_COLS), lambda l, i: (l, 0, i))],
        out_specs=pl.BlockSpec((1, d_ff_pad, PREP_COLS), lambda l, i: (l, 0, i)),
        out_shape=jax.ShapeDtypeStruct((depth, d_ff_pad, d), BF16),
        compiler_params=_params(("arbitrary", "arbitrary"),
                                2 * PREP_COLS * (d_ff * 4 + d_ff_pad * 2)),
    )(w_down)
    return w_up_p, w_down_p


def _conv_ffn(h, layer, w_up_p, conv_w, conv_b, w_down_p, x, gate, final_g, *, final_norm):
    s, d = x.shape
    d_ff = conv_w.shape[1] // 2
    tm, tf = FFN_TM, FFN_TF
    d_ff_pad = w_down_p.shape[1]
    nj = d_ff_pad // tf
    conv_w_p = _pad_halves(conv_w, d_ff, d_ff_pad)
    conv_b_p = _pad_halves(conv_b.reshape(1, -1), d_ff, d_ff_pad)
    vec = pl.BlockSpec((1, d), lambda i, j: (0, 0))
    vmem = (2 * tm * d * 2 + 2 * 2 * d * tf * 2 + 2 * tf * d * 2 + tm * d * 4 + 2 * tm * d * 4
            + 2 * nj * SUBLANES * tf * 4 + 8 * tm * tf * 4)
    return pl.pallas_call(
        functools.partial(_ffn_kernel, final_norm=final_norm),
        name="conv_ffn",
        grid=(s // tm, nj),
        in_specs=[pl.BlockSpec((tm, d), lambda i, j: (i, 0)),
                  pl.BlockSpec((None, d, tf), lambda i, j: (layer, 0, j)),
                  pl.BlockSpec((None, d, tf), lambda i, j: (layer, 0, nj + j)),
                  pl.BlockSpec((CONV_WIDTH, tf), lambda i, j: (0, j)),
                  pl.BlockSpec((CONV_WIDTH, tf), lambda i, j: (0, nj + j)),
                  pl.BlockSpec((1, tf), lambda i, j: (0, j)),
                  pl.BlockSpec((1, tf), lambda i, j: (0, nj + j)),
                  pl.BlockSpec((None, tf, d), lambda i, j: (layer, j, 0)),
                  pl.BlockSpec((tm, d), lambda i, j: (i, 0), pipeline_mode=pl.Buffered(1)),
                  vec, vec],
        out_specs=pl.BlockSpec((tm, d), lambda i, j: (i, 0)),
        out_shape=jax.ShapeDtypeStruct((s, d), F32),
        scratch_shapes=[pltpu.VMEM((nj, SUBLANES, tf), F32), pltpu.VMEM((nj, SUBLANES, tf), F32),
                        pltpu.VMEM((SUBLANES + tm, tf), F32), pltpu.VMEM((SUBLANES + tm, tf), F32)],
        compiler_params=_params(("arbitrary", "arbitrary"), vmem),
    )(h, w_up_p, w_up_p, conv_w_p, conv_w_p, conv_b_p, conv_b_p, w_down_p, x, gate, final_g.reshape(1, d))


def kernel(x, c, norm_mix_g, norm_ffn_g, ada_w, ada_b, w_in_a, w_out_a, w_in_b, w_out_b,
           w_up, conv_w, conv_b, w_down, final_g):
    batch, s, d = x.shape
    assert batch == 1, "the sequence is processed as one (S, D) slab"
    depth = ada_w.shape[0]
    n_heads = w_out_a.shape[1] // HEAD_DIM
    n_groups = len(DILATED_CONFIGS)

    xs = x.reshape(s, d)
    mod = _ada_modulation(c, ada_w, ada_b)
    w_up_p, w_down_p = _prep_ffn_weights(w_up, w_down)
    for i in range(depth):
        sh1, sc1, g1, sh2, sc2, g2 = [mod[i, :, k * d:(k + 1) * d] for k in range(N_MOD)]
        norm_g = norm_mix_g[i].reshape(1, d)
        ffn_g = norm_ffn_g[i].reshape(1, d)
        if i % 2 == 0:
            qkv = _qkv_projection(xs, norm_g, sc1, sh1, w_in_a[i // 2].astype(BF16), layout="residue_major")
            outs, lses = zip(*[_dilated_group_attention(qkv, g, n_heads=n_heads) for g in range(n_groups)])
            w_out = w_out_a[i // 2]
        else:
            qkv = _qkv_projection(xs, norm_g, sc1, sh1, w_in_b[i // 2].astype(BF16), layout="head_major")
            outs, lses = [_stick_breaking_attention(qkv, n_heads=n_heads)], None
            w_out = w_out_b[i // 2]
        xs, h2 = _out_projection(outs, lses, w_out.astype(BF16), xs, g1, ffn_g, sc2, sh2, n_heads=n_heads)
        xs = _conv_ffn(h2, i, w_up_p, conv_w[i], conv_b[i], w_down_p, xs, g2, final_g,
                       final_norm=(i == depth - 1))
    return xs.reshape(batch, s, d)
```

```python
import functools
import math
from typing import Callable, NamedTuple

import numpy as np
import jax
import jax.numpy as jnp
from jax import lax
from jax.experimental import pallas as pl
from jax.experimental.pallas import tpu as pltpu

HEAD_DIM = 128
DILATED_CONFIGS = ((128, 1), (512, 4), (2048, 16))
ATTN_SPAN_MAX = 128
CONV_WIDTH = 3
EPS = 1e-6
N_MOD = 6
NEG_BIG = -1e30
N_RESIDUES = 16
MID_DILATION = 4

LANES = 128
SUBLANES = 8
BF16_SUBLANES = 16
V7X_VMEM_BYTES = 64 * 1024 * 1024
VMEM_HEADROOM_BYTES = 6 * 1024 * 1024

F32_EXP2_UNDERFLOW = 150.0
LOG2_E = 1.4426950408889634
LN_2 = 0.6931471805599453

F32 = jnp.float32
BF16 = jnp.bfloat16


def _params(semantics, vmem_estimate_bytes):
    limit = min(int(vmem_estimate_bytes) + VMEM_HEADROOM_BYTES, V7X_VMEM_BYTES - VMEM_HEADROOM_BYTES)
    return pltpu.CompilerParams(dimension_semantics=semantics, vmem_limit_bytes=limit)


def _modulated_rms_norm(x, g, sc, sh):
    y = x * lax.rsqrt(jnp.mean(x * x, axis=-1, keepdims=True) + EPS)
    return (y * g) * (1.0 + sc) + sh


ADA_TN = 1024
ADA_ROWS = 256


def _ada_kernel(c_ref, w_ref, b_ref, o_ref):
    d = w_ref.shape[1]
    tn = w_ref.shape[2]

    def body(k, acc):
        r = pl.multiple_of(k * ADA_ROWS, ADA_ROWS)
        prod = c_ref[pl.ds(r, ADA_ROWS), :] * w_ref[0, pl.ds(r, ADA_ROWS), :]
        return acc + prod.reshape(ADA_ROWS // SUBLANES, SUBLANES, tn).sum(axis=0)

    acc = lax.fori_loop(0, d // ADA_ROWS, body, jnp.zeros((SUBLANES, tn), F32))
    o_ref[0] = acc.sum(axis=0, keepdims=True) + b_ref[0]


def _ada_modulation(c, ada_w, ada_b):
    depth, d, n = ada_w.shape
    c_col = c.reshape(d, 1)
    return pl.pallas_call(
        _ada_kernel,
        name="ada_modulation",
        grid=(depth, n // ADA_TN),
        in_specs=[
            pl.BlockSpec((d, 1), lambda l, j: (0, 0)),
            pl.BlockSpec((1, d, ADA_TN), lambda l, j: (l, 0, j)),
            pl.BlockSpec((1, 1, ADA_TN), lambda l, j: (l, 0, j)),
        ],
        out_specs=pl.BlockSpec((1, 1, ADA_TN), lambda l, j: (l, 0, j)),
        out_shape=jax.ShapeDtypeStruct((depth, 1, n), F32),
        compiler_params=_params(("arbitrary", "arbitrary"),
                                2 * d * ADA_TN * 4 + d * LANES * 4),
    )(c_col, ada_w, ada_b.reshape(depth, 1, n))


QKV_TM = 1024
QKV_TN = 1024
NORM_ROWS = 64


def _residue_of_slot(slot):
    per = N_RESIDUES // MID_DILATION
    return MID_DILATION * (slot % per) + slot // per


def _residue_major_permutation(rows):
    rho = np.arange(rows)
    rows_per = rows // N_RESIDUES
    src = N_RESIDUES * (rho % rows_per) + _residue_of_slot(rho // rows_per)
    perm = np.zeros((rows, rows), np.float32)
    perm[rho, src] = 1.0
    return perm


def _side_block_rows(rows, layers, n_steps, granule):
    for r in range(granule, rows + 1, granule):
        if rows % r == 0 and layers * (rows // r) <= n_steps:
            return r
    raise ValueError("side cast does not fit in the host call's grid")


class _SideCast(NamedTuple):
    src: jax.Array
    in_block: tuple
    in_index: Callable
    out_shape: tuple
    out_block: tuple
    out_index: Callable
    n_blocks: int
    body: Callable


def _side_cast_rows(w, n_steps):
    layers, rows, cols = w.shape
    block_rows = _side_block_rows(rows, layers, n_steps, BF16_SUBLANES)
    per = rows // block_rows

    def index(b):
        return (b // per, b % per, 0)

    def body(src_ref, dst_ref, b):
        dst_ref[...] = src_ref[...].astype(BF16)
    block = (1, block_rows, cols)
    return _SideCast(w, block, index, w.shape, block, index, layers * per, body)


def _side_cast_up(w_up, d_ff, d_ff_pad, n_steps):
    layers, d, _ = w_up.shape
    assert d_ff % LANES == 0 and d_ff_pad % LANES == 0
    block_rows = _side_block_rows(d, layers, n_steps, BF16_SUBLANES)
    per = d // block_rows

    def index(b):
        return (b // per, b % per, 0)

    def body(src_ref, dst_ref, b):
        for half in range(2):
            dst_ref[0, :, half * d_ff_pad:half * d_ff_pad + d_ff] = (
                src_ref[0, :, half * d_ff:(half + 1) * d_ff].astype(BF16))
            if d_ff_pad > d_ff:
                dst_ref[0, :, half * d_ff_pad + d_ff:(half + 1) * d_ff_pad] = (
                    jnp.zeros((block_rows, d_ff_pad - d_ff), BF16))
    return _SideCast(w_up, (1, block_rows, 2 * d_ff), index, (layers, d, 2 * d_ff_pad),
                     (1, block_rows, 2 * d_ff_pad), index, layers * per, body)


def _side_cast_down(w_down, d_ff_pad, n_steps):
    layers, d_ff, d = w_down.shape
    common = math.gcd(d_ff, d_ff_pad)
    rows = _side_block_rows(common, layers * (d_ff_pad // common), n_steps, BF16_SUBLANES)
    src_per, per = d_ff // rows, d_ff_pad // rows

    def in_index(b):
        return (b // per, jnp.minimum(b % per, src_per - 1), 0)

    def out_index(b):
        return (b // per, b % per, 0)

    def body(src_ref, dst_ref, b):
        dst_ref[...] = jnp.where(b % per < src_per, src_ref[...], 0.0).astype(BF16)
    return _SideCast(w_down, (1, rows, d), in_index, (layers, d_ff_pad, d), (1, rows, d), out_index,
                     layers * per, body)


def _qkv_kernel(*refs, layout, side_casts):
    n_side = len(side_casts)
    n_in = 6 if layout == "residue_major" else 5
    x_ref, g_ref, sc_ref, sh_ref, w_ref = refs[:5]
    perm_ref = refs[5] if layout == "residue_major" else None
    side_in = refs[n_in:n_in + n_side]
    o_ref = refs[n_in + n_side]
    side_out = refs[n_in + n_side + 1:n_in + 2 * n_side + 1]
    h_ref = refs[-1]
    tm = x_ref.shape[0]

    @pl.when(pl.program_id(1) == 0)
    def _():
        def body(k, carry):
            r = pl.multiple_of(k * NORM_ROWS, NORM_ROWS)
            h = _modulated_rms_norm(x_ref[pl.ds(r, NORM_ROWS), :], g_ref[...], sc_ref[...], sh_ref[...])
            h_ref[pl.ds(r, NORM_ROWS), :] = h.astype(BF16)
            return carry
        lax.fori_loop(0, tm // NORM_ROWS, body, 0)
        if layout == "residue_major":
            h_ref[...] = jnp.dot(perm_ref[...], h_ref[...], preferred_element_type=F32).astype(BF16)

    step = pl.program_id(0) * pl.num_programs(1) + pl.program_id(1)
    for (body, n_blocks), src_ref, dst_ref in zip(side_casts, side_in, side_out):
        body(src_ref, dst_ref, jnp.minimum(step, n_blocks - 1))

    res = jnp.dot(h_ref[...], w_ref[...], preferred_element_type=F32)
    if layout == "head_major":
        for cb in range(o_ref.shape[0]):
            o_ref[cb] = res[:, cb * HEAD_DIM:(cb + 1) * HEAD_DIM].astype(BF16)
    else:
        o_ref[...] = res.reshape(o_ref.shape).astype(BF16)


def _qkv_projection(x, g, sc, sh, w, *, layout, side_casts=()):
    s, d = x.shape
    n = w.shape[1]
    tm, tn = QKV_TM, QKV_TN
    vec = pl.BlockSpec((1, d), lambda i, j: (0, 0))
    in_specs = [pl.BlockSpec((tm, d), lambda i, j: (i, 0)), vec, vec, vec,
                pl.BlockSpec((d, tn), lambda i, j: (0, j))]
    args = [x, g, sc, sh, w]
    vmem = 2 * tm * d * 4 + tm * d * 2 + 2 * d * tn * 2 + 2 * tm * tn * 2 + tm * tn * 4
    if layout == "head_major":
        out_shape = jax.ShapeDtypeStruct((n // HEAD_DIM, s, HEAD_DIM), BF16)
        out_spec = pl.BlockSpec((tn // HEAD_DIM, tm, HEAD_DIM), lambda i, j: (j, i, 0))
    else:
        assert layout == "residue_major"
        out_shape = jax.ShapeDtypeStruct((N_RESIDUES, s // N_RESIDUES, n), BF16)
        out_spec = pl.BlockSpec((N_RESIDUES, tm // N_RESIDUES, tn), lambda i, j: (0, i, j))
        in_specs.append(pl.BlockSpec((tm, tm), lambda i, j: (0, 0)))
        args.append(jnp.asarray(_residue_major_permutation(tm), BF16))
        vmem += 2 * tm * tm * 2 + tm * d * (4 + 2)
    nj = n // tn
    n_steps = (s // tm) * nj
    out_specs, out_shapes = [out_spec], [out_shape]
    for cast in side_casts:
        assert cast.n_blocks <= n_steps

        def block_of(i, j, cast=cast):
            return jnp.minimum(i * nj + j, cast.n_blocks - 1)
        in_specs.append(pl.BlockSpec(cast.in_block, lambda i, j, c=cast, b=block_of: c.in_index(b(i, j))))
        out_specs.append(pl.BlockSpec(cast.out_block, lambda i, j, c=cast, b=block_of: c.out_index(b(i, j))))
        out_shapes.append(jax.ShapeDtypeStruct(cast.out_shape, BF16))
        args.append(cast.src)
        vmem += 2 * (int(np.prod(cast.in_block)) * 4 + int(np.prod(cast.out_block)) * 2)
    outs = pl.pallas_call(
        functools.partial(_qkv_kernel, layout=layout,
                          side_casts=tuple((c.body, c.n_blocks) for c in side_casts)),
        name="qkv_projection",
        grid=(s // tm, nj),
        in_specs=in_specs,
        out_specs=out_specs,
        out_shape=out_shapes,
        scratch_shapes=[pltpu.VMEM((tm, d), BF16)],
        compiler_params=_params(("arbitrary", "arbitrary"), vmem),
    )(*args)
    return outs[0] if not side_casts else outs


def _alibi_slopes(n):
    return [float(np.float32(2.0 ** (-8.0 * (i + 1) / n))) for i in range(n)]


def _dilated_kernel(q_ref, kc_ref, vc_ref, bias_ref, o_ref, lse_ref, kp_ref, vp_ref, *, n_heads):
    slots, rows_per, _ = q_ref.shape
    rows = slots * rows_per
    first_block = (pl.program_id(1) == 0).astype(jnp.int32)

    @pl.when(pl.program_id(1) == 0)
    def _():
        kp_ref[...] = jnp.zeros_like(kp_ref)
        vp_ref[...] = jnp.zeros_like(vp_ref)
    lane = lax.broadcasted_iota(jnp.int32, (rows, LANES), 1)
    to_log2 = HEAD_DIM ** -0.5 * LOG2_E
    contract_last = (((1,), (1,)), ((), ()))

    def head(ref, h):
        return ref[:, :, h * HEAD_DIM:(h + 1) * HEAD_DIM].reshape(rows, HEAD_DIM)

    scores = []
    for h in range(n_heads):
        keys = jnp.concatenate([head(kp_ref, h), head(kc_ref, h)], axis=0)
        scores.append(lax.dot_general(head(q_ref, h), keys, contract_last, preferred_element_type=F32))

    max_tile = jnp.zeros((rows, LANES), F32)
    den_tile = jnp.ones((rows, LANES), F32)
    for h in range(n_heads):
        s = scores[h] * to_log2 + bias_ref[first_block, h]
        m = jnp.max(s, axis=-1, keepdims=True)
        p = jnp.exp2(s - m)
        denom = jnp.sum(p, axis=-1, keepdims=True)
        values = jnp.concatenate([head(vp_ref, h), head(vc_ref, h)], axis=0)
        acc = jnp.dot(p.astype(BF16), values, preferred_element_type=F32)
        o_ref[:, :, h * HEAD_DIM:(h + 1) * HEAD_DIM] = (
            (acc / denom).reshape(slots, rows_per, HEAD_DIM).astype(o_ref.dtype))
        max_tile = jnp.where(lane == h, m, max_tile)
        den_tile = jnp.where(lane == h, denom, den_tile)
    lse_ref[...] = ((max_tile + jnp.log2(den_tile)) * LN_2).reshape(lse_ref.shape)
    kp_ref[...] = kc_ref[...]
    vp_ref[...] = vc_ref[...]


def _alibi_bias_tables(dilation, span, slots, rows_per, n_heads):
    rows = slots * rows_per
    rho = np.arange(rows)
    pos = N_RESIDUES * (rho % rows_per) + _residue_of_slot(rho // rows_per)
    m = pos // dilation
    j_cur = m[:, None] - m[None, :]
    j = np.concatenate([j_cur + rows, j_cur], axis=1)
    valid = (j >= 0) & (j <= span)
    valid = jnp.asarray(np.stack([valid, valid & (np.arange(2 * rows) >= rows)[None, :]]))
    dist = jnp.asarray((j * dilation).astype(np.float32))
    slopes = jnp.asarray(_alibi_slopes(n_heads), F32)
    bias = -(slopes[:, None, None] * dist[None]) * LOG2_E
    return jnp.where(valid[:, None], bias[None], NEG_BIG)


def _dilated_group_attention(qkv, group, *, n_heads):
    window, dilation = DILATED_CONFIGS[group]
    n_res, s_per, _ = qkv.shape
    dm = n_heads * HEAD_DIM
    span = window // dilation
    assert n_res == N_RESIDUES and N_RESIDUES % dilation == 0 and n_heads <= LANES
    assert window % dilation == 0 and span <= ATTN_SPAN_MAX
    slots = N_RESIDUES // dilation
    rows_per = max(ATTN_SPAN_MAX // slots, BF16_SUBLANES)
    rows = slots * rows_per
    assert s_per % rows_per == 0 and rows >= span
    bias = _alibi_bias_tables(dilation, span, slots, rows_per, n_heads)
    base = group * 3

    def spec(which):
        return pl.BlockSpec((slots, rows_per, dm), lambda p, b: (p, b, base + which))

    return pl.pallas_call(
        functools.partial(_dilated_kernel, n_heads=n_heads),
        name=f"dilated_attention_g{group}",
        grid=(N_RESIDUES // slots, s_per // rows_per),
        in_specs=[spec(0), spec(1), spec(2),
                  pl.BlockSpec((2, n_heads, rows, 2 * rows), lambda p, b: (0, 0, 0, 0),
                               pipeline_mode=pl.Buffered(1))],
        out_specs=[pl.BlockSpec((slots, rows_per, dm), lambda p, b: (p, b, 0)),
                   pl.BlockSpec((slots, rows_per, LANES), lambda p, b: (p, b, 0))],
        out_shape=[jax.ShapeDtypeStruct((N_RESIDUES, s_per, dm), BF16),
                   jax.ShapeDtypeStruct((N_RESIDUES, s_per, LANES), F32)],
        scratch_shapes=[pltpu.VMEM((slots, rows_per, dm), BF16), pltpu.VMEM((slots, rows_per, dm), BF16)],
        compiler_params=_params(("arbitrary", "arbitrary"),
                                (2 * 4 + 2) * rows * dm * 2 + 2 * rows * LANES * 4
                                + (2 + 4 * n_heads) * rows * 2 * rows * 4),
    )(qkv, qkv, qkv, bias)


SB_TQ = 1024
SB_SUB = 128
SB_BLK = 256
SB_NEAR = 512
SB_GROUP = 4


def _sb_kernel(q_ref, k_ref, v_ref, tri2_ref, o_ref, acc_ref, c_ref):
    tq = q_ref.shape[1]
    sub, blk, near = SB_SUB, SB_BLK, SB_NEAR
    n_sub = tq // sub
    i = pl.program_id(1)
    to_log2 = HEAD_DIM ** -0.5 * LOG2_E
    contract_last = (((1,), (1,)), ((), ()))

    def masked_scores(r, kstart, col, limit):
        q = q_ref[0, r * sub:(r + 1) * sub, :]
        k = k_ref[0, pl.ds(kstart, col.shape[1]), :]
        y = lax.dot_general(q, k, contract_last, preferred_element_type=F32) * to_log2
        return jnp.where(col < limit, y, NEG_BIG)

    def log_terms(y):
        neg_abs = pltpu.bitcast(pltpu.bitcast(y, jnp.uint32) | jnp.uint32(0x80000000), F32)
        log_beta = jnp.minimum(y, 0.0) - jnp.log2(1.0 + jnp.exp2(neg_abs))
        log_keep = log_beta - y
        return log_beta, log_keep

    def staged_log_terms(ys):
        n_blk = ys[0].shape[1] // blk
        stage = []
        for g in range(0, len(ys), SB_GROUP):
            terms, lhs = [], []
            for y in ys[g:g + SB_GROUP]:
                log_beta, log_keep = log_terms(y)
                hi = log_keep.astype(BF16)
                lo = (log_keep - hi.astype(F32)).astype(BF16)
                lhs += [jnp.concatenate([hi[:, b * blk:(b + 1) * blk], lo[:, b * blk:(b + 1) * blk]], axis=1)
                        for b in range(n_blk)]
                terms.append((log_beta, log_keep))
            local = jnp.dot(jnp.concatenate(lhs, axis=0), tri2_ref[...], preferred_element_type=F32)
            rows = n_blk * sub
            stage += [(lb, lk, local[k * rows:(k + 1) * rows, :]) for k, (lb, lk) in enumerate(terms)]
        return stage

    n_blk = near // blk
    col_minus_row = (lax.broadcasted_iota(jnp.int32, (sub, near), 1)
                     - lax.broadcasted_iota(jnp.int32, (sub, near), 0))
    kstarts, ys = [], []
    for r in range(n_sub):
        q_start = i * tq + r * sub
        kstart = pl.multiple_of(jnp.maximum(q_start + sub - near, 0), sub)
        kstarts.append(kstart)
        ys.append(masked_scores(r, kstart, col_minus_row, q_start - kstart))
    stage = staged_log_terms(ys)
    for r in range(n_sub):
        log_beta, log_keep, local = stage[r]
        newer = jnp.zeros((sub, 1), F32)
        after = [None] * n_blk
        for b in reversed(range(n_blk)):
            loc = local[b * sub:(b + 1) * sub, :]
            after[b] = loc + newer if b < n_blk - 1 else loc
            newer = newer + (loc[:, 0:1] + log_keep[:, b * blk:b * blk + 1])
        a = jnp.exp2(log_beta + jnp.concatenate(after, axis=1))
        v = v_ref[0, pl.ds(kstarts[r], near), :]
        acc_ref[r * sub:(r + 1) * sub, :] = jnp.dot(a.astype(BF16), v, preferred_element_type=F32)
        c_ref[r * sub:(r + 1) * sub, :] = newer

    def cond(carry):
        m, c_max = carry
        return (kstarts[n_sub - 1] - m * blk > 0) & (c_max > -F32_EXP2_UNDERFLOW)

    def body(carry):
        m, _ = carry
        col = lax.broadcasted_iota(jnp.int32, (sub, blk), 1)
        wins, ys = [], []
        for r in range(n_sub):
            kend = kstarts[r] - m * blk
            wstart = pl.multiple_of(jnp.maximum(kend - blk, 0), sub)
            wins.append(wstart)
            ys.append(masked_scores(r, wstart, col, kend - wstart))
        stage = staged_log_terms(ys)
        c_old = c_ref[...]
        outs, c_news = [], []
        for r in range(n_sub):
            log_beta, log_keep, local = stage[r]
            after = local + c_old[r * sub:(r + 1) * sub, :]
            a = jnp.exp2(log_beta + after)
            v = v_ref[0, pl.ds(wins[r], blk), :]
            outs.append(jnp.dot(a.astype(BF16), v, preferred_element_type=F32))
            c_news.append(after[:, 0:1] + log_keep[:, 0:1])
        c_new = jnp.concatenate(c_news, axis=0)
        acc_ref[...] += jnp.concatenate(outs, axis=0)
        c_ref[...] = c_new
        return m + 1, jnp.max(c_new)

    lax.while_loop(cond, body, (0, jnp.max(c_ref[...])))
    o_ref[...] = acc_ref[...].astype(o_ref.dtype)


def _stick_breaking_attention(qkv, *, n_heads):
    _, s, _ = qkv.shape
    tq, sub, blk, near = SB_TQ, SB_SUB, SB_BLK, SB_NEAR
    assert s % tq == 0 and tq % sub == 0 and near % blk == 0 and blk % sub == 0 and s >= near
    tri = np.tril(np.ones((blk, blk), np.float32), k=-1)
    tri2 = jnp.asarray(np.concatenate([tri, tri], axis=0), BF16)
    vmem = (2 * 2 * s * HEAD_DIM * 2 + 2 * 2 * blk * blk * 2 + 4 * tq * HEAD_DIM * 2
            + tq * LANES * 4 * 2 + 10 * tq * near * 4)
    return pl.pallas_call(
        _sb_kernel,
        name="stick_breaking_attention",
        grid=(n_heads, s // tq),
        in_specs=[pl.BlockSpec((1, tq, HEAD_DIM), lambda h, i: (h, i, 0)),
                  pl.BlockSpec((1, s, HEAD_DIM), lambda h, i: (n_heads + h, 0, 0)),
                  pl.BlockSpec((1, s, HEAD_DIM), lambda h, i: (2 * n_heads + h, 0, 0)),
                  pl.BlockSpec((2 * blk, blk), lambda h, i: (0, 0))],
        out_specs=pl.BlockSpec((tq, HEAD_DIM), lambda h, i: (i, h)),
        out_shape=jax.ShapeDtypeStruct((s, n_heads * HEAD_DIM), BF16),
        scratch_shapes=[pltpu.VMEM((tq, HEAD_DIM), F32), pltpu.VMEM((tq, 1), F32)],
        compiler_params=_params(("arbitrary", "arbitrary"), vmem),
    )(qkv, qkv, qkv, tri2)


OUT_TM = 512
OUT_HEAD_GROUPS = 4


def _out_kernel(*refs, n_groups, n_heads):
    o_refs = refs[:n_groups]
    lse_refs = refs[n_groups:2 * n_groups] if n_groups > 1 else ()
    rest = refs[len(o_refs) + len(lse_refs):]
    w_ref, x_ref, gate_ref, g_ref, sc_ref, sh_ref = rest[:6]
    if n_groups == 1:
        xo_ref, h_ref = rest[6:]
    else:
        unperm_ref, xo_ref, h_ref, merged_ref = rest[6:]
    tm = x_ref.shape[0]

    if n_groups == 1:
        y = jnp.dot(o_refs[0][...], w_ref[...], preferred_element_type=F32)
    else:
        rows_per = tm // N_RESIDUES
        wts = []
        for k in range(N_RESIDUES):
            lses = [lr[k] for lr in lse_refs]
            m = functools.reduce(jnp.maximum, lses)
            es = [jnp.exp(l - m) for l in lses]
            inv = 1.0 / functools.reduce(jnp.add, es)
            wts.append([e * inv for e in es[1:]])
        heads_per = n_heads // OUT_HEAD_GROUPS
        y = None
        for c in range(OUT_HEAD_GROUPS):
            for k in range(N_RESIDUES):
                rows = slice(k * rows_per, (k + 1) * rows_per)
                for h in range(c * heads_per, (c + 1) * heads_per):
                    cols = slice(h * HEAD_DIM, (h + 1) * HEAD_DIM)
                    base = o_refs[0][k, :, cols].astype(F32)
                    acc = base
                    for g in range(1, n_groups):
                        acc = acc + wts[k][g - 1][:, h:h + 1] * (o_refs[g][k, :, cols].astype(F32) - base)
                    merged_ref[rows, cols] = acc.astype(BF16)
            gcols = slice(c * heads_per * HEAD_DIM, (c + 1) * heads_per * HEAD_DIM)
            mixed = jnp.dot(unperm_ref[...], merged_ref[:, gcols], preferred_element_type=F32).astype(BF16)
            part = jnp.dot(mixed, w_ref[gcols, :], preferred_element_type=F32)
            y = part if y is None else y + part

    xo_ref[...] = x_ref[...] + gate_ref[...] * y

    def norm(k, carry):
        r = pl.multiple_of(k * NORM_ROWS, NORM_ROWS)
        h = _modulated_rms_norm(xo_ref[pl.ds(r, NORM_ROWS), :], g_ref[...], sc_ref[...], sh_ref[...])
        h_ref[pl.ds(r, NORM_ROWS), :] = h.astype(BF16)
        return carry
    lax.fori_loop(0, tm // NORM_ROWS, norm, 0, unroll=2)


def _out_projection(outs, lses, w, x, gate, g, sc, sh, *, n_heads):
    s, d = x.shape
    dm = w.shape[0]
    n_groups = len(outs)
    tm = OUT_TM
    row_d = pl.BlockSpec((tm, d), lambda i: (i, 0))
    vec = pl.BlockSpec((1, d), lambda i: (0, 0))
    scratch = []
    if n_groups == 1:
        in_specs = [pl.BlockSpec((tm, dm), lambda i: (i, 0))]
        args = list(outs)
    else:
        rows_per = tm // N_RESIDUES
        in_specs = ([pl.BlockSpec((N_RESIDUES, rows_per, dm), lambda i: (0, i, 0))] * n_groups
                    + [pl.BlockSpec((N_RESIDUES, rows_per, LANES), lambda i: (0, i, 0))] * n_groups)
        args = list(outs) + list(lses)
        scratch = [pltpu.VMEM((tm, dm), BF16)]
    in_specs += [pl.BlockSpec((dm, d), lambda i: (0, 0)), row_d, vec, vec, vec, vec]
    args += [w, x, gate, g, sc, sh]
    if n_groups > 1:
        in_specs.append(pl.BlockSpec((tm, tm), lambda i: (0, 0)))
        args.append(jnp.asarray(_residue_major_permutation(tm).T, BF16))
    vmem = (2 * n_groups * tm * dm * 2 + 2 * dm * d * 2 + 2 * tm * d * 4 + 2 * tm * d * 4
            + 2 * tm * d * 2 + tm * dm * 2 + tm * d * 4)
    return pl.pallas_call(
        functools.partial(_out_kernel, n_groups=n_groups, n_heads=n_heads),
        name="mixer_out_projection",
        grid=(s // tm,),
        in_specs=in_specs,
        out_specs=[row_d, row_d],
        out_shape=[jax.ShapeDtypeStruct((s, d), F32), jax.ShapeDtypeStruct((s, d), BF16)],
        scratch_shapes=scratch,
        compiler_params=_params(("arbitrary",), vmem),
    )(*args)


FFN_TM = 1024
FFN_TF = 512
FFN_ROW_SPLIT = 2


def _causal_conv(u_ref, row0, rows, cw_ref, cb_ref):
    def at(shift):
        return u_ref[SUBLANES + row0 - shift:SUBLANES + row0 - shift + rows, :]

    y = cb_ref[...] + cw_ref[0:1, :] * at(2)
    y = y + cw_ref[1:2, :] * at(1)
    return y + cw_ref[2:3, :] * at(0)


def _ffn_kernel(h_ref, wa_ref, wg_ref, cwa_ref, cwg_ref, cba_ref, cbg_ref, wd_ref, x_ref, gate_ref,
                fg_ref, o_ref, tail_a_ref, tail_g_ref, ua_ref, ug_ref, *, final_norm):
    i = pl.program_id(0)
    j = pl.program_id(1)
    tm = h_ref.shape[0]

    @pl.when(i == 0)
    def _():
        tail_a_ref[j] = jnp.zeros(tail_a_ref.shape[1:], F32)
        tail_g_ref[j] = jnp.zeros(tail_g_ref.shape[1:], F32)

    @pl.when(j == 0)
    def _():
        o_ref[...] = jnp.zeros_like(o_ref)

    ua_ref[0:SUBLANES, :] = tail_a_ref[j]
    ug_ref[0:SUBLANES, :] = tail_g_ref[j]

    sub = tm // FFN_ROW_SPLIT
    for r in range(FFN_ROW_SPLIT):
        h = h_ref[r * sub:(r + 1) * sub, :]
        rows = slice(SUBLANES + r * sub, SUBLANES + (r + 1) * sub)
        ua_ref[rows, :] = jnp.dot(h, wa_ref[...], preferred_element_type=F32)
        ug_ref[rows, :] = jnp.dot(h, wg_ref[...], preferred_element_type=F32)
    for r in range(FFN_ROW_SPLIT):
        ya = _causal_conv(ua_ref, r * sub, sub, cwa_ref, cba_ref)
        yg = _causal_conv(ug_ref, r * sub, sub, cwg_ref, cbg_ref)
        act = (yg * (1.0 / (1.0 + jnp.exp(-yg))) * ya).astype(BF16)
        o_ref[r * sub:(r + 1) * sub, :] += jnp.dot(act, wd_ref[...], preferred_element_type=F32)
    tail_a_ref[j] = ua_ref[tm:tm + SUBLANES, :]
    tail_g_ref[j] = ug_ref[tm:tm + SUBLANES, :]

    @pl.when(j == pl.num_programs(1) - 1)
    def _():
        def fin(k, carry):
            r = pl.multiple_of(k * NORM_ROWS, NORM_ROWS)
            rows = pl.ds(r, NORM_ROWS)
            xn = x_ref[rows, :] + gate_ref[...] * o_ref[rows, :]
            if final_norm:
                xn = (xn * lax.rsqrt(jnp.mean(xn * xn, axis=-1, keepdims=True) + EPS)) * fg_ref[...]
            o_ref[rows, :] = xn
            return carry
        lax.fori_loop(0, tm // NORM_ROWS, fin, 0)


def _pad_halves(a, d_ff, d_ff_pad):
    pad = [(0, 0)] * (a.ndim - 1) + [(0, d_ff_pad - d_ff)]
    return jnp.concatenate([jnp.pad(a[..., :d_ff], pad), jnp.pad(a[..., d_ff:], pad)], axis=-1)


def _conv_ffn(h, layer, w_up_p, conv_w, conv_b, w_down_p, x, gate, final_g, *, final_norm):
    s, d = x.shape
    d_ff = conv_w.shape[1] // 2
    tm, tf = FFN_TM, FFN_TF
    d_ff_pad = w_down_p.shape[1]
    nj = d_ff_pad // tf
    conv_w_p = _pad_halves(conv_w, d_ff, d_ff_pad)
    conv_b_p = _pad_halves(conv_b.reshape(1, -1), d_ff, d_ff_pad)
    vec = pl.BlockSpec((1, d), lambda i, j: (0, 0))
    vmem = (2 * tm * d * 2 + 2 * 2 * d * tf * 2 + 2 * tf * d * 2 + tm * d * 4 + 2 * tm * d * 4
            + 2 * nj * SUBLANES * tf * 4 + 8 * tm * tf * 4)
    return pl.pallas_call(
        functools.partial(_ffn_kernel, final_norm=final_norm),
        name="conv_ffn",
        grid=(s // tm, nj),
        in_specs=[pl.BlockSpec((tm, d), lambda i, j: (i, 0)),
                  pl.BlockSpec((None, d, tf), lambda i, j: (layer, 0, j)),
                  pl.BlockSpec((None, d, tf), lambda i, j: (layer, 0, nj + j)),
                  pl.BlockSpec((CONV_WIDTH, tf), lambda i, j: (0, j)),
                  pl.BlockSpec((CONV_WIDTH, tf), lambda i, j: (0, nj + j)),
                  pl.BlockSpec((1, tf), lambda i, j: (0, j)),
                  pl.BlockSpec((1, tf), lambda i, j: (0, nj + j)),
                  pl.BlockSpec((None, tf, d), lambda i, j: (layer, j, 0)),
                  pl.BlockSpec((tm, d), lambda i, j: (i, 0), pipeline_mode=pl.Buffered(1)),
                  vec, vec],
        out_specs=pl.BlockSpec((tm, d), lambda i, j: (i, 0)),
        out_shape=jax.ShapeDtypeStruct((s, d), F32),
        scratch_shapes=[pltpu.VMEM((nj, SUBLANES, tf), F32), pltpu.VMEM((nj, SUBLANES, tf), F32),
                        pltpu.VMEM((SUBLANES + tm, tf), F32), pltpu.VMEM((SUBLANES + tm, tf), F32)],
        compiler_params=_params(("arbitrary", "arbitrary"), vmem),
    )(h, w_up_p, w_up_p, conv_w_p, conv_w_p, conv_b_p, conv_b_p, w_down_p, x, gate, final_g.reshape(1, d))


def kernel(x, c, norm_mix_g, norm_ffn_g, ada_w, ada_b, w_in_a, w_out_a, w_in_b, w_out_b,
           w_up, conv_w, conv_b, w_down, final_g):
    batch, s, d = x.shape
    assert batch == 1, "the sequence is processed as one (S, D) slab"
    depth = ada_w.shape[0]
    n_heads = w_out_a.shape[1] // HEAD_DIM
    n_groups = len(DILATED_CONFIGS)

    xs = x.reshape(s, d)
    mod = _ada_modulation(c, ada_w, ada_b)
    d_ff = w_down.shape[1]
    d_ff_pad = pl.cdiv(d_ff, FFN_TF) * FFN_TF
    n_steps = (s // QKV_TM) * (w_in_a.shape[2] // QKV_TN)
    side_casts = [_side_cast_up(w_up, d_ff, d_ff_pad, n_steps), _side_cast_down(w_down, d_ff_pad, n_steps),
                  _side_cast_rows(w_in_b, n_steps), _side_cast_rows(w_out_a, n_steps),
                  _side_cast_rows(w_out_b, n_steps)]
    w_up_p = w_down_p = w_in_b16 = w_out_a16 = w_out_b16 = None
    for i in range(depth):
        sh1, sc1, g1, sh2, sc2, g2 = [mod[i, :, k * d:(k + 1) * d] for k in range(N_MOD)]
        norm_g = norm_mix_g[i].reshape(1, d)
        ffn_g = norm_ffn_g[i].reshape(1, d)
        if i % 2 == 0:
            w_in = w_in_a[i // 2].astype(BF16)
            if i == 0:
                qkv, w_up_p, w_down_p, w_in_b16, w_out_a16, w_out_b16 = _qkv_projection(
                    xs, norm_g, sc1, sh1, w_in, layout="residue_major", side_casts=side_casts)
            else:
                qkv = _qkv_projection(xs, norm_g, sc1, sh1, w_in, layout="residue_major")
            outs, lses = zip(*[_dilated_group_attention(qkv, g, n_heads=n_heads) for g in range(n_groups)])
            w_out = w_out_a16[i // 2]
        else:
            qkv = _qkv_projection(xs, norm_g, sc1, sh1, w_in_b16[i // 2], layout="head_major")
            outs, lses = [_stick_breaking_attention(qkv, n_heads=n_heads)], None
            w_out = w_out_b16[i // 2]
        xs, h2 = _out_projection(outs, lses, w_out, xs, g1, ffn_g, sc2, sh2, n_heads=n_heads)
        xs = _conv_ffn(h2, i, w_up_p, conv_w[i], conv_b[i], w_down_p, xs, g2, final_g,
                       final_norm=(i == depth - 1))
    return xs.reshape(batch, s, d)
```

```python
import functools
import math
from typing import Callable, NamedTuple

import numpy as np
import jax
import jax.numpy as jnp
from jax import lax
from jax.experimental import pallas as pl
from jax.experimental.pallas import tpu as pltpu

HEAD_DIM = 128
DILATED_CONFIGS = ((128, 1), (512, 4), (2048, 16))
ATTN_SPAN_MAX = 128
CONV_WIDTH = 3
EPS = 1e-6
N_MOD = 6
NEG_BIG = -1e30
N_RESIDUES = 16
MID_DILATION = 4

LANES = 128
SUBLANES = 8
BF16_SUBLANES = 16
V7X_VMEM_BYTES = 64 * 1024 * 1024
VMEM_HEADROOM_BYTES = 6 * 1024 * 1024

F32_EXP2_UNDERFLOW = 150.0
LOG2_E = 1.4426950408889634
LN_2 = 0.6931471805599453

F32 = jnp.float32
BF16 = jnp.bfloat16


def _params(semantics, vmem_estimate_bytes):
    limit = min(int(vmem_estimate_bytes) + VMEM_HEADROOM_BYTES, V7X_VMEM_BYTES - VMEM_HEADROOM_BYTES)
    return pltpu.CompilerParams(dimension_semantics=semantics, vmem_limit_bytes=limit)


def _modulated_rms_norm(x, g, sc, sh):
    y = x * lax.rsqrt(jnp.mean(x * x, axis=-1, keepdims=True) + EPS)
    return (y * g) * (1.0 + sc) + sh


ADA_TN = 1024
ADA_ROWS = 256


def _ada_kernel(c_ref, w_ref, b_ref, o_ref):
    d = w_ref.shape[1]
    tn = w_ref.shape[2]

    def body(k, acc):
        r = pl.multiple_of(k * ADA_ROWS, ADA_ROWS)
        prod = c_ref[pl.ds(r, ADA_ROWS), :] * w_ref[0, pl.ds(r, ADA_ROWS), :]
        return acc + prod.reshape(ADA_ROWS // SUBLANES, SUBLANES, tn).sum(axis=0)

    acc = lax.fori_loop(0, d // ADA_ROWS, body, jnp.zeros((SUBLANES, tn), F32))
    o_ref[0] = acc.sum(axis=0, keepdims=True) + b_ref[0]


def _ada_modulation(c, ada_w, ada_b):
    depth, d, n = ada_w.shape
    c_col = c.reshape(d, 1)
    return pl.pallas_call(
        _ada_kernel,
        name="ada_modulation",
        grid=(depth, n // ADA_TN),
        in_specs=[
            pl.BlockSpec((d, 1), lambda l, j: (0, 0)),
            pl.BlockSpec((1, d, ADA_TN), lambda l, j: (l, 0, j)),
            pl.BlockSpec((1, 1, ADA_TN), lambda l, j: (l, 0, j)),
        ],
        out_specs=pl.BlockSpec((1, 1, ADA_TN), lambda l, j: (l, 0, j)),
        out_shape=jax.ShapeDtypeStruct((depth, 1, n), F32),
        compiler_params=_params(("arbitrary", "arbitrary"),
                                2 * d * ADA_TN * 4 + d * LANES * 4),
    )(c_col, ada_w, ada_b.reshape(depth, 1, n))


QKV_TM = 1024
QKV_TN = 1024
NORM_ROWS = 64


def _residue_of_slot(slot):
    per = N_RESIDUES // MID_DILATION
    return MID_DILATION * (slot % per) + slot // per


def _residue_major_permutation(rows):
    rho = np.arange(rows)
    rows_per = rows // N_RESIDUES
    src = N_RESIDUES * (rho % rows_per) + _residue_of_slot(rho // rows_per)
    perm = np.zeros((rows, rows), np.float32)
    perm[rho, src] = 1.0
    return perm


def _side_block_rows(rows, layers, n_steps, granule):
    for r in range(granule, rows + 1, granule):
        if rows % r == 0 and layers * (rows // r) <= n_steps:
            return r
    raise ValueError("side cast does not fit in the host call's grid")


class _SideCast(NamedTuple):
    src: jax.Array
    in_block: tuple
    in_index: Callable
    out_shape: tuple
    out_block: tuple
    out_index: Callable
    n_blocks: int
    body: Callable


def _side_cast_rows(w, n_steps):
    layers, rows, cols = w.shape
    block_rows = _side_block_rows(rows, layers, n_steps, BF16_SUBLANES)
    per = rows // block_rows

    def index(b):
        return (b // per, b % per, 0)

    def body(src_ref, dst_ref, b):
        dst_ref[...] = src_ref[...].astype(BF16)
    block = (1, block_rows, cols)
    return _SideCast(w, block, index, w.shape, block, index, layers * per, body)


def _side_cast_up(w_up, d_ff, d_ff_pad, n_steps):
    layers, d, _ = w_up.shape
    assert d_ff % LANES == 0 and d_ff_pad % LANES == 0
    block_rows = _side_block_rows(d, layers, n_steps, BF16_SUBLANES)
    per = d // block_rows

    def index(b):
        return (b // per, b % per, 0)

    def body(src_ref, dst_ref, b):
        for half in range(2):
            dst_ref[0, :, half * d_ff_pad:half * d_ff_pad + d_ff] = (
                src_ref[0, :, half * d_ff:(half + 1) * d_ff].astype(BF16))
            if d_ff_pad > d_ff:
                dst_ref[0, :, half * d_ff_pad + d_ff:(half + 1) * d_ff_pad] = (
                    jnp.zeros((block_rows, d_ff_pad - d_ff), BF16))
    return _SideCast(w_up, (1, block_rows, 2 * d_ff), index, (layers, d, 2 * d_ff_pad),
                     (1, block_rows, 2 * d_ff_pad), index, layers * per, body)


def _side_cast_down(w_down, d_ff_pad, n_steps):
    layers, d_ff, d = w_down.shape
    common = math.gcd(d_ff, d_ff_pad)
    rows = _side_block_rows(common, layers * (d_ff_pad // common), n_steps, BF16_SUBLANES)
    src_per, per = d_ff // rows, d_ff_pad // rows

    def in_index(b):
        return (b // per, jnp.minimum(b % per, src_per - 1), 0)

    def out_index(b):
        return (b // per, b % per, 0)

    def body(src_ref, dst_ref, b):
        dst_ref[...] = jnp.where(b % per < src_per, src_ref[...], 0.0).astype(BF16)
    return _SideCast(w_down, (1, rows, d), in_index, (layers, d_ff_pad, d), (1, rows, d), out_index,
                     layers * per, body)


def _qkv_kernel(*refs, layout, side_casts):
    n_side = len(side_casts)
    n_in = 6 if layout == "residue_major" else 5
    x_ref, g_ref, sc_ref, sh_ref, w_ref = refs[:5]
    perm_ref = refs[5] if layout == "residue_major" else None
    side_in = refs[n_in:n_in + n_side]
    o_ref = refs[n_in + n_side]
    side_out = refs[n_in + n_side + 1:n_in + 2 * n_side + 1]
    h_ref = refs[-1]
    tm = x_ref.shape[0]

    @pl.when(pl.program_id(1) == 0)
    def _():
        def body(k, carry):
            r = pl.multiple_of(k * NORM_ROWS, NORM_ROWS)
            h = _modulated_rms_norm(x_ref[pl.ds(r, NORM_ROWS), :], g_ref[...], sc_ref[...], sh_ref[...])
            h_ref[pl.ds(r, NORM_ROWS), :] = h.astype(BF16)
            return carry
        lax.fori_loop(0, tm // NORM_ROWS, body, 0)
        if layout == "residue_major":
            h_ref[...] = jnp.dot(perm_ref[...], h_ref[...], preferred_element_type=F32).astype(BF16)

    step = pl.program_id(0) * pl.num_programs(1) + pl.program_id(1)
    for (body, n_blocks), src_ref, dst_ref in zip(side_casts, side_in, side_out):
        body(src_ref, dst_ref, jnp.minimum(step, n_blocks - 1))

    res = jnp.dot(h_ref[...], w_ref[...], preferred_element_type=F32)
    if layout == "head_major":
        for cb in range(o_ref.shape[0]):
            o_ref[cb] = res[:, cb * HEAD_DIM:(cb + 1) * HEAD_DIM].astype(BF16)
    else:
        o_ref[...] = res.reshape(o_ref.shape).astype(BF16)


def _qkv_projection(x, g, sc, sh, w, *, layout, side_casts=()):
    s, d = x.shape
    n = w.shape[1]
    tm, tn = QKV_TM, QKV_TN
    vec = pl.BlockSpec((1, d), lambda i, j: (0, 0))
    in_specs = [pl.BlockSpec((tm, d), lambda i, j: (i, 0)), vec, vec, vec,
                pl.BlockSpec((d, tn), lambda i, j: (0, j))]
    args = [x, g, sc, sh, w]
    vmem = 2 * tm * d * 4 + tm * d * 2 + 2 * d * tn * 2 + 2 * tm * tn * 2 + tm * tn * 4
    if layout == "head_major":
        out_shape = jax.ShapeDtypeStruct((n // HEAD_DIM, s, HEAD_DIM), BF16)
        out_spec = pl.BlockSpec((tn // HEAD_DIM, tm, HEAD_DIM), lambda i, j: (j, i, 0))
    else:
        assert layout == "residue_major"
        out_shape = jax.ShapeDtypeStruct((N_RESIDUES, s // N_RESIDUES, n), BF16)
        out_spec = pl.BlockSpec((N_RESIDUES, tm // N_RESIDUES, tn), lambda i, j: (0, i, j))
        in_specs.append(pl.BlockSpec((tm, tm), lambda i, j: (0, 0)))
        args.append(jnp.asarray(_residue_major_permutation(tm), BF16))
        vmem += 2 * tm * tm * 2 + tm * d * (4 + 2)
    nj = n // tn
    n_steps = (s // tm) * nj
    out_specs, out_shapes = [out_spec], [out_shape]
    for cast in side_casts:
        assert cast.n_blocks <= n_steps

        def block_of(i, j, cast=cast):
            return jnp.minimum(i * nj + j, cast.n_blocks - 1)
        in_specs.append(pl.BlockSpec(cast.in_block, lambda i, j, c=cast, b=block_of: c.in_index(b(i, j))))
        out_specs.append(pl.BlockSpec(cast.out_block, lambda i, j, c=cast, b=block_of: c.out_index(b(i, j))))
        out_shapes.append(jax.ShapeDtypeStruct(cast.out_shape, BF16))
        args.append(cast.src)
        vmem += 2 * (int(np.prod(cast.in_block)) * 4 + int(np.prod(cast.out_block)) * 2)
    outs = pl.pallas_call(
        functools.partial(_qkv_kernel, layout=layout,
                          side_casts=tuple((c.body, c.n_blocks) for c in side_casts)),
        name="qkv_projection",
        grid=(s // tm, nj),
        in_specs=in_specs,
        out_specs=out_specs,
        out_shape=out_shapes,
        scratch_shapes=[pltpu.VMEM((tm, d), BF16)],
        compiler_params=_params(("arbitrary", "arbitrary"), vmem),
    )(*args)
    return outs[0] if not side_casts else outs


def _alibi_slopes(n):
    return [float(np.float32(2.0 ** (-8.0 * (i + 1) / n))) for i in range(n)]


def _dilated_kernel(q_ref, kc_ref, vc_ref, bias_ref, o_ref, lse_ref, kp_ref, vp_ref, *, n_heads):
    slots, rows_per, _ = q_ref.shape
    rows = slots * rows_per
    first_block = (pl.program_id(1) == 0).astype(jnp.int32)

    @pl.when(pl.program_id(1) == 0)
    def _():
        kp_ref[...] = jnp.zeros_like(kp_ref)
        vp_ref[...] = jnp.zeros_like(vp_ref)
    lane = lax.broadcasted_iota(jnp.int32, (rows, LANES), 1)
    to_log2 = HEAD_DIM ** -0.5 * LOG2_E
    contract_last = (((1,), (1,)), ((), ()))

    def head(ref, h):
        return ref[:, :, h * HEAD_DIM:(h + 1) * HEAD_DIM].reshape(rows, HEAD_DIM)

    scores = []
    for h in range(n_heads):
        keys = jnp.concatenate([head(kp_ref, h), head(kc_ref, h)], axis=0)
        scores.append(lax.dot_general(head(q_ref, h), keys, contract_last, preferred_element_type=F32))

    max_tile = jnp.zeros((rows, LANES), F32)
    den_tile = jnp.ones((rows, LANES), F32)
    for h in range(n_heads):
        s = scores[h] * to_log2 + bias_ref[first_block, h]
        m = jnp.max(s, axis=-1, keepdims=True)
        p = jnp.exp2(s - m)
        denom = jnp.sum(p, axis=-1, keepdims=True)
        values = jnp.concatenate([head(vp_ref, h), head(vc_ref, h)], axis=0)
        acc = jnp.dot(p.astype(BF16), values, preferred_element_type=F32)
        o_ref[:, :, h * HEAD_DIM:(h + 1) * HEAD_DIM] = (
            (acc / denom).reshape(slots, rows_per, HEAD_DIM).astype(o_ref.dtype))
        max_tile = jnp.where(lane == h, m, max_tile)
        den_tile = jnp.where(lane == h, denom, den_tile)
    lse_ref[...] = ((max_tile + jnp.log2(den_tile)) * LN_2).reshape(lse_ref.shape)
    kp_ref[...] = kc_ref[...]
    vp_ref[...] = vc_ref[...]


def _alibi_bias_tables(dilation, span, slots, rows_per, n_heads):
    rows = slots * rows_per
    rho = np.arange(rows)
    pos = N_RESIDUES * (rho % rows_per) + _residue_of_slot(rho // rows_per)
    m = pos // dilation
    j_cur = m[:, None] - m[None, :]
    j = np.concatenate([j_cur + rows, j_cur], axis=1)
    valid = (j >= 0) & (j <= span)
    valid = jnp.asarray(np.stack([valid, valid & (np.arange(2 * rows) >= rows)[None, :]]))
    dist = jnp.asarray((j * dilation).astype(np.float32))
    slopes = jnp.asarray(_alibi_slopes(n_heads), F32)
    bias = -(slopes[:, None, None] * dist[None]) * LOG2_E
    return jnp.where(valid[:, None], bias[None], NEG_BIG)


def _dilated_group_attention(qkv, group, *, n_heads):
    window, dilation = DILATED_CONFIGS[group]
    n_res, s_per, _ = qkv.shape
    dm = n_heads * HEAD_DIM
    span = window // dilation
    assert n_res == N_RESIDUES and N_RESIDUES % dilation == 0 and n_heads <= LANES
    assert window % dilation == 0 and span <= ATTN_SPAN_MAX
    slots = N_RESIDUES // dilation
    rows_per = max(ATTN_SPAN_MAX // slots, BF16_SUBLANES)
    rows = slots * rows_per
    assert s_per % rows_per == 0 and rows >= span
    bias = _alibi_bias_tables(dilation, span, slots, rows_per, n_heads)
    base = group * 3

    def spec(which):
        return pl.BlockSpec((slots, rows_per, dm), lambda p, b: (p, b, base + which))

    return pl.pallas_call(
        functools.partial(_dilated_kernel, n_heads=n_heads),
        name=f"dilated_attention_g{group}",
        grid=(N_RESIDUES // slots, s_per // rows_per),
        in_specs=[spec(0), spec(1), spec(2),
                  pl.BlockSpec((2, n_heads, rows, 2 * rows), lambda p, b: (0, 0, 0, 0),
                               pipeline_mode=pl.Buffered(1))],
        out_specs=[pl.BlockSpec((slots, rows_per, dm), lambda p, b: (p, b, 0)),
                   pl.BlockSpec((slots, rows_per, LANES), lambda p, b: (p, b, 0))],
        out_shape=[jax.ShapeDtypeStruct((N_RESIDUES, s_per, dm), BF16),
                   jax.ShapeDtypeStruct((N_RESIDUES, s_per, LANES), F32)],
        scratch_shapes=[pltpu.VMEM((slots, rows_per, dm), BF16), pltpu.VMEM((slots, rows_per, dm), BF16)],
        compiler_params=_params(("arbitrary", "arbitrary"),
                                (2 * 4 + 2) * rows * dm * 2 + 2 * rows * LANES * 4
                                + (2 + 4 * n_heads) * rows * 2 * rows * 4),
    )(qkv, qkv, qkv, bias)


SB_TQ = 1024
SB_SUB = 128
SB_BLK = 256
SB_NEAR = 512
SB_GROUP = 4


def _sb_kernel(q_ref, k_ref, v_ref, tri2_ref, o_ref, acc_ref, c_ref):
    tq = q_ref.shape[1]
    sub, blk, near = SB_SUB, SB_BLK, SB_NEAR
    n_sub = tq // sub
    i = pl.program_id(1)
    to_log2 = HEAD_DIM ** -0.5 * LOG2_E
    contract_last = (((1,), (1,)), ((), ()))

    def masked_scores(r, kstart, col, limit):
        q = q_ref[0, r * sub:(r + 1) * sub, :]
        k = k_ref[0, pl.ds(kstart, col.shape[1]), :]
        y = lax.dot_general(q, k, contract_last, preferred_element_type=F32) * to_log2
        return jnp.where(col < limit, y, NEG_BIG)

    def log_terms(y):
        neg_abs = pltpu.bitcast(pltpu.bitcast(y, jnp.uint32) | jnp.uint32(0x80000000), F32)
        log_beta = jnp.minimum(y, 0.0) - jnp.log2(1.0 + jnp.exp2(neg_abs))
        log_keep = log_beta - y
        return log_beta, log_keep

    def staged_log_terms(ys):
        n_blk = ys[0].shape[1] // blk
        stage = []
        for g in range(0, len(ys), SB_GROUP):
            terms, lhs = [], []
            for y in ys[g:g + SB_GROUP]:
                log_beta, log_keep = log_terms(y)
                hi = log_keep.astype(BF16)
                lo = (log_keep - hi.astype(F32)).astype(BF16)
                lhs += [jnp.concatenate([hi[:, b * blk:(b + 1) * blk], lo[:, b * blk:(b + 1) * blk]], axis=1)
                        for b in range(n_blk)]
                terms.append((log_beta, log_keep))
            local = jnp.dot(jnp.concatenate(lhs, axis=0), tri2_ref[...], preferred_element_type=F32)
            rows = n_blk * sub
            stage += [(lb, lk, local[k * rows:(k + 1) * rows, :]) for k, (lb, lk) in enumerate(terms)]
        return stage

    n_blk = near // blk
    col_minus_row = (lax.broadcasted_iota(jnp.int32, (sub, near), 1)
                     - lax.broadcasted_iota(jnp.int32, (sub, near), 0))
    kstarts, ys = [], []
    for r in range(n_sub):
        q_start = i * tq + r * sub
        kstart = pl.multiple_of(jnp.maximum(q_start + sub - near, 0), sub)
        kstarts.append(kstart)
        ys.append(masked_scores(r, kstart, col_minus_row, q_start - kstart))
    stage = staged_log_terms(ys)
    for r in range(n_sub):
        log_beta, log_keep, local = stage[r]
        newer = jnp.zeros((sub, 1), F32)
        after = [None] * n_blk
        for b in reversed(range(n_blk)):
            loc = local[b * sub:(b + 1) * sub, :]
            after[b] = loc + newer if b < n_blk - 1 else loc
            newer = newer + (loc[:, 0:1] + log_keep[:, b * blk:b * blk + 1])
        a = jnp.exp2(log_beta + jnp.concatenate(after, axis=1))
        v = v_ref[0, pl.ds(kstarts[r], near), :]
        acc_ref[r * sub:(r + 1) * sub, :] = jnp.dot(a.astype(BF16), v, preferred_element_type=F32)
        c_ref[r * sub:(r + 1) * sub, :] = newer

    def cond(carry):
        m, c_max = carry
        return (kstarts[n_sub - 1] - m * blk > 0) & (c_max > -F32_EXP2_UNDERFLOW)

    def body(carry):
        m, _ = carry
        col = lax.broadcasted_iota(jnp.int32, (sub, blk), 1)
        wins, ys = [], []
        for r in range(n_sub):
            kend = kstarts[r] - m * blk
            wstart = pl.multiple_of(jnp.maximum(kend - blk, 0), sub)
            wins.append(wstart)
            ys.append(masked_scores(r, wstart, col, kend - wstart))
        stage = staged_log_terms(ys)
        c_old = c_ref[...]
        outs, c_news = [], []
        for r in range(n_sub):
            log_beta, log_keep, local = stage[r]
            after = local + c_old[r * sub:(r + 1) * sub, :]
            a = jnp.exp2(log_beta + after)
            v = v_ref[0, pl.ds(wins[r], blk), :]
            outs.append(jnp.dot(a.astype(BF16), v, preferred_element_type=F32))
            c_news.append(after[:, 0:1] + log_keep[:, 0:1])
        c_new = jnp.concatenate(c_news, axis=0)
        acc_ref[...] += jnp.concatenate(outs, axis=0)
        c_ref[...] = c_new
        return m + 1, jnp.max(c_new)

    lax.while_loop(cond, body, (0, jnp.max(c_ref[...])))
    o_ref[...] = acc_ref[...].astype(o_ref.dtype)


def _stick_breaking_attention(qkv, *, n_heads):
    _, s, _ = qkv.shape
    tq, sub, blk, near = SB_TQ, SB_SUB, SB_BLK, SB_NEAR
    assert s % tq == 0 and tq % sub == 0 and near % blk == 0 and blk % sub == 0 and s >= near
    tri = np.tril(np.ones((blk, blk), np.float32), k=-1)
    tri2 = jnp.asarray(np.concatenate([tri, tri], axis=0), BF16)
    vmem = (2 * 2 * s * HEAD_DIM * 2 + 2 * 2 * blk * blk * 2 + 4 * tq * HEAD_DIM * 2
            + tq * LANES * 4 * 2 + 10 * tq * near * 4)
    return pl.pallas_call(
        _sb_kernel,
        name="stick_breaking_attention",
        grid=(n_heads, s // tq),
        in_specs=[pl.BlockSpec((1, tq, HEAD_DIM), lambda h, i: (h, i, 0)),
                  pl.BlockSpec((1, s, HEAD_DIM), lambda h, i: (n_heads + h, 0, 0)),
                  pl.BlockSpec((1, s, HEAD_DIM), lambda h, i: (2 * n_heads + h, 0, 0)),
                  pl.BlockSpec((2 * blk, blk), lambda h, i: (0, 0))],
        out_specs=pl.BlockSpec((tq, HEAD_DIM), lambda h, i: (i, h)),
        out_shape=jax.ShapeDtypeStruct((s, n_heads * HEAD_DIM), BF16),
        scratch_shapes=[pltpu.VMEM((tq, HEAD_DIM), F32), pltpu.VMEM((tq, 1), F32)],
        compiler_params=_params(("arbitrary", "arbitrary"), vmem),
    )(qkv, qkv, qkv, tri2)


OUT_TM = 512
OUT_HEAD_GROUPS = 4


def _out_kernel(*refs, n_groups, n_heads):
    o_refs = refs[:n_groups]
    lse_refs = refs[n_groups:2 * n_groups] if n_groups > 1 else ()
    rest = refs[len(o_refs) + len(lse_refs):]
    w_ref, x_ref, gate_ref, g_ref, sc_ref, sh_ref = rest[:6]
    if n_groups == 1:
        xo_ref, h_ref = rest[6:]
    else:
        unperm_ref, xo_ref, h_ref, merged_ref = rest[6:]
    tm = x_ref.shape[0]

    def finish(row0, y):
        for k in range(y.shape[0] // NORM_ROWS):
            rows = slice(row0 + k * NORM_ROWS, row0 + (k + 1) * NORM_ROWS)
            xn = x_ref[rows, :] + gate_ref[...] * y[k * NORM_ROWS:(k + 1) * NORM_ROWS]
            xo_ref[rows, :] = xn
            h_ref[rows, :] = _modulated_rms_norm(xn, g_ref[...], sc_ref[...], sh_ref[...]).astype(BF16)

    if n_groups == 1:
        half = tm // 2
        ys = [jnp.dot(o_refs[0][r * half:(r + 1) * half, :], w_ref[...], preferred_element_type=F32)
              for r in range(2)]
        for r in range(2):
            finish(r * half, ys[r])
    else:
        rows_per = tm // N_RESIDUES
        wts = []
        for k in range(N_RESIDUES):
            lses = [lr[k] for lr in lse_refs]
            m = functools.reduce(jnp.maximum, lses)
            es = [jnp.exp(l - m) for l in lses]
            inv = 1.0 / functools.reduce(jnp.add, es)
            wts.append([e * inv for e in es[1:]])
        heads_per = n_heads // OUT_HEAD_GROUPS
        y = None
        for c in range(OUT_HEAD_GROUPS):
            for k in range(N_RESIDUES):
                rows = slice(k * rows_per, (k + 1) * rows_per)
                for h in range(c * heads_per, (c + 1) * heads_per):
                    cols = slice(h * HEAD_DIM, (h + 1) * HEAD_DIM)
                    base = o_refs[0][k, :, cols].astype(F32)
                    acc = base
                    for g in range(1, n_groups):
                        acc = acc + wts[k][g - 1][:, h:h + 1] * (o_refs[g][k, :, cols].astype(F32) - base)
                    merged_ref[rows, cols] = acc.astype(BF16)
            gcols = slice(c * heads_per * HEAD_DIM, (c + 1) * heads_per * HEAD_DIM)
            mixed = jnp.dot(unperm_ref[...], merged_ref[:, gcols], preferred_element_type=F32).astype(BF16)
            part = jnp.dot(mixed, w_ref[gcols, :], preferred_element_type=F32)
            y = part if y is None else y + part
        finish(0, y)


def _out_projection(outs, lses, w, x, gate, g, sc, sh, *, n_heads):
    s, d = x.shape
    dm = w.shape[0]
    n_groups = len(outs)
    tm = OUT_TM
    row_d = pl.BlockSpec((tm, d), lambda i: (i, 0))
    vec = pl.BlockSpec((1, d), lambda i: (0, 0))
    scratch = []
    if n_groups == 1:
        in_specs = [pl.BlockSpec((tm, dm), lambda i: (i, 0))]
        args = list(outs)
    else:
        rows_per = tm // N_RESIDUES
        in_specs = ([pl.BlockSpec((N_RESIDUES, rows_per, dm), lambda i: (0, i, 0))] * n_groups
                    + [pl.BlockSpec((N_RESIDUES, rows_per, LANES), lambda i: (0, i, 0))] * n_groups)
        args = list(outs) + list(lses)
        scratch = [pltpu.VMEM((tm, dm), BF16)]
    in_specs += [pl.BlockSpec((dm, d), lambda i: (0, 0)), row_d, vec, vec, vec, vec]
    args += [w, x, gate, g, sc, sh]
    if n_groups > 1:
        in_specs.append(pl.BlockSpec((tm, tm), lambda i: (0, 0)))
        args.append(jnp.asarray(_residue_major_permutation(tm).T, BF16))
    vmem = (2 * n_groups * tm * dm * 2 + 2 * dm * d * 2 + 2 * tm * d * 4 + 2 * tm * d * 4
            + 2 * tm * d * 2 + tm * dm * 2 + tm * d * 4)
    return pl.pallas_call(
        functools.partial(_out_kernel, n_groups=n_groups, n_heads=n_heads),
        name="mixer_out_projection",
        grid=(s // tm,),
        in_specs=in_specs,
        out_specs=[row_d, row_d],
        out_shape=[jax.ShapeDtypeStruct((s, d), F32), jax.ShapeDtypeStruct((s, d), BF16)],
        scratch_shapes=scratch,
        compiler_params=_params(("arbitrary",), vmem),
    )(*args)


FFN_TM = 1024
FFN_TF = 512
FFN_ROW_SPLIT = 2


def _causal_conv(u_ref, row0, rows, cw_ref, cb_ref):
    def at(shift):
        return u_ref[SUBLANES + row0 - shift:SUBLANES + row0 - shift + rows, :]

    y = cb_ref[...] + cw_ref[0:1, :] * at(2)
    y = y + cw_ref[1:2, :] * at(1)
    return y + cw_ref[2:3, :] * at(0)


def _ffn_kernel(h_ref, wa_ref, wg_ref, cwa_ref, cwg_ref, cba_ref, cbg_ref, wd_ref, x_ref, gate_ref,
                fg_ref, o_ref, tail_a_ref, tail_g_ref, ua_ref, ug_ref, *, final_norm):
    i = pl.program_id(0)
    j = pl.program_id(1)
    tm = h_ref.shape[0]

    @pl.when(i == 0)
    def _():
        tail_a_ref[j] = jnp.zeros(tail_a_ref.shape[1:], F32)
        tail_g_ref[j] = jnp.zeros(tail_g_ref.shape[1:], F32)

    @pl.when(j == 0)
    def _():
        o_ref[...] = jnp.zeros_like(o_ref)

    ua_ref[0:SUBLANES, :] = tail_a_ref[j]
    ug_ref[0:SUBLANES, :] = tail_g_ref[j]

    sub = tm // FFN_ROW_SPLIT
    for r in range(FFN_ROW_SPLIT):
        h = h_ref[r * sub:(r + 1) * sub, :]
        rows = slice(SUBLANES + r * sub, SUBLANES + (r + 1) * sub)
        ua_ref[rows, :] = jnp.dot(h, wa_ref[...], preferred_element_type=F32)
        ug_ref[rows, :] = jnp.dot(h, wg_ref[...], preferred_element_type=F32)
    for r in range(FFN_ROW_SPLIT):
        ya = _causal_conv(ua_ref, r * sub, sub, cwa_ref, cba_ref)
        yg = _causal_conv(ug_ref, r * sub, sub, cwg_ref, cbg_ref)
        act = (yg * (1.0 / (1.0 + jnp.exp(-yg))) * ya).astype(BF16)
        o_ref[r * sub:(r + 1) * sub, :] += jnp.dot(act, wd_ref[...], preferred_element_type=F32)
    tail_a_ref[j] = ua_ref[tm:tm + SUBLANES, :]
    tail_g_ref[j] = ug_ref[tm:tm + SUBLANES, :]

    @pl.when(j == pl.num_programs(1) - 1)
    def _():
        def fin(k, carry):
            r = pl.multiple_of(k * NORM_ROWS, NORM_ROWS)
            rows = pl.ds(r, NORM_ROWS)
            xn = x_ref[rows, :] + gate_ref[...] * o_ref[rows, :]
            if final_norm:
                xn = (xn * lax.rsqrt(jnp.mean(xn * xn, axis=-1, keepdims=True) + EPS)) * fg_ref[...]
            o_ref[rows, :] = xn
            return carry
        lax.fori_loop(0, tm // NORM_ROWS, fin, 0)


def _pad_halves(a, d_ff, d_ff_pad):
    pad = [(0, 0)] * (a.ndim - 1) + [(0, d_ff_pad - d_ff)]
    return jnp.concatenate([jnp.pad(a[..., :d_ff], pad), jnp.pad(a[..., d_ff:], pad)], axis=-1)


def _conv_ffn(h, layer, w_up_p, conv_w, conv_b, w_down_p, x, gate, final_g, *, final_norm):
    s, d = x.shape
    d_ff = conv_w.shape[1] // 2
    tm, tf = FFN_TM, FFN_TF
    d_ff_pad = w_down_p.shape[1]
    nj = d_ff_pad // tf
    conv_w_p = _pad_halves(conv_w, d_ff, d_ff_pad)
    conv_b_p = _pad_halves(conv_b.reshape(1, -1), d_ff, d_ff_pad)
    vec = pl.BlockSpec((1, d), lambda i, j: (0, 0))
    vmem = (2 * tm * d * 2 + 2 * 2 * d * tf * 2 + 2 * tf * d * 2 + tm * d * 4 + 2 * tm * d * 4
            + 2 * nj * SUBLANES * tf * 4 + 8 * tm * tf * 4)
    return pl.pallas_call(
        functools.partial(_ffn_kernel, final_norm=final_norm),
        name="conv_ffn",
        grid=(s // tm, nj),
        in_specs=[pl.BlockSpec((tm, d), lambda i, j: (i, 0)),
                  pl.BlockSpec((None, d, tf), lambda i, j: (layer, 0, j)),
                  pl.BlockSpec((None, d, tf), lambda i, j: (layer, 0, nj + j)),
                  pl.BlockSpec((CONV_WIDTH, tf), lambda i, j: (0, j)),
                  pl.BlockSpec((CONV_WIDTH, tf), lambda i, j: (0, nj + j)),
                  pl.BlockSpec((1, tf), lambda i, j: (0, j)),
                  pl.BlockSpec((1, tf), lambda i, j: (0, nj + j)),
                  pl.BlockSpec((None, tf, d), lambda i, j: (layer, j, 0)),
                  pl.BlockSpec((tm, d), lambda i, j: (i, 0), pipeline_mode=pl.Buffered(1)),
                  vec, vec],
        out_specs=pl.BlockSpec((tm, d), lambda i, j: (i, 0)),
        out_shape=jax.ShapeDtypeStruct((s, d), F32),
        scratch_shapes=[pltpu.VMEM((nj, SUBLANES, tf), F32), pltpu.VMEM((nj, SUBLANES, tf), F32),
                        pltpu.VMEM((SUBLANES + tm, tf), F32), pltpu.VMEM((SUBLANES + tm, tf), F32)],
        compiler_params=_params(("arbitrary", "arbitrary"), vmem),
    )(h, w_up_p, w_up_p, conv_w_p, conv_w_p, conv_b_p, conv_b_p, w_down_p, x, gate, final_g.reshape(1, d))


def kernel(x, c, norm_mix_g, norm_ffn_g, ada_w, ada_b, w_in_a, w_out_a, w_in_b, w_out_b,
           w_up, conv_w, conv_b, w_down, final_g):
    batch, s, d = x.shape
    assert batch == 1, "the sequence is processed as one (S, D) slab"
    depth = ada_w.shape[0]
    n_heads = w_out_a.shape[1] // HEAD_DIM
    n_groups = len(DILATED_CONFIGS)

    xs = x.reshape(s, d)
    mod = _ada_modulation(c, ada_w, ada_b)
    d_ff = w_down.shape[1]
    d_ff_pad = pl.cdiv(d_ff, FFN_TF) * FFN_TF
    n_steps = (s // QKV_TM) * (w_in_a.shape[2] // QKV_TN)
    side_casts = [_side_cast_up(w_up, d_ff, d_ff_pad, n_steps), _side_cast_down(w_down, d_ff_pad, n_steps),
                  _side_cast_rows(w_in_b, n_steps), _side_cast_rows(w_out_a, n_steps),
                  _side_cast_rows(w_out_b, n_steps)]
    w_up_p = w_down_p = w_in_b16 = w_out_a16 = w_out_b16 = None
    for i in range(depth):
        sh1, sc1, g1, sh2, sc2, g2 = [mod[i, :, k * d:(k + 1) * d] for k in range(N_MOD)]
        norm_g = norm_mix_g[i].reshape(1, d)
        ffn_g = norm_ffn_g[i].reshape(1, d)
        if i % 2 == 0:
            w_in = w_in_a[i // 2].astype(BF16)
            if i == 0:
                qkv, w_up_p, w_down_p, w_in_b16, w_out_a16, w_out_b16 = _qkv_projection(
                    xs, norm_g, sc1, sh1, w_in, layout="residue_major", side_casts=side_casts)
            else:
                qkv = _qkv_projection(xs, norm_g, sc1, sh1, w_in, layout="residue_major")
            outs, lses = zip(*[_dilated_group_attention(qkv, g, n_heads=n_heads) for g in range(n_groups)])
            w_out = w_out_a16[i // 2]
        else:
            qkv = _qkv_projection(xs, norm_g, sc1, sh1, w_in_b16[i // 2], layout="head_major")
            outs, lses = [_stick_breaking_attention(qkv, n_heads=n_heads)], None
            w_out = w_out_b16[i // 2]
        xs, h2 = _out_projection(outs, lses, w_out, xs, g1, ffn_g, sc2, sh2, n_heads=n_heads)
        xs = _conv_ffn(h2, i, w_up_p, conv_w[i], conv_b[i], w_down_p, xs, g2, final_g,
                       final_norm=(i == depth - 1))
    return xs.reshape(batch, s, d)
```

```python
import functools
import math
from typing import Callable, NamedTuple

import numpy as np
import jax
import jax.numpy as jnp
from jax import lax
from jax.experimental import pallas as pl
from jax.experimental.pallas import tpu as pltpu

HEAD_DIM = 128
DILATED_CONFIGS = ((128, 1), (512, 4), (2048, 16))
ATTN_SPAN_MAX = 128
CONV_WIDTH = 3
EPS = 1e-6
N_MOD = 6
NEG_BIG = -1e30
N_RESIDUES = 16
MID_DILATION = 4

LANES = 128
SUBLANES = 8
BF16_SUBLANES = 16
V7X_VMEM_BYTES = 64 * 1024 * 1024
VMEM_HEADROOM_BYTES = 6 * 1024 * 1024

F32_EXP2_UNDERFLOW = 150.0
LOG2_E = 1.4426950408889634
LN_2 = 0.6931471805599453

F32 = jnp.float32
BF16 = jnp.bfloat16


def _params(semantics, vmem_estimate_bytes):
    limit = min(int(vmem_estimate_bytes) + VMEM_HEADROOM_BYTES, V7X_VMEM_BYTES - VMEM_HEADROOM_BYTES)
    return pltpu.CompilerParams(dimension_semantics=semantics, vmem_limit_bytes=limit)


def _modulated_rms_norm(x, g, sc, sh):
    y = x * lax.rsqrt(jnp.mean(x * x, axis=-1, keepdims=True) + EPS)
    return (y * g) * (1.0 + sc) + sh


ADA_TN = 1024
ADA_ROWS = 256


def _ada_kernel(c_ref, w_ref, b_ref, o_ref):
    d = w_ref.shape[1]
    tn = w_ref.shape[2]

    def body(k, acc):
        r = pl.multiple_of(k * ADA_ROWS, ADA_ROWS)
        prod = c_ref[pl.ds(r, ADA_ROWS), :] * w_ref[0, pl.ds(r, ADA_ROWS), :]
        return acc + prod.reshape(ADA_ROWS // SUBLANES, SUBLANES, tn).sum(axis=0)

    acc = lax.fori_loop(0, d // ADA_ROWS, body, jnp.zeros((SUBLANES, tn), F32))
    o_ref[0] = acc.sum(axis=0, keepdims=True) + b_ref[0]


def _ada_modulation(c, ada_w, ada_b):
    depth, d, n = ada_w.shape
    c_col = c.reshape(d, 1)
    return pl.pallas_call(
        _ada_kernel,
        name="ada_modulation",
        grid=(depth, n // ADA_TN),
        in_specs=[
            pl.BlockSpec((d, 1), lambda l, j: (0, 0)),
            pl.BlockSpec((1, d, ADA_TN), lambda l, j: (l, 0, j)),
            pl.BlockSpec((1, 1, ADA_TN), lambda l, j: (l, 0, j)),
        ],
        out_specs=pl.BlockSpec((1, 1, ADA_TN), lambda l, j: (l, 0, j)),
        out_shape=jax.ShapeDtypeStruct((depth, 1, n), F32),
        compiler_params=_params(("arbitrary", "arbitrary"),
                                2 * d * ADA_TN * 4 + d * LANES * 4),
    )(c_col, ada_w, ada_b.reshape(depth, 1, n))


QKV_TM = 1024
QKV_TN = 1024
NORM_ROWS = 64


def _residue_of_slot(slot):
    per = N_RESIDUES // MID_DILATION
    return MID_DILATION * (slot % per) + slot // per


def _residue_major_permutation(rows):
    rho = np.arange(rows)
    rows_per = rows // N_RESIDUES
    src = N_RESIDUES * (rho % rows_per) + _residue_of_slot(rho // rows_per)
    perm = np.zeros((rows, rows), np.float32)
    perm[rho, src] = 1.0
    return perm


def _side_block_rows(rows, layers, n_steps, granule):
    for r in range(granule, rows + 1, granule):
        if rows % r == 0 and layers * (rows // r) <= n_steps:
            return r
    raise ValueError("side cast does not fit in the host call's grid")


class _SideCast(NamedTuple):
    src: jax.Array
    in_block: tuple
    in_index: Callable
    out_shape: tuple
    out_block: tuple
    out_index: Callable
    n_blocks: int
    body: Callable


def _side_cast_rows(w, n_steps):
    layers, rows, cols = w.shape
    block_rows = _side_block_rows(rows, layers, n_steps, BF16_SUBLANES)
    per = rows // block_rows

    def index(b):
        return (b // per, b % per, 0)

    def body(src_ref, dst_ref, b):
        dst_ref[...] = src_ref[...].astype(BF16)
    block = (1, block_rows, cols)
    return _SideCast(w, block, index, w.shape, block, index, layers * per, body)


def _side_cast_up(w_up, d_ff, d_ff_pad, n_steps):
    layers, d, _ = w_up.shape
    assert d_ff % LANES == 0 and d_ff_pad % LANES == 0
    block_rows = _side_block_rows(d, layers, n_steps, BF16_SUBLANES)
    per = d // block_rows

    def index(b):
        return (b // per, b % per, 0)

    def body(src_ref, dst_ref, b):
        for half in range(2):
            dst_ref[0, :, half * d_ff_pad:half * d_ff_pad + d_ff] = (
                src_ref[0, :, half * d_ff:(half + 1) * d_ff].astype(BF16))
            if d_ff_pad > d_ff:
                dst_ref[0, :, half * d_ff_pad + d_ff:(half + 1) * d_ff_pad] = (
                    jnp.zeros((block_rows, d_ff_pad - d_ff), BF16))
    return _SideCast(w_up, (1, block_rows, 2 * d_ff), index, (layers, d, 2 * d_ff_pad),
                     (1, block_rows, 2 * d_ff_pad), index, layers * per, body)


def _side_cast_down(w_down, d_ff_pad, n_steps):
    layers, d_ff, d = w_down.shape
    common = math.gcd(d_ff, d_ff_pad)
    rows = _side_block_rows(common, layers * (d_ff_pad // common), n_steps, BF16_SUBLANES)
    src_per, per = d_ff // rows, d_ff_pad // rows

    def in_index(b):
        return (b // per, jnp.minimum(b % per, src_per - 1), 0)

    def out_index(b):
        return (b // per, b % per, 0)

    def body(src_ref, dst_ref, b):
        dst_ref[...] = jnp.where(b % per < src_per, src_ref[...], 0.0).astype(BF16)
    return _SideCast(w_down, (1, rows, d), in_index, (layers, d_ff_pad, d), (1, rows, d), out_index,
                     layers * per, body)


def _qkv_kernel(*refs, layout, side_casts):
    n_side = len(side_casts)
    n_in = 6 if layout == "residue_major" else 5
    x_ref, g_ref, sc_ref, sh_ref, w_ref = refs[:5]
    perm_ref = refs[5] if layout == "residue_major" else None
    side_in = refs[n_in:n_in + n_side]
    o_ref = refs[n_in + n_side]
    side_out = refs[n_in + n_side + 1:n_in + 2 * n_side + 1]
    h_ref = refs[-1]
    tm = x_ref.shape[0]

    @pl.when(pl.program_id(1) == 0)
    def _():
        def body(k, carry):
            r = pl.multiple_of(k * NORM_ROWS, NORM_ROWS)
            h = _modulated_rms_norm(x_ref[pl.ds(r, NORM_ROWS), :], g_ref[...], sc_ref[...], sh_ref[...])
            h_ref[pl.ds(r, NORM_ROWS), :] = h.astype(BF16)
            return carry
        lax.fori_loop(0, tm // NORM_ROWS, body, 0, unroll=4)
        if layout == "residue_major":
            h_ref[...] = jnp.dot(perm_ref[...], h_ref[...], preferred_element_type=F32).astype(BF16)

    step = pl.program_id(0) * pl.num_programs(1) + pl.program_id(1)
    for (body, n_blocks), src_ref, dst_ref in zip(side_casts, side_in, side_out):
        body(src_ref, dst_ref, jnp.minimum(step, n_blocks - 1))

    res = jnp.dot(h_ref[...], w_ref[...], preferred_element_type=F32)
    if layout == "head_major":
        for cb in range(o_ref.shape[0]):
            o_ref[cb] = res[:, cb * HEAD_DIM:(cb + 1) * HEAD_DIM].astype(BF16)
    else:
        o_ref[...] = res.reshape(o_ref.shape).astype(BF16)


def _qkv_projection(x, g, sc, sh, w, *, layout, side_casts=()):
    s, d = x.shape
    n = w.shape[1]
    tm, tn = QKV_TM, QKV_TN
    vec = pl.BlockSpec((1, d), lambda i, j: (0, 0))
    in_specs = [pl.BlockSpec((tm, d), lambda i, j: (i, 0)), vec, vec, vec,
                pl.BlockSpec((d, tn), lambda i, j: (0, j))]
    args = [x, g, sc, sh, w]
    vmem = 2 * tm * d * 4 + tm * d * 2 + 2 * d * tn * 2 + 2 * tm * tn * 2 + tm * tn * 4
    if layout == "head_major":
        out_shape = jax.ShapeDtypeStruct((n // HEAD_DIM, s, HEAD_DIM), BF16)
        out_spec = pl.BlockSpec((tn // HEAD_DIM, tm, HEAD_DIM), lambda i, j: (j, i, 0))
    else:
        assert layout == "residue_major"
        out_shape = jax.ShapeDtypeStruct((N_RESIDUES, s // N_RESIDUES, n), BF16)
        out_spec = pl.BlockSpec((N_RESIDUES, tm // N_RESIDUES, tn), lambda i, j: (0, i, j))
        in_specs.append(pl.BlockSpec((tm, tm), lambda i, j: (0, 0)))
        args.append(jnp.asarray(_residue_major_permutation(tm), BF16))
        vmem += 2 * tm * tm * 2 + tm * d * (4 + 2)
    nj = n // tn
    n_steps = (s // tm) * nj
    out_specs, out_shapes = [out_spec], [out_shape]
    for cast in side_casts:
        assert cast.n_blocks <= n_steps

        def block_of(i, j, cast=cast):
            return jnp.minimum(i * nj + j, cast.n_blocks - 1)
        in_specs.append(pl.BlockSpec(cast.in_block, lambda i, j, c=cast, b=block_of: c.in_index(b(i, j))))
        out_specs.append(pl.BlockSpec(cast.out_block, lambda i, j, c=cast, b=block_of: c.out_index(b(i, j))))
        out_shapes.append(jax.ShapeDtypeStruct(cast.out_shape, BF16))
        args.append(cast.src)
        vmem += 2 * (int(np.prod(cast.in_block)) * 4 + int(np.prod(cast.out_block)) * 2)
    outs = pl.pallas_call(
        functools.partial(_qkv_kernel, layout=layout,
                          side_casts=tuple((c.body, c.n_blocks) for c in side_casts)),
        name="qkv_projection",
        grid=(s // tm, nj),
        in_specs=in_specs,
        out_specs=out_specs,
        out_shape=out_shapes,
        scratch_shapes=[pltpu.VMEM((tm, d), BF16)],
        compiler_params=_params(("arbitrary", "arbitrary"), vmem),
    )(*args)
    return outs[0] if not side_casts else outs


def _alibi_slopes(n):
    return [float(np.float32(2.0 ** (-8.0 * (i + 1) / n))) for i in range(n)]


def _dilated_kernel(q_ref, kc_ref, vc_ref, bias_ref, o_ref, lse_ref, kp_ref, vp_ref, *, n_heads):
    slots, rows_per, _ = q_ref.shape
    rows = slots * rows_per
    first_block = (pl.program_id(1) == 0).astype(jnp.int32)

    @pl.when(pl.program_id(1) == 0)
    def _():
        kp_ref[...] = jnp.zeros_like(kp_ref)
        vp_ref[...] = jnp.zeros_like(vp_ref)
    lane = lax.broadcasted_iota(jnp.int32, (rows, LANES), 1)
    to_log2 = HEAD_DIM ** -0.5 * LOG2_E
    contract_last = (((1,), (1,)), ((), ()))

    def head(ref, h):
        return ref[:, :, h * HEAD_DIM:(h + 1) * HEAD_DIM].reshape(rows, HEAD_DIM)

    scores = []
    for h in range(n_heads):
        keys = jnp.concatenate([head(kp_ref, h), head(kc_ref, h)], axis=0)
        scores.append(lax.dot_general(head(q_ref, h), keys, contract_last, preferred_element_type=F32))

    max_tile = jnp.zeros((rows, LANES), F32)
    den_tile = jnp.ones((rows, LANES), F32)
    for h in range(n_heads):
        s = scores[h] * to_log2 + bias_ref[first_block, h]
        m = jnp.max(s, axis=-1, keepdims=True)
        p = jnp.exp2(s - m)
        denom = jnp.sum(p, axis=-1, keepdims=True)
        values = jnp.concatenate([head(vp_ref, h), head(vc_ref, h)], axis=0)
        acc = jnp.dot(p.astype(BF16), values, preferred_element_type=F32)
        o_ref[:, :, h * HEAD_DIM:(h + 1) * HEAD_DIM] = (
            (acc / denom).reshape(slots, rows_per, HEAD_DIM).astype(o_ref.dtype))
        max_tile = jnp.where(lane == h, m, max_tile)
        den_tile = jnp.where(lane == h, denom, den_tile)
    lse_ref[...] = ((max_tile + jnp.log2(den_tile)) * LN_2).reshape(lse_ref.shape)
    kp_ref[...] = kc_ref[...]
    vp_ref[...] = vc_ref[...]


def _alibi_bias_tables(dilation, span, slots, rows_per, n_heads):
    rows = slots * rows_per
    rho = np.arange(rows)
    pos = N_RESIDUES * (rho % rows_per) + _residue_of_slot(rho // rows_per)
    m = pos // dilation
    j_cur = m[:, None] - m[None, :]
    j = np.concatenate([j_cur + rows, j_cur], axis=1)
    valid = (j >= 0) & (j <= span)
    valid = jnp.asarray(np.stack([valid, valid & (np.arange(2 * rows) >= rows)[None, :]]))
    dist = jnp.asarray((j * dilation).astype(np.float32))
    slopes = jnp.asarray(_alibi_slopes(n_heads), F32)
    bias = -(slopes[:, None, None] * dist[None]) * LOG2_E
    return jnp.where(valid[:, None], bias[None], NEG_BIG)


def _dilated_group_attention(qkv, group, *, n_heads):
    window, dilation = DILATED_CONFIGS[group]
    n_res, s_per, _ = qkv.shape
    dm = n_heads * HEAD_DIM
    span = window // dilation
    assert n_res == N_RESIDUES and N_RESIDUES % dilation == 0 and n_heads <= LANES
    assert window % dilation == 0 and span <= ATTN_SPAN_MAX
    slots = N_RESIDUES // dilation
    rows_per = max(ATTN_SPAN_MAX // slots, BF16_SUBLANES)
    rows = slots * rows_per
    assert s_per % rows_per == 0 and rows >= span
    bias = _alibi_bias_tables(dilation, span, slots, rows_per, n_heads)
    base = group * 3

    def spec(which):
        return pl.BlockSpec((slots, rows_per, dm), lambda p, b: (p, b, base + which))

    return pl.pallas_call(
        functools.partial(_dilated_kernel, n_heads=n_heads),
        name=f"dilated_attention_g{group}",
        grid=(N_RESIDUES // slots, s_per // rows_per),
        in_specs=[spec(0), spec(1), spec(2),
                  pl.BlockSpec((2, n_heads, rows, 2 * rows), lambda p, b: (0, 0, 0, 0),
                               pipeline_mode=pl.Buffered(1))],
        out_specs=[pl.BlockSpec((slots, rows_per, dm), lambda p, b: (p, b, 0)),
                   pl.BlockSpec((slots, rows_per, LANES), lambda p, b: (p, b, 0))],
        out_shape=[jax.ShapeDtypeStruct((N_RESIDUES, s_per, dm), BF16),
                   jax.ShapeDtypeStruct((N_RESIDUES, s_per, LANES), F32)],
        scratch_shapes=[pltpu.VMEM((slots, rows_per, dm), BF16), pltpu.VMEM((slots, rows_per, dm), BF16)],
        compiler_params=_params(("arbitrary", "arbitrary"),
                                (2 * 4 + 2) * rows * dm * 2 + 2 * rows * LANES * 4
                                + (2 + 4 * n_heads) * rows * 2 * rows * 4),
    )(qkv, qkv, qkv, bias)


SB_TQ = 2048
SB_SUB = 128
SB_BLK = 256
SB_NEAR = 512
SB_GROUP = 4


def _sb_kernel(q_ref, k_ref, v_ref, tri2_ref, o_ref, acc_ref, c_ref):
    tq = q_ref.shape[1]
    sub, blk, near = SB_SUB, SB_BLK, SB_NEAR
    n_sub = tq // sub
    i = pl.program_id(1)
    to_log2 = HEAD_DIM ** -0.5 * LOG2_E
    contract_last = (((1,), (1,)), ((), ()))

    def masked_scores(r, kstart, col, limit):
        q = q_ref[0, r * sub:(r + 1) * sub, :]
        k = k_ref[0, pl.ds(kstart, col.shape[1]), :]
        y = lax.dot_general(q, k, contract_last, preferred_element_type=F32) * to_log2
        return jnp.where(col < limit, y, NEG_BIG)

    def log_terms(y):
        neg_abs = pltpu.bitcast(pltpu.bitcast(y, jnp.uint32) | jnp.uint32(0x80000000), F32)
        log_beta = jnp.minimum(y, 0.0) - jnp.log2(1.0 + jnp.exp2(neg_abs))
        log_keep = log_beta - y
        return log_beta, log_keep

    def staged_log_terms(ys):
        n_blk = ys[0].shape[1] // blk
        stage = []
        for g in range(0, len(ys), SB_GROUP):
            terms, lhs = [], []
            for y in ys[g:g + SB_GROUP]:
                log_beta, log_keep = log_terms(y)
                hi = log_keep.astype(BF16)
                lo = (log_keep - hi.astype(F32)).astype(BF16)
                lhs += [jnp.concatenate([hi[:, b * blk:(b + 1) * blk], lo[:, b * blk:(b + 1) * blk]], axis=1)
                        for b in range(n_blk)]
                terms.append((log_beta, log_keep))
            local = jnp.dot(jnp.concatenate(lhs, axis=0), tri2_ref[...], preferred_element_type=F32)
            rows = n_blk * sub
            stage += [(lb, lk, local[k * rows:(k + 1) * rows, :]) for k, (lb, lk) in enumerate(terms)]
        return stage

    n_blk = near // blk
    col_minus_row = (lax.broadcasted_iota(jnp.int32, (sub, near), 1)
                     - lax.broadcasted_iota(jnp.int32, (sub, near), 0))
    kstarts, ys = [], []
    for r in range(n_sub):
        q_start = i * tq + r * sub
        kstart = pl.multiple_of(jnp.maximum(q_start + sub - near, 0), sub)
        kstarts.append(kstart)
        ys.append(masked_scores(r, kstart, col_minus_row, q_start - kstart))
    stage = staged_log_terms(ys)
    for r in range(n_sub):
        log_beta, log_keep, local = stage[r]
        newer = jnp.zeros((sub, 1), F32)
        after = [None] * n_blk
        for b in reversed(range(n_blk)):
            loc = local[b * sub:(b + 1) * sub, :]
            after[b] = loc + newer if b < n_blk - 1 else loc
            newer = newer + (loc[:, 0:1] + log_keep[:, b * blk:b * blk + 1])
        a = jnp.exp2(log_beta + jnp.concatenate(after, axis=1))
        v = v_ref[0, pl.ds(kstarts[r], near), :]
        acc_ref[r * sub:(r + 1) * sub, :] = jnp.dot(a.astype(BF16), v, preferred_element_type=F32)
        c_ref[r * sub:(r + 1) * sub, :] = newer

    def cond(carry):
        m, c_max = carry
        return (kstarts[n_sub - 1] - m * blk > 0) & (c_max > -F32_EXP2_UNDERFLOW)

    def body(carry):
        m, _ = carry
        col = lax.broadcasted_iota(jnp.int32, (sub, blk), 1)
        wins, ys = [], []
        for r in range(n_sub):
            kend = kstarts[r] - m * blk
            wstart = pl.multiple_of(jnp.maximum(kend - blk, 0), sub)
            wins.append(wstart)
            ys.append(masked_scores(r, wstart, col, kend - wstart))
        stage = staged_log_terms(ys)
        c_old = c_ref[...]
        outs, c_news = [], []
        for r in range(n_sub):
            log_beta, log_keep, local = stage[r]
            after = local + c_old[r * sub:(r + 1) * sub, :]
            a = jnp.exp2(log_beta + after)
            v = v_ref[0, pl.ds(wins[r], blk), :]
            outs.append(jnp.dot(a.astype(BF16), v, preferred_element_type=F32))
            c_news.append(after[:, 0:1] + log_keep[:, 0:1])
        c_new = jnp.concatenate(c_news, axis=0)
        acc_ref[...] += jnp.concatenate(outs, axis=0)
        c_ref[...] = c_new
        return m + 1, jnp.max(c_new)

    lax.while_loop(cond, body, (0, jnp.max(c_ref[...])))
    o_ref[...] = acc_ref[...].astype(o_ref.dtype)


def _stick_breaking_attention(qkv, *, n_heads):
    _, s, _ = qkv.shape
    tq, sub, blk, near = SB_TQ, SB_SUB, SB_BLK, SB_NEAR
    assert s % tq == 0 and tq % sub == 0 and near % blk == 0 and blk % sub == 0 and s >= near
    tri = np.tril(np.ones((blk, blk), np.float32), k=-1)
    tri2 = jnp.asarray(np.concatenate([tri, tri], axis=0), BF16)
    vmem = (2 * 2 * s * HEAD_DIM * 2 + 2 * 2 * blk * blk * 2 + 4 * tq * HEAD_DIM * 2
            + tq * LANES * 4 * 2 + 10 * tq * near * 4)
    return pl.pallas_call(
        _sb_kernel,
        name="stick_breaking_attention",
        grid=(n_heads, s // tq),
        in_specs=[pl.BlockSpec((1, tq, HEAD_DIM), lambda h, i: (h, i, 0)),
                  pl.BlockSpec((1, s, HEAD_DIM), lambda h, i: (n_heads + h, 0, 0)),
                  pl.BlockSpec((1, s, HEAD_DIM), lambda h, i: (2 * n_heads + h, 0, 0)),
                  pl.BlockSpec((2 * blk, blk), lambda h, i: (0, 0))],
        out_specs=pl.BlockSpec((tq, HEAD_DIM), lambda h, i: (i, h)),
        out_shape=jax.ShapeDtypeStruct((s, n_heads * HEAD_DIM), BF16),
        scratch_shapes=[pltpu.VMEM((tq, HEAD_DIM), F32), pltpu.VMEM((tq, 1), F32)],
        compiler_params=_params(("arbitrary", "arbitrary"), vmem),
    )(qkv, qkv, qkv, tri2)


OUT_TM = 512
OUT_HEAD_GROUPS = 4


def _out_kernel(*refs, n_groups, n_heads):
    o_refs = refs[:n_groups]
    lse_refs = refs[n_groups:2 * n_groups] if n_groups > 1 else ()
    rest = refs[len(o_refs) + len(lse_refs):]
    w_ref, x_ref, gate_ref, g_ref, sc_ref, sh_ref = rest[:6]
    if n_groups == 1:
        xo_ref, h_ref = rest[6:]
    else:
        unperm_ref, xo_ref, h_ref, merged_ref = rest[6:]
    tm = x_ref.shape[0]

    def finish(row0, y):
        for k in range(y.shape[0] // NORM_ROWS):
            rows = slice(row0 + k * NORM_ROWS, row0 + (k + 1) * NORM_ROWS)
            xn = x_ref[rows, :] + gate_ref[...] * y[k * NORM_ROWS:(k + 1) * NORM_ROWS]
            xo_ref[rows, :] = xn
            h_ref[rows, :] = _modulated_rms_norm(xn, g_ref[...], sc_ref[...], sh_ref[...]).astype(BF16)

    if n_groups == 1:
        half = tm // 2
        ys = [jnp.dot(o_refs[0][r * half:(r + 1) * half, :], w_ref[...], preferred_element_type=F32)
              for r in range(2)]
        for r in range(2):
            finish(r * half, ys[r])
    else:
        rows_per = tm // N_RESIDUES
        wts = []
        for k in range(N_RESIDUES):
            lses = [lr[k] for lr in lse_refs]
            m = functools.reduce(jnp.maximum, lses)
            es = [jnp.exp(l - m) for l in lses]
            inv = 1.0 / functools.reduce(jnp.add, es)
            wts.append([e * inv for e in es[1:]])
        heads_per = n_heads // OUT_HEAD_GROUPS
        y = None
        for c in range(OUT_HEAD_GROUPS):
            for k in range(N_RESIDUES):
                rows = slice(k * rows_per, (k + 1) * rows_per)
                for h in range(c * heads_per, (c + 1) * heads_per):
                    cols = slice(h * HEAD_DIM, (h + 1) * HEAD_DIM)
                    base = o_refs[0][k, :, cols].astype(F32)
                    acc = base
                    for g in range(1, n_groups):
                        acc = acc + wts[k][g - 1][:, h:h + 1] * (o_refs[g][k, :, cols].astype(F32) - base)
                    merged_ref[rows, cols] = acc.astype(BF16)
            gcols = slice(c * heads_per * HEAD_DIM, (c + 1) * heads_per * HEAD_DIM)
            mixed = jnp.dot(unperm_ref[...], merged_ref[:, gcols], preferred_element_type=F32).astype(BF16)
            part = jnp.dot(mixed, w_ref[gcols, :], preferred_element_type=F32)
            y = part if y is None else y + part
        finish(0, y)


def _out_projection(outs, lses, w, x, gate, g, sc, sh, *, n_heads):
    s, d = x.shape
    dm = w.shape[0]
    n_groups = len(outs)
    tm = OUT_TM
    row_d = pl.BlockSpec((tm, d), lambda i: (i, 0))
    vec = pl.BlockSpec((1, d), lambda i: (0, 0))
    scratch = []
    if n_groups == 1:
        in_specs = [pl.BlockSpec((tm, dm), lambda i: (i, 0))]
        args = list(outs)
    else:
        rows_per = tm // N_RESIDUES
        in_specs = ([pl.BlockSpec((N_RESIDUES, rows_per, dm), lambda i: (0, i, 0))] * n_groups
                    + [pl.BlockSpec((N_RESIDUES, rows_per, LANES), lambda i: (0, i, 0))] * n_groups)
        args = list(outs) + list(lses)
        scratch = [pltpu.VMEM((tm, dm), BF16)]
    in_specs += [pl.BlockSpec((dm, d), lambda i: (0, 0)), row_d, vec, vec, vec, vec]
    args += [w, x, gate, g, sc, sh]
    if n_groups > 1:
        in_specs.append(pl.BlockSpec((tm, tm), lambda i: (0, 0)))
        args.append(jnp.asarray(_residue_major_permutation(tm).T, BF16))
    vmem = (2 * n_groups * tm * dm * 2 + 2 * dm * d * 2 + 2 * tm * d * 4 + 2 * tm * d * 4
            + 2 * tm * d * 2 + tm * dm * 2 + tm * d * 4)
    return pl.pallas_call(
        functools.partial(_out_kernel, n_groups=n_groups, n_heads=n_heads),
        name="mixer_out_projection",
        grid=(s // tm,),
        in_specs=in_specs,
        out_specs=[row_d, row_d],
        out_shape=[jax.ShapeDtypeStruct((s, d), F32), jax.ShapeDtypeStruct((s, d), BF16)],
        scratch_shapes=scratch,
        compiler_params=_params(("arbitrary",), vmem),
    )(*args)


FFN_TM = 1024
FFN_TF = 512
FFN_ROW_SPLIT = 2


def _causal_conv(u_ref, row0, rows, cw_ref, cb_ref):
    def at(shift):
        return u_ref[SUBLANES + row0 - shift:SUBLANES + row0 - shift + rows, :]

    y = cb_ref[...] + cw_ref[0:1, :] * at(2)
    y = y + cw_ref[1:2, :] * at(1)
    return y + cw_ref[2:3, :] * at(0)


def _ffn_kernel(h_ref, wa_ref, wg_ref, cwa_ref, cwg_ref, cba_ref, cbg_ref, wd_ref, x_ref, gate_ref,
                fg_ref, o_ref, tail_a_ref, tail_g_ref, ua_ref, ug_ref, *, final_norm):
    i = pl.program_id(0)
    j = pl.program_id(1)
    tm = h_ref.shape[0]

    @pl.when(i == 0)
    def _():
        tail_a_ref[j] = jnp.zeros(tail_a_ref.shape[1:], F32)
        tail_g_ref[j] = jnp.zeros(tail_g_ref.shape[1:], F32)

    @pl.when(j == 0)
    def _():
        o_ref[...] = jnp.zeros_like(o_ref)

    ua_ref[0:SUBLANES, :] = tail_a_ref[j]
    ug_ref[0:SUBLANES, :] = tail_g_ref[j]

    sub = tm // FFN_ROW_SPLIT
    for r in range(FFN_ROW_SPLIT):
        h = h_ref[r * sub:(r + 1) * sub, :]
        rows = slice(SUBLANES + r * sub, SUBLANES + (r + 1) * sub)
        ua_ref[rows, :] = jnp.dot(h, wa_ref[...], preferred_element_type=F32)
        ug_ref[rows, :] = jnp.dot(h, wg_ref[...], preferred_element_type=F32)
    for r in range(FFN_ROW_SPLIT):
        ya = _causal_conv(ua_ref, r * sub, sub, cwa_ref, cba_ref)
        yg = _causal_conv(ug_ref, r * sub, sub, cwg_ref, cbg_ref)
        act = (yg * (1.0 / (1.0 + jnp.exp(-yg))) * ya).astype(BF16)
        o_ref[r * sub:(r + 1) * sub, :] += jnp.dot(act, wd_ref[...], preferred_element_type=F32)
    tail_a_ref[j] = ua_ref[tm:tm + SUBLANES, :]
    tail_g_ref[j] = ug_ref[tm:tm + SUBLANES, :]

    @pl.when(j == pl.num_programs(1) - 1)
    def _():
        def fin(k, carry):
            r = pl.multiple_of(k * NORM_ROWS, NORM_ROWS)
            rows = pl.ds(r, NORM_ROWS)
            xn = x_ref[rows, :] + gate_ref[...] * o_ref[rows, :]
            if final_norm:
                xn = (xn * lax.rsqrt(jnp.mean(xn * xn, axis=-1, keepdims=True) + EPS)) * fg_ref[...]
            o_ref[rows, :] = xn
            return carry
        lax.fori_loop(0, tm // NORM_ROWS, fin, 0, unroll=4)


def _pad_halves(a, d_ff, d_ff_pad):
    pad = [(0, 0)] * (a.ndim - 1) + [(0, d_ff_pad - d_ff)]
    return jnp.concatenate([jnp.pad(a[..., :d_ff], pad), jnp.pad(a[..., d_ff:], pad)], axis=-1)


def _conv_ffn(h, layer, w_up_p, conv_w, conv_b, w_down_p, x, gate, final_g, *, final_norm):
    s, d = x.shape
    d_ff = conv_w.shape[1] // 2
    tm, tf = FFN_TM, FFN_TF
    d_ff_pad = w_down_p.shape[1]
    nj = d_ff_pad // tf
    conv_w_p = _pad_halves(conv_w, d_ff, d_ff_pad)
    conv_b_p = _pad_halves(conv_b.reshape(1, -1), d_ff, d_ff_pad)
    vec = pl.BlockSpec((1, d), lambda i, j: (0, 0))
    vmem = (2 * tm * d * 2 + 2 * 2 * d * tf * 2 + 2 * tf * d * 2 + tm * d * 4 + 2 * tm * d * 4
            + 2 * nj * SUBLANES * tf * 4 + 8 * tm * tf * 4)
    return pl.pallas_call(
        functools.partial(_ffn_kernel, final_norm=final_norm),
        name="conv_ffn",
        grid=(s // tm, nj),
        in_specs=[pl.BlockSpec((tm, d), lambda i, j: (i, 0)),
                  pl.BlockSpec((None, d, tf), lambda i, j: (layer, 0, j)),
                  pl.BlockSpec((None, d, tf), lambda i, j: (layer, 0, nj + j)),
                  pl.BlockSpec((CONV_WIDTH, tf), lambda i, j: (0, j)),
                  pl.BlockSpec((CONV_WIDTH, tf), lambda i, j: (0, nj + j)),
                  pl.BlockSpec((1, tf), lambda i, j: (0, j)),
                  pl.BlockSpec((1, tf), lambda i, j: (0, nj + j)),
                  pl.BlockSpec((None, tf, d), lambda i, j: (layer, j, 0)),
                  pl.BlockSpec((tm, d), lambda i, j: (i, 0), pipeline_mode=pl.Buffered(1)),
                  vec, vec],
        out_specs=pl.BlockSpec((tm, d), lambda i, j: (i, 0)),
        out_shape=jax.ShapeDtypeStruct((s, d), F32),
        scratch_shapes=[pltpu.VMEM((nj, SUBLANES, tf), F32), pltpu.VMEM((nj, SUBLANES, tf), F32),
                        pltpu.VMEM((SUBLANES + tm, tf), F32), pltpu.VMEM((SUBLANES + tm, tf), F32)],
        compiler_params=_params(("arbitrary", "arbitrary"), vmem),
    )(h, w_up_p, w_up_p, conv_w_p, conv_w_p, conv_b_p, conv_b_p, w_down_p, x, gate, final_g.reshape(1, d))


def kernel(x, c, norm_mix_g, norm_ffn_g, ada_w, ada_b, w_in_a, w_out_a, w_in_b, w_out_b,
           w_up, conv_w, conv_b, w_down, final_g):
    batch, s, d = x.shape
    assert batch == 1, "the sequence is processed as one (S, D) slab"
    depth = ada_w.shape[0]
    n_heads = w_out_a.shape[1] // HEAD_DIM
    n_groups = len(DILATED_CONFIGS)

    xs = x.reshape(s, d)
    mod = _ada_modulation(c, ada_w, ada_b)
    d_ff = w_down.shape[1]
    d_ff_pad = pl.cdiv(d_ff, FFN_TF) * FFN_TF
    n_steps = (s // QKV_TM) * (w_in_a.shape[2] // QKV_TN)
    side_casts = [_side_cast_up(w_up, d_ff, d_ff_pad, n_steps), _side_cast_down(w_down, d_ff_pad, n_steps),
                  _side_cast_rows(w_in_b, n_steps), _side_cast_rows(w_out_a, n_steps),
                  _side_cast_rows(w_out_b, n_steps)]
    w_up_p = w_down_p = w_in_b16 = w_out_a16 = w_out_b16 = None
    for i in range(depth):
        sh1, sc1, g1, sh2, sc2, g2 = [mod[i, :, k * d:(k + 1) * d] for k in range(N_MOD)]
        norm_g = norm_mix_g[i].reshape(1, d)
        ffn_g = norm_ffn_g[i].reshape(1, d)
        if i % 2 == 0:
            w_in = w_in_a[i // 2].astype(BF16)
            if i == 0:
                qkv, w_up_p, w_down_p, w_in_b16, w_out_a16, w_out_b16 = _qkv_projection(
                    xs, norm_g, sc1, sh1, w_in, layout="residue_major", side_casts=side_casts)
            else:
                qkv = _qkv_projection(xs, norm_g, sc1, sh1, w_in, layout="residue_major")
            outs, lses = zip(*[_dilated_group_attention(qkv, g, n_heads=n_heads) for g in range(n_groups)])
            w_out = w_out_a16[i // 2]
        else:
            qkv = _qkv_projection(xs, norm_g, sc1, sh1, w_in_b16[i // 2], layout="head_major")
            outs, lses = [_stick_breaking_attention(qkv, n_heads=n_heads)], None
            w_out = w_out_b16[i // 2]
        xs, h2 = _out_projection(outs, lses, w_out, xs, g1, ffn_g, sc2, sh2, n_heads=n_heads)
        xs = _conv_ffn(h2, i, w_up_p, conv_w[i], conv_b[i], w_down_p, xs, g2, final_g,
                       final_norm=(i == depth - 1))
    return xs.reshape(batch, s, d)
```

```python
import functools
import math
from typing import Callable, NamedTuple

import numpy as np
import jax
import jax.numpy as jnp
from jax import lax
from jax.experimental import pallas as pl
from jax.experimental.pallas import tpu as pltpu

HEAD_DIM = 128
DILATED_CONFIGS = ((128, 1), (512, 4), (2048, 16))
ATTN_SPAN_MAX = 128
CONV_WIDTH = 3
EPS = 1e-6
N_MOD = 6
NEG_BIG = -1e30
N_RESIDUES = 16
MID_DILATION = 4

LANES = 128
SUBLANES = 8
BF16_SUBLANES = 16
V7X_VMEM_BYTES = 64 * 1024 * 1024
VMEM_HEADROOM_BYTES = 6 * 1024 * 1024

F32_EXP2_UNDERFLOW = 150.0
LOG2_E = 1.4426950408889634
LN_2 = 0.6931471805599453

F32 = jnp.float32
BF16 = jnp.bfloat16


def _params(semantics, vmem_estimate_bytes):
    limit = min(int(vmem_estimate_bytes) + VMEM_HEADROOM_BYTES, V7X_VMEM_BYTES - VMEM_HEADROOM_BYTES)
    return pltpu.CompilerParams(dimension_semantics=semantics, vmem_limit_bytes=limit)


def _modulated_rms_norm(x, g, sc, sh):
    y = x * lax.rsqrt(jnp.mean(x * x, axis=-1, keepdims=True) + EPS)
    return (y * g) * (1.0 + sc) + sh


ADA_TN = 1024
ADA_ROWS = 256


def _ada_kernel(c_ref, w_ref, b_ref, o_ref):
    d = w_ref.shape[1]
    tn = w_ref.shape[2]

    def body(k, acc):
        r = pl.multiple_of(k * ADA_ROWS, ADA_ROWS)
        prod = c_ref[pl.ds(r, ADA_ROWS), :] * w_ref[0, pl.ds(r, ADA_ROWS), :]
        return acc + prod.reshape(ADA_ROWS // SUBLANES, SUBLANES, tn).sum(axis=0)

    acc = lax.fori_loop(0, d // ADA_ROWS, body, jnp.zeros((SUBLANES, tn), F32))
    o_ref[0] = acc.sum(axis=0, keepdims=True) + b_ref[0]


def _ada_modulation(c, ada_w, ada_b):
    depth, d, n = ada_w.shape
    c_col = c.reshape(d, 1)
    return pl.pallas_call(
        _ada_kernel,
        name="ada_modulation",
        grid=(depth, n // ADA_TN),
        in_specs=[
            pl.BlockSpec((d, 1), lambda l, j: (0, 0)),
            pl.BlockSpec((1, d, ADA_TN), lambda l, j: (l, 0, j)),
            pl.BlockSpec((1, 1, ADA_TN), lambda l, j: (l, 0, j)),
        ],
        out_specs=pl.BlockSpec((1, 1, ADA_TN), lambda l, j: (l, 0, j)),
        out_shape=jax.ShapeDtypeStruct((depth, 1, n), F32),
        compiler_params=_params(("arbitrary", "arbitrary"),
                                2 * d * ADA_TN * 4 + d * LANES * 4),
    )(c_col, ada_w, ada_b.reshape(depth, 1, n))


QKV_TM = 1024
QKV_TN = 1024
NORM_ROWS = 64


def _residue_of_slot(slot):
    per = N_RESIDUES // MID_DILATION
    return MID_DILATION * (slot % per) + slot // per


def _residue_major_permutation(rows):
    rho = np.arange(rows)
    rows_per = rows // N_RESIDUES
    src = N_RESIDUES * (rho % rows_per) + _residue_of_slot(rho // rows_per)
    perm = np.zeros((rows, rows), np.float32)
    perm[rho, src] = 1.0
    return perm


def _side_block_rows(rows, layers, n_steps, granule):
    for r in range(granule, rows + 1, granule):
        if rows % r == 0 and layers * (rows // r) <= n_steps:
            return r
    raise ValueError("side cast does not fit in the host call's grid")


class _SideCast(NamedTuple):
    src: jax.Array
    in_block: tuple
    in_index: Callable
    out_shape: tuple
    out_block: tuple
    out_index: Callable
    n_blocks: int
    body: Callable


def _side_cast_rows(w, n_steps):
    layers, rows, cols = w.shape
    block_rows = _side_block_rows(rows, layers, n_steps, BF16_SUBLANES)
    per = rows // block_rows

    def index(b):
        return (b // per, b % per, 0)

    def body(src_ref, dst_ref, b):
        dst_ref[...] = src_ref[...].astype(BF16)
    block = (1, block_rows, cols)
    return _SideCast(w, block, index, w.shape, block, index, layers * per, body)


def _side_cast_up(w_up, d_ff, d_ff_pad, n_steps):
    layers, d, _ = w_up.shape
    assert d_ff % LANES == 0 and d_ff_pad % LANES == 0
    block_rows = _side_block_rows(d, layers, n_steps, BF16_SUBLANES)
    per = d // block_rows

    def index(b):
        return (b // per, b % per, 0)

    def body(src_ref, dst_ref, b):
        for half in range(2):
            dst_ref[0, :, half * d_ff_pad:half * d_ff_pad + d_ff] = (
                src_ref[0, :, half * d_ff:(half + 1) * d_ff].astype(BF16))
            if d_ff_pad > d_ff:
                dst_ref[0, :, half * d_ff_pad + d_ff:(half + 1) * d_ff_pad] = (
                    jnp.zeros((block_rows, d_ff_pad - d_ff), BF16))
    return _SideCast(w_up, (1, block_rows, 2 * d_ff), index, (layers, d, 2 * d_ff_pad),
                     (1, block_rows, 2 * d_ff_pad), index, layers * per, body)


def _side_cast_down(w_down, d_ff_pad, n_steps):
    layers, d_ff, d = w_down.shape
    common = math.gcd(d_ff, d_ff_pad)
    rows = _side_block_rows(common, layers * (d_ff_pad // common), n_steps, BF16_SUBLANES)
    src_per, per = d_ff // rows, d_ff_pad // rows

    def in_index(b):
        return (b // per, jnp.minimum(b % per, src_per - 1), 0)

    def out_index(b):
        return (b // per, b % per, 0)

    def body(src_ref, dst_ref, b):
        dst_ref[...] = jnp.where(b % per < src_per, src_ref[...], 0.0).astype(BF16)
    return _SideCast(w_down, (1, rows, d), in_index, (layers, d_ff_pad, d), (1, rows, d), out_index,
                     layers * per, body)


def _qkv_kernel(*refs, layout, side_casts):
    n_side = len(side_casts)
    n_in = 6 if layout == "residue_major" else 5
    x_ref, g_ref, sc_ref, sh_ref, w_ref = refs[:5]
    perm_ref = refs[5] if layout == "residue_major" else None
    side_in = refs[n_in:n_in + n_side]
    o_ref = refs[n_in + n_side]
    side_out = refs[n_in + n_side + 1:n_in + 2 * n_side + 1]
    h_ref = refs[-1]
    tm = x_ref.shape[0]

    @pl.when(pl.program_id(1) == 0)
    def _():
        def body(k, carry):
            r = pl.multiple_of(k * NORM_ROWS, NORM_ROWS)
            h = _modulated_rms_norm(x_ref[pl.ds(r, NORM_ROWS), :], g_ref[...], sc_ref[...], sh_ref[...])
            h_ref[pl.ds(r, NORM_ROWS), :] = h.astype(BF16)
            return carry
        lax.fori_loop(0, tm // NORM_ROWS, body, 0, unroll=4)
        if layout == "residue_major":
            h_ref[...] = jnp.dot(perm_ref[...], h_ref[...], preferred_element_type=F32).astype(BF16)

    step = pl.program_id(0) * pl.num_programs(1) + pl.program_id(1)
    for (body, n_blocks), src_ref, dst_ref in zip(side_casts, side_in, side_out):
        body(src_ref, dst_ref, jnp.minimum(step, n_blocks - 1))

    res = jnp.dot(h_ref[...], w_ref[...], preferred_element_type=F32)
    if layout == "head_major":
        for cb in range(o_ref.shape[0]):
            o_ref[cb] = res[:, cb * HEAD_DIM:(cb + 1) * HEAD_DIM].astype(BF16)
    else:
        o_ref[...] = res.reshape(o_ref.shape).astype(BF16)


def _qkv_projection(x, g, sc, sh, w, *, layout, side_casts=()):
    s, d = x.shape
    n = w.shape[1]
    tm, tn = QKV_TM, QKV_TN
    vec = pl.BlockSpec((1, d), lambda i, j: (0, 0))
    in_specs = [pl.BlockSpec((tm, d), lambda i, j: (i, 0)), vec, vec, vec,
                pl.BlockSpec((d, tn), lambda i, j: (0, j))]
    args = [x, g, sc, sh, w]
    vmem = 2 * tm * d * 4 + tm * d * 2 + 2 * d * tn * 2 + 2 * tm * tn * 2 + tm * tn * 4
    if layout == "head_major":
        out_shape = jax.ShapeDtypeStruct((n // HEAD_DIM, s, HEAD_DIM), BF16)
        out_spec = pl.BlockSpec((tn // HEAD_DIM, tm, HEAD_DIM), lambda i, j: (j, i, 0))
    else:
        assert layout == "residue_major"
        out_shape = jax.ShapeDtypeStruct((N_RESIDUES, s // N_RESIDUES, n), BF16)
        out_spec = pl.BlockSpec((N_RESIDUES, tm // N_RESIDUES, tn), lambda i, j: (0, i, j))
        in_specs.append(pl.BlockSpec((tm, tm), lambda i, j: (0, 0)))
        args.append(jnp.asarray(_residue_major_permutation(tm), BF16))
        vmem += 2 * tm * tm * 2 + tm * d * (4 + 2)
    nj = n // tn
    n_steps = (s // tm) * nj
    out_specs, out_shapes = [out_spec], [out_shape]
    for cast in side_casts:
        assert cast.n_blocks <= n_steps

        def block_of(i, j, cast=cast):
            return jnp.minimum(i * nj + j, cast.n_blocks - 1)
        in_specs.append(pl.BlockSpec(cast.in_block, lambda i, j, c=cast, b=block_of: c.in_index(b(i, j))))
        out_specs.append(pl.BlockSpec(cast.out_block, lambda i, j, c=cast, b=block_of: c.out_index(b(i, j))))
        out_shapes.append(jax.ShapeDtypeStruct(cast.out_shape, BF16))
        args.append(cast.src)
        vmem += 2 * (int(np.prod(cast.in_block)) * 4 + int(np.prod(cast.out_block)) * 2)
    outs = pl.pallas_call(
        functools.partial(_qkv_kernel, layout=layout,
                          side_casts=tuple((c.body, c.n_blocks) for c in side_casts)),
        name="qkv_projection",
        grid=(s // tm, nj),
        in_specs=in_specs,
        out_specs=out_specs,
        out_shape=out_shapes,
        scratch_shapes=[pltpu.VMEM((tm, d), BF16)],
        compiler_params=_params(("arbitrary", "arbitrary"), vmem),
    )(*args)
    return outs[0] if not side_casts else outs


def _alibi_slopes(n):
    return [float(np.float32(2.0 ** (-8.0 * (i + 1) / n))) for i in range(n)]


def _dilated_kernel(q_ref, kc_ref, vc_ref, bias_ref, o_ref, lse_ref, kp_ref, vp_ref, *, n_heads):
    slots, rows_per, _ = q_ref.shape
    rows = slots * rows_per
    first_block = (pl.program_id(1) == 0).astype(jnp.int32)

    @pl.when(pl.program_id(1) == 0)
    def _():
        kp_ref[...] = jnp.zeros_like(kp_ref)
        vp_ref[...] = jnp.zeros_like(vp_ref)
    lane = lax.broadcasted_iota(jnp.int32, (rows, LANES), 1)
    to_log2 = HEAD_DIM ** -0.5 * LOG2_E
    contract_last = (((1,), (1,)), ((), ()))

    def head(ref, h):
        return ref[:, :, h * HEAD_DIM:(h + 1) * HEAD_DIM].reshape(rows, HEAD_DIM)

    scores = []
    for h in range(n_heads):
        keys = jnp.concatenate([head(kp_ref, h), head(kc_ref, h)], axis=0)
        scores.append(lax.dot_general(head(q_ref, h), keys, contract_last, preferred_element_type=F32))

    max_tile = jnp.zeros((rows, LANES), F32)
    den_tile = jnp.ones((rows, LANES), F32)
    for h in range(n_heads):
        s = scores[h] * to_log2 + bias_ref[first_block, h]
        m = jnp.max(s, axis=-1, keepdims=True)
        p = jnp.exp2(s - m)
        denom = jnp.sum(p, axis=-1, keepdims=True)
        values = jnp.concatenate([head(vp_ref, h), head(vc_ref, h)], axis=0)
        acc = jnp.dot(p.astype(BF16), values, preferred_element_type=F32)
        o_ref[:, :, h * HEAD_DIM:(h + 1) * HEAD_DIM] = (
            (acc / denom).reshape(slots, rows_per, HEAD_DIM).astype(o_ref.dtype))
        max_tile = jnp.where(lane == h, m, max_tile)
        den_tile = jnp.where(lane == h, denom, den_tile)
    lse_ref[...] = ((max_tile + jnp.log2(den_tile)) * LN_2).reshape(lse_ref.shape)
    kp_ref[...] = kc_ref[...]
    vp_ref[...] = vc_ref[...]


def _alibi_bias_tables(dilation, span, slots, rows_per, n_heads):
    rows = slots * rows_per
    rho = np.arange(rows)
    pos = N_RESIDUES * (rho % rows_per) + _residue_of_slot(rho // rows_per)
    m = pos // dilation
    j_cur = m[:, None] - m[None, :]
    j = np.concatenate([j_cur + rows, j_cur], axis=1)
    valid = (j >= 0) & (j <= span)
    valid = jnp.asarray(np.stack([valid, valid & (np.arange(2 * rows) >= rows)[None, :]]))
    dist = jnp.asarray((j * dilation).astype(np.float32))
    slopes = jnp.asarray(_alibi_slopes(n_heads), F32)
    bias = -(slopes[:, None, None] * dist[None]) * LOG2_E
    return jnp.where(valid[:, None], bias[None], NEG_BIG)


def _dilated_group_attention(qkv, group, *, n_heads):
    window, dilation = DILATED_CONFIGS[group]
    n_res, s_per, _ = qkv.shape
    dm = n_heads * HEAD_DIM
    span = window // dilation
    assert n_res == N_RESIDUES and N_RESIDUES % dilation == 0 and n_heads <= LANES
    assert window % dilation == 0 and span <= ATTN_SPAN_MAX
    slots = N_RESIDUES // dilation
    rows_per = max(ATTN_SPAN_MAX // slots, BF16_SUBLANES)
    rows = slots * rows_per
    assert s_per % rows_per == 0 and rows >= span
    bias = _alibi_bias_tables(dilation, span, slots, rows_per, n_heads)
    base = group * 3

    def spec(which):
        return pl.BlockSpec((slots, rows_per, dm), lambda p, b: (p, b, base + which))

    return pl.pallas_call(
        functools.partial(_dilated_kernel, n_heads=n_heads),
        name=f"dilated_attention_g{group}",
        grid=(N_RESIDUES // slots, s_per // rows_per),
        in_specs=[spec(0), spec(1), spec(2),
                  pl.BlockSpec((2, n_heads, rows, 2 * rows), lambda p, b: (0, 0, 0, 0),
                               pipeline_mode=pl.Buffered(1))],
        out_specs=[pl.BlockSpec((slots, rows_per, dm), lambda p, b: (p, b, 0)),
                   pl.BlockSpec((slots, rows_per, LANES), lambda p, b: (p, b, 0))],
        out_shape=[jax.ShapeDtypeStruct((N_RESIDUES, s_per, dm), BF16),
                   jax.ShapeDtypeStruct((N_RESIDUES, s_per, LANES), F32)],
        scratch_shapes=[pltpu.VMEM((slots, rows_per, dm), BF16), pltpu.VMEM((slots, rows_per, dm), BF16)],
        compiler_params=_params(("arbitrary", "arbitrary"),
                                (2 * 4 + 2) * rows * dm * 2 + 2 * rows * LANES * 4
                                + (2 + 4 * n_heads) * rows * 2 * rows * 4),
    )(qkv, qkv, qkv, bias)


SB_TQ = 1024
SB_SUB = 128
SB_BLK = 256
SB_NEAR = 512
SB_GROUP = 4


def _sb_kernel(q_ref, k_ref, v_ref, tri2_ref, o_ref, acc_ref, c_ref):
    tq = q_ref.shape[1]
    sub, blk, near = SB_SUB, SB_BLK, SB_NEAR
    n_sub = tq // sub
    i = pl.program_id(1)
    to_log2 = HEAD_DIM ** -0.5 * LOG2_E
    contract_last = (((1,), (1,)), ((), ()))

    def masked_scores(r, kstart, col, limit):
        q = q_ref[0, r * sub:(r + 1) * sub, :]
        k = k_ref[0, pl.ds(kstart, col.shape[1]), :]
        y = lax.dot_general(q, k, contract_last, preferred_element_type=F32) * to_log2
        return jnp.where(col < limit, y, NEG_BIG)

    def log_terms(y):
        neg_abs = pltpu.bitcast(pltpu.bitcast(y, jnp.uint32) | jnp.uint32(0x80000000), F32)
        log_beta = jnp.minimum(y, 0.0) - jnp.log2(1.0 + jnp.exp2(neg_abs))
        log_keep = log_beta - y
        return log_beta, log_keep

    def staged_log_terms(ys):
        n_blk = ys[0].shape[1] // blk
        stage = []
        for g in range(0, len(ys), SB_GROUP):
            terms, lhs = [], []
            for y in ys[g:g + SB_GROUP]:
                log_beta, log_keep = log_terms(y)
                hi = log_keep.astype(BF16)
                lo = (log_keep - hi.astype(F32)).astype(BF16)
                lhs += [jnp.concatenate([hi[:, b * blk:(b + 1) * blk], lo[:, b * blk:(b + 1) * blk]], axis=1)
                        for b in range(n_blk)]
                terms.append((log_beta, log_keep))
            local = jnp.dot(jnp.concatenate(lhs, axis=0), tri2_ref[...], preferred_element_type=F32)
            rows = n_blk * sub
            stage += [(lb, lk, local[k * rows:(k + 1) * rows, :]) for k, (lb, lk) in enumerate(terms)]
        return stage

    n_blk = near // blk
    col_minus_row = (lax.broadcasted_iota(jnp.int32, (sub, near), 1)
                     - lax.broadcasted_iota(jnp.int32, (sub, near), 0))
    kstarts, ys = [], []
    for r in range(n_sub):
        q_start = i * tq + r * sub
        kstart = pl.multiple_of(jnp.maximum(q_start + sub - near, 0), sub)
        kstarts.append(kstart)
        ys.append(masked_scores(r, kstart, col_minus_row, q_start - kstart))
    stage = staged_log_terms(ys)
    for r in range(n_sub):
        log_beta, log_keep, local = stage[r]
        newer = jnp.zeros((sub, 1), F32)
        after = [None] * n_blk
        for b in reversed(range(n_blk)):
            loc = local[b * sub:(b + 1) * sub, :]
            after[b] = loc + newer if b < n_blk - 1 else loc
            newer = newer + (loc[:, 0:1] + log_keep[:, b * blk:b * blk + 1])
        a = jnp.exp2(log_beta + jnp.concatenate(after, axis=1))
        v = v_ref[0, pl.ds(kstarts[r], near), :]
        acc_ref[r * sub:(r + 1) * sub, :] = jnp.dot(a.astype(BF16), v, preferred_element_type=F32)
        c_ref[r * sub:(r + 1) * sub, :] = newer

    def cond(carry):
        m, c_max = carry
        return (kstarts[n_sub - 1] - m * blk > 0) & (c_max > -F32_EXP2_UNDERFLOW)

    def body(carry):
        m, _ = carry
        col = lax.broadcasted_iota(jnp.int32, (sub, blk), 1)
        wins, ys = [], []
        for r in range(n_sub):
            kend = kstarts[r] - m * blk
            wstart = pl.multiple_of(jnp.maximum(kend - blk, 0), sub)
            wins.append(wstart)
            ys.append(masked_scores(r, wstart, col, kend - wstart))
        stage = staged_log_terms(ys)
        c_old = c_ref[...]
        outs, c_news = [], []
        for r in range(n_sub):
            log_beta, log_keep, local = stage[r]
            after = local + c_old[r * sub:(r + 1) * sub, :]
            a = jnp.exp2(log_beta + after)
            v = v_ref[0, pl.ds(wins[r], blk), :]
            outs.append(jnp.dot(a.astype(BF16), v, preferred_element_type=F32))
            c_news.append(after[:, 0:1] + log_keep[:, 0:1])
        c_new = jnp.concatenate(c_news, axis=0)
        acc_ref[...] += jnp.concatenate(outs, axis=0)
        c_ref[...] = c_new
        return m + 1, jnp.max(c_new)

    lax.while_loop(cond, body, (0, jnp.max(c_ref[...])))
    o_ref[...] = acc_ref[...].astype(o_ref.dtype)


def _stick_breaking_attention(qkv, *, n_heads):
    _, s, _ = qkv.shape
    tq, sub, blk, near = SB_TQ, SB_SUB, SB_BLK, SB_NEAR
    assert s % tq == 0 and tq % sub == 0 and near % blk == 0 and blk % sub == 0 and s >= near
    tri = np.tril(np.ones((blk, blk), np.float32), k=-1)
    tri2 = jnp.asarray(np.concatenate([tri, tri], axis=0), BF16)
    vmem = (2 * 2 * s * HEAD_DIM * 2 + 2 * 2 * blk * blk * 2 + 4 * tq * HEAD_DIM * 2
            + tq * LANES * 4 * 2 + 10 * tq * near * 4)
    return pl.pallas_call(
        _sb_kernel,
        name="stick_breaking_attention",
        grid=(n_heads, s // tq),
        in_specs=[pl.BlockSpec((1, tq, HEAD_DIM), lambda h, i: (h, i, 0)),
                  pl.BlockSpec((1, s, HEAD_DIM), lambda h, i: (n_heads + h, 0, 0)),
                  pl.BlockSpec((1, s, HEAD_DIM), lambda h, i: (2 * n_heads + h, 0, 0)),
                  pl.BlockSpec((2 * blk, blk), lambda h, i: (0, 0))],
        out_specs=pl.BlockSpec((tq, HEAD_DIM), lambda h, i: (i, h)),
        out_shape=jax.ShapeDtypeStruct((s, n_heads * HEAD_DIM), BF16),
        scratch_shapes=[pltpu.VMEM((tq, HEAD_DIM), F32), pltpu.VMEM((tq, 1), F32)],
        compiler_params=_params(("arbitrary", "arbitrary"), vmem),
    )(qkv, qkv, qkv, tri2)


OUT_TM = 512
OUT_HEAD_GROUPS = 4


def _out_kernel(*refs, n_groups, n_heads):
    o_refs = refs[:n_groups]
    lse_refs = refs[n_groups:2 * n_groups] if n_groups > 1 else ()
    rest = refs[len(o_refs) + len(lse_refs):]
    w_ref, x_ref, gate_ref, g_ref, sc_ref, sh_ref = rest[:6]
    if n_groups == 1:
        xo_ref, h_ref = rest[6:]
    else:
        unperm_ref, xo_ref, h_ref, merged_ref = rest[6:]
    tm = x_ref.shape[0]

    def finish(row0, y):
        for k in range(y.shape[0] // NORM_ROWS):
            rows = slice(row0 + k * NORM_ROWS, row0 + (k + 1) * NORM_ROWS)
            xn = x_ref[rows, :] + gate_ref[...] * y[k * NORM_ROWS:(k + 1) * NORM_ROWS]
            xo_ref[rows, :] = xn
            h_ref[rows, :] = _modulated_rms_norm(xn, g_ref[...], sc_ref[...], sh_ref[...]).astype(BF16)

    if n_groups == 1:
        half = tm // 2
        ys = [jnp.dot(o_refs[0][r * half:(r + 1) * half, :], w_ref[...], preferred_element_type=F32)
              for r in range(2)]
        for r in range(2):
            finish(r * half, ys[r])
    else:
        rows_per = tm // N_RESIDUES
        wts = []
        for k in range(N_RESIDUES):
            lses = [lr[k] for lr in lse_refs]
            m = functools.reduce(jnp.maximum, lses)
            es = [jnp.exp(l - m) for l in lses]
            inv = 1.0 / functools.reduce(jnp.add, es)
            wts.append([e * inv for e in es[1:]])
        heads_per = n_heads // OUT_HEAD_GROUPS
        y = None
        for c in range(OUT_HEAD_GROUPS):
            for k in range(N_RESIDUES):
                rows = slice(k * rows_per, (k + 1) * rows_per)
                for h in range(c * heads_per, (c + 1) * heads_per):
                    cols = slice(h * HEAD_DIM, (h + 1) * HEAD_DIM)
                    base = o_refs[0][k, :, cols].astype(F32)
                    acc = base
                    for g in range(1, n_groups):
                        acc = acc + wts[k][g - 1][:, h:h + 1] * (o_refs[g][k, :, cols].astype(F32) - base)
                    merged_ref[rows, cols] = acc.astype(BF16)
            gcols = slice(c * heads_per * HEAD_DIM, (c + 1) * heads_per * HEAD_DIM)
            mixed = jnp.dot(unperm_ref[...], merged_ref[:, gcols], preferred_element_type=F32).astype(BF16)
            part = jnp.dot(mixed, w_ref[gcols, :], preferred_element_type=F32)
            y = part if y is None else y + part
        finish(0, y)


def _out_projection(outs, lses, w, x, gate, g, sc, sh, *, n_heads):
    s, d = x.shape
    dm = w.shape[0]
    n_groups = len(outs)
    tm = OUT_TM
    row_d = pl.BlockSpec((tm, d), lambda i: (i, 0))
    vec = pl.BlockSpec((1, d), lambda i: (0, 0))
    scratch = []
    if n_groups == 1:
        in_specs = [pl.BlockSpec((tm, dm), lambda i: (i, 0))]
        args = list(outs)
    else:
        rows_per = tm // N_RESIDUES
        in_specs = ([pl.BlockSpec((N_RESIDUES, rows_per, dm), lambda i: (0, i, 0))] * n_groups
                    + [pl.BlockSpec((N_RESIDUES, rows_per, LANES), lambda i: (0, i, 0))] * n_groups)
        args = list(outs) + list(lses)
        scratch = [pltpu.VMEM((tm, dm), BF16)]
    in_specs += [pl.BlockSpec((dm, d), lambda i: (0, 0)), row_d, vec, vec, vec, vec]
    args += [w, x, gate, g, sc, sh]
    if n_groups > 1:
        in_specs.append(pl.BlockSpec((tm, tm), lambda i: (0, 0)))
        args.append(jnp.asarray(_residue_major_permutation(tm).T, BF16))
    vmem = (2 * n_groups * tm * dm * 2 + 2 * dm * d * 2 + 2 * tm * d * 4 + 2 * tm * d * 4
            + 2 * tm * d * 2 + tm * dm * 2 + tm * d * 4)
    return pl.pallas_call(
        functools.partial(_out_kernel, n_groups=n_groups, n_heads=n_heads),
        name="mixer_out_projection",
        grid=(s // tm,),
        in_specs=in_specs,
        out_specs=[row_d, row_d],
        out_shape=[jax.ShapeDtypeStruct((s, d), F32), jax.ShapeDtypeStruct((s, d), BF16)],
        scratch_shapes=scratch,
        compiler_params=_params(("arbitrary",), vmem),
    )(*args)


FFN_TM = 1024
FFN_TF = 512
FFN_ROW_SPLIT = 2


def _causal_conv(u_ref, row0, rows, cw_ref, cb_ref):
    def at(shift):
        return u_ref[SUBLANES + row0 - shift:SUBLANES + row0 - shift + rows, :]

    y = cb_ref[...] + cw_ref[0:1, :] * at(2)
    y = y + cw_ref[1:2, :] * at(1)
    return y + cw_ref[2:3, :] * at(0)


def _ffn_kernel(h_ref, wa_ref, wg_ref, cwa_ref, cwg_ref, cba_ref, cbg_ref, wd_ref, x_ref, gate_ref,
                fg_ref, o_ref, tail_a_ref, tail_g_ref, ua_ref, ug_ref, *, final_norm):
    i = pl.program_id(0)
    j = pl.program_id(1)
    tm = h_ref.shape[0]

    @pl.when(i == 0)
    def _():
        tail_a_ref[j] = jnp.zeros(tail_a_ref.shape[1:], F32)
        tail_g_ref[j] = jnp.zeros(tail_g_ref.shape[1:], F32)

    @pl.when(j == 0)
    def _():
        o_ref[...] = jnp.zeros_like(o_ref)

    ua_ref[0:SUBLANES, :] = tail_a_ref[j]
    ug_ref[0:SUBLANES, :] = tail_g_ref[j]

    sub = tm // FFN_ROW_SPLIT
    for r in range(FFN_ROW_SPLIT):
        h = h_ref[r * sub:(r + 1) * sub, :]
        rows = slice(SUBLANES + r * sub, SUBLANES + (r + 1) * sub)
        ua_ref[rows, :] = jnp.dot(h, wa_ref[...], preferred_element_type=F32)
        ug_ref[rows, :] = jnp.dot(h, wg_ref[...], preferred_element_type=F32)
    for r in range(FFN_ROW_SPLIT):
        ya = _causal_conv(ua_ref, r * sub, sub, cwa_ref, cba_ref)
        yg = _causal_conv(ug_ref, r * sub, sub, cwg_ref, cbg_ref)
        act = (yg * (1.0 / (1.0 + jnp.exp(-yg))) * ya).astype(BF16)
        o_ref[r * sub:(r + 1) * sub, :] += jnp.dot(act, wd_ref[...], preferred_element_type=F32)
    tail_a_ref[j] = ua_ref[tm:tm + SUBLANES, :]
    tail_g_ref[j] = ug_ref[tm:tm + SUBLANES, :]

    @pl.when(j == pl.num_programs(1) - 1)
    def _():
        def fin(k, carry):
            r = pl.multiple_of(k * NORM_ROWS, NORM_ROWS)
            rows = pl.ds(r, NORM_ROWS)
            xn = x_ref[rows, :] + gate_ref[...] * o_ref[rows, :]
            if final_norm:
                xn = (xn * lax.rsqrt(jnp.mean(xn * xn, axis=-1, keepdims=True) + EPS)) * fg_ref[...]
            o_ref[rows, :] = xn
            return carry
        lax.fori_loop(0, tm // NORM_ROWS, fin, 0, unroll=4)


def _pad_halves(a, d_ff, d_ff_pad):
    pad = [(0, 0)] * (a.ndim - 1) + [(0, d_ff_pad - d_ff)]
    return jnp.concatenate([jnp.pad(a[..., :d_ff], pad), jnp.pad(a[..., d_ff:], pad)], axis=-1)


def _conv_ffn(h, layer, w_up_p, conv_w, conv_b, w_down_p, x, gate, final_g, *, final_norm):
    s, d = x.shape
    d_ff = conv_w.shape[1] // 2
    tm, tf = FFN_TM, FFN_TF
    d_ff_pad = w_down_p.shape[1]
    nj = d_ff_pad // tf
    conv_w_p = _pad_halves(conv_w, d_ff, d_ff_pad)
    conv_b_p = _pad_halves(conv_b.reshape(1, -1), d_ff, d_ff_pad)
    vec = pl.BlockSpec((1, d), lambda i, j: (0, 0))
    vmem = (2 * tm * d * 2 + 2 * 2 * d * tf * 2 + 2 * tf * d * 2 + tm * d * 4 + 2 * tm * d * 4
            + 2 * nj * SUBLANES * tf * 4 + 8 * tm * tf * 4)
    return pl.pallas_call(
        functools.partial(_ffn_kernel, final_norm=final_norm),
        name="conv_ffn",
        grid=(s // tm, nj),
        in_specs=[pl.BlockSpec((tm, d), lambda i, j: (i, 0)),
                  pl.BlockSpec((None, d, tf), lambda i, j: (layer, 0, j)),
                  pl.BlockSpec((None, d, tf), lambda i, j: (layer, 0, nj + j)),
                  pl.BlockSpec((CONV_WIDTH, tf), lambda i, j: (0, j)),
                  pl.BlockSpec((CONV_WIDTH, tf), lambda i, j: (0, nj + j)),
                  pl.BlockSpec((1, tf), lambda i, j: (0, j)),
                  pl.BlockSpec((1, tf), lambda i, j: (0, nj + j)),
                  pl.BlockSpec((None, tf, d), lambda i, j: (layer, j, 0)),
                  pl.BlockSpec((tm, d), lambda i, j: (i, 0), pipeline_mode=pl.Buffered(1)),
                  vec, vec],
        out_specs=pl.BlockSpec((tm, d), lambda i, j: (i, 0)),
        out_shape=jax.ShapeDtypeStruct((s, d), F32),
        scratch_shapes=[pltpu.VMEM((nj, SUBLANES, tf), F32), pltpu.VMEM((nj, SUBLANES, tf), F32),
                        pltpu.VMEM((SUBLANES + tm, tf), F32), pltpu.VMEM((SUBLANES + tm, tf), F32)],
        compiler_params=_params(("arbitrary", "arbitrary"), vmem),
    )(h, w_up_p, w_up_p, conv_w_p, conv_w_p, conv_b_p, conv_b_p, w_down_p, x, gate, final_g.reshape(1, d))


def kernel(x, c, norm_mix_g, norm_ffn_g, ada_w, ada_b, w_in_a, w_out_a, w_in_b, w_out_b,
           w_up, conv_w, conv_b, w_down, final_g):
    batch, s, d = x.shape
    assert batch == 1, "the sequence is processed as one (S, D) slab"
    depth = ada_w.shape[0]
    n_heads = w_out_a.shape[1] // HEAD_DIM
    n_groups = len(DILATED_CONFIGS)

    xs = x.reshape(s, d)
    mod = _ada_modulation(c, ada_w, ada_b)
    d_ff = w_down.shape[1]
    d_ff_pad = pl.cdiv(d_ff, FFN_TF) * FFN_TF
    n_steps = (s // QKV_TM) * (w_in_a.shape[2] // QKV_TN)
    side_casts = [_side_cast_up(w_up, d_ff, d_ff_pad, n_steps), _side_cast_down(w_down, d_ff_pad, n_steps),
                  _side_cast_rows(w_in_b, n_steps), _side_cast_rows(w_out_a, n_steps),
                  _side_cast_rows(w_out_b, n_steps)]
    w_up_p = w_down_p = w_in_b16 = w_out_a16 = w_out_b16 = None
    for i in range(depth):
        sh1, sc1, g1, sh2, sc2, g2 = [mod[i, :, k * d:(k + 1) * d] for k in range(N_MOD)]
        norm_g = norm_mix_g[i].reshape(1, d)
        ffn_g = norm_ffn_g[i].reshape(1, d)
        if i % 2 == 0:
            w_in = w_in_a[i // 2].astype(BF16)
            if i == 0:
                qkv, w_up_p, w_down_p, w_in_b16, w_out_a16, w_out_b16 = _qkv_projection(
                    xs, norm_g, sc1, sh1, w_in, layout="residue_major", side_casts=side_casts)
            else:
                qkv = _qkv_projection(xs, norm_g, sc1, sh1, w_in, layout="residue_major")
            outs, lses = zip(*[_dilated_group_attention(qkv, g, n_heads=n_heads) for g in range(n_groups)])
            w_out = w_out_a16[i // 2]
        else:
            qkv = _qkv_projection(xs, norm_g, sc1, sh1, w_in_b16[i // 2], layout="head_major")
            outs, lses = [_stick_breaking_attention(qkv, n_heads=n_heads)], None
            w_out = w_out_b16[i // 2]
        xs, h2 = _out_projection(outs, lses, w_out, xs, g1, ffn_g, sc2, sh2, n_heads=n_heads)
        xs = _conv_ffn(h2, i, w_up_p, conv_w[i], conv_b[i], w_down_p, xs, g2, final_g,
                       final_norm=(i == depth - 1))
    return xs.reshape(batch, s, d)
```

```python
import functools
import math
from typing import Callable, NamedTuple

import numpy as np
import jax
import jax.numpy as jnp
from jax import lax
from jax.experimental import pallas as pl
from jax.experimental.pallas import tpu as pltpu

HEAD_DIM = 128
DILATED_CONFIGS = ((128, 1), (512, 4), (2048, 16))
ATTN_SPAN_MAX = 128
CONV_WIDTH = 3
EPS = 1e-6
N_MOD = 6
NEG_BIG = -1e30
N_RESIDUES = 16
MID_DILATION = 4

LANES = 128
SUBLANES = 8
BF16_SUBLANES = 16
V7X_VMEM_BYTES = 64 * 1024 * 1024
VMEM_HEADROOM_BYTES = 6 * 1024 * 1024

F32_EXP2_UNDERFLOW = 150.0
LOG2_E = 1.4426950408889634
LN_2 = 0.6931471805599453

F32 = jnp.float32
BF16 = jnp.bfloat16


def _params(semantics, vmem_estimate_bytes):
    limit = min(int(vmem_estimate_bytes) + VMEM_HEADROOM_BYTES, V7X_VMEM_BYTES - VMEM_HEADROOM_BYTES)
    return pltpu.CompilerParams(dimension_semantics=semantics, vmem_limit_bytes=limit)


def _modulated_rms_norm(x, g, sc, sh):
    y = x * lax.rsqrt(jnp.mean(x * x, axis=-1, keepdims=True) + EPS)
    return (y * g) * (1.0 + sc) + sh


ADA_TN = 1024
ADA_ROWS = 256


def _ada_kernel(c_ref, w_ref, b_ref, o_ref):
    d = w_ref.shape[1]
    tn = w_ref.shape[2]

    def body(k, acc):
        r = pl.multiple_of(k * ADA_ROWS, ADA_ROWS)
        prod = c_ref[pl.ds(r, ADA_ROWS), :] * w_ref[0, pl.ds(r, ADA_ROWS), :]
        return acc + prod.reshape(ADA_ROWS // SUBLANES, SUBLANES, tn).sum(axis=0)

    acc = lax.fori_loop(0, d // ADA_ROWS, body, jnp.zeros((SUBLANES, tn), F32))
    o_ref[0] = acc.sum(axis=0, keepdims=True) + b_ref[0]


def _ada_modulation(c, ada_w, ada_b):
    depth, d, n = ada_w.shape
    c_col = c.reshape(d, 1)
    return pl.pallas_call(
        _ada_kernel,
        name="ada_modulation",
        grid=(depth, n // ADA_TN),
        in_specs=[
            pl.BlockSpec((d, 1), lambda l, j: (0, 0)),
            pl.BlockSpec((1, d, ADA_TN), lambda l, j: (l, 0, j)),
            pl.BlockSpec((1, 1, ADA_TN), lambda l, j: (l, 0, j)),
        ],
        out_specs=pl.BlockSpec((1, 1, ADA_TN), lambda l, j: (l, 0, j)),
        out_shape=jax.ShapeDtypeStruct((depth, 1, n), F32),
        compiler_params=_params(("arbitrary", "arbitrary"),
                                2 * d * ADA_TN * 4 + d * LANES * 4),
    )(c_col, ada_w, ada_b.reshape(depth, 1, n))


QKV_TM = 1024
QKV_TN = 1024
NORM_ROWS = 64


def _residue_of_slot(slot):
    per = N_RESIDUES // MID_DILATION
    return MID_DILATION * (slot % per) + slot // per


def _residue_major_permutation(rows):
    rho = np.arange(rows)
    rows_per = rows // N_RESIDUES
    src = N_RESIDUES * (rho % rows_per) + _residue_of_slot(rho // rows_per)
    perm = np.zeros((rows, rows), np.float32)
    perm[rho, src] = 1.0
    return perm


def _side_block_rows(rows, layers, n_steps, granule):
    for r in range(granule, rows + 1, granule):
        if rows % r == 0 and layers * (rows // r) <= n_steps:
            return r
    raise ValueError("side cast does not fit in the host call's grid")


class _SideCast(NamedTuple):
    src: jax.Array
    in_block: tuple
    in_index: Callable
    out_shape: tuple
    out_block: tuple
    out_index: Callable
    n_blocks: int
    body: Callable


def _side_cast_rows(w, n_steps):
    layers, rows, cols = w.shape
    block_rows = _side_block_rows(rows, layers, n_steps, BF16_SUBLANES)
    per = rows // block_rows

    def index(b):
        return (b // per, b % per, 0)

    def body(src_ref, dst_ref, b):
        dst_ref[...] = src_ref[...].astype(BF16)
    block = (1, block_rows, cols)
    return _SideCast(w, block, index, w.shape, block, index, layers * per, body)


def _side_cast_up(w_up, d_ff, d_ff_pad, n_steps):
    layers, d, _ = w_up.shape
    assert d_ff % LANES == 0 and d_ff_pad % LANES == 0
    block_rows = _side_block_rows(d, layers, n_steps, BF16_SUBLANES)
    per = d // block_rows

    def index(b):
        return (b // per, b % per, 0)

    def body(src_ref, dst_ref, b):
        for half in range(2):
            dst_ref[0, :, half * d_ff_pad:half * d_ff_pad + d_ff] = (
                src_ref[0, :, half * d_ff:(half + 1) * d_ff].astype(BF16))
            if d_ff_pad > d_ff:
                dst_ref[0, :, half * d_ff_pad + d_ff:(half + 1) * d_ff_pad] = (
                    jnp.zeros((block_rows, d_ff_pad - d_ff), BF16))
    return _SideCast(w_up, (1, block_rows, 2 * d_ff), index, (layers, d, 2 * d_ff_pad),
                     (1, block_rows, 2 * d_ff_pad), index, layers * per, body)


def _side_cast_down(w_down, d_ff_pad, n_steps):
    layers, d_ff, d = w_down.shape
    common = math.gcd(d_ff, d_ff_pad)
    rows = _side_block_rows(common, layers * (d_ff_pad // common), n_steps, BF16_SUBLANES)
    src_per, per = d_ff // rows, d_ff_pad // rows

    def in_index(b):
        return (b // per, jnp.minimum(b % per, src_per - 1), 0)

    def out_index(b):
        return (b // per, b % per, 0)

    def body(src_ref, dst_ref, b):
        dst_ref[...] = jnp.where(b % per < src_per, src_ref[...], 0.0).astype(BF16)
    return _SideCast(w_down, (1, rows, d), in_index, (layers, d_ff_pad, d), (1, rows, d), out_index,
                     layers * per, body)


CAST_ROWS = 128


def _cast_kernel(w_ref, o_ref):
    o_ref[...] = w_ref[...].astype(BF16)


def _cast_bf16(w):
    layers, rows, cols = w.shape
    assert rows % CAST_ROWS == 0
    block = (1, CAST_ROWS, cols)
    return pl.pallas_call(
        _cast_kernel,
        name="cast_weight",
        grid=(layers, rows // CAST_ROWS),
        in_specs=[pl.BlockSpec(block, lambda l, i: (l, i, 0))],
        out_specs=pl.BlockSpec(block, lambda l, i: (l, i, 0)),
        out_shape=jax.ShapeDtypeStruct(w.shape, BF16),
        compiler_params=_params(("arbitrary", "arbitrary"), 2 * CAST_ROWS * cols * (4 + 2)),
    )(w)


def _qkv_kernel(*refs, layout, side_casts):
    n_side = len(side_casts)
    n_in = 6 if layout == "residue_major" else 5
    x_ref, g_ref, sc_ref, sh_ref, w_ref = refs[:5]
    perm_ref = refs[5] if layout == "residue_major" else None
    side_in = refs[n_in:n_in + n_side]
    o_ref = refs[n_in + n_side]
    side_out = refs[n_in + n_side + 1:n_in + 2 * n_side + 1]
    h_ref = refs[-1]
    tm = x_ref.shape[0]

    @pl.when(pl.program_id(1) == 0)
    def _():
        def body(k, carry):
            r = pl.multiple_of(k * NORM_ROWS, NORM_ROWS)
            h = _modulated_rms_norm(x_ref[pl.ds(r, NORM_ROWS), :], g_ref[...], sc_ref[...], sh_ref[...])
            h_ref[pl.ds(r, NORM_ROWS), :] = h.astype(BF16)
            return carry
        lax.fori_loop(0, tm // NORM_ROWS, body, 0, unroll=4)
        if layout == "residue_major":
            h_ref[...] = jnp.dot(perm_ref[...], h_ref[...], preferred_element_type=F32).astype(BF16)

    step = pl.program_id(0) * pl.num_programs(1) + pl.program_id(1)
    for (body, n_blocks), src_ref, dst_ref in zip(side_casts, side_in, side_out):
        body(src_ref, dst_ref, jnp.minimum(step, n_blocks - 1))

    res = jnp.dot(h_ref[...], w_ref[...], preferred_element_type=F32)
    if layout == "head_major":
        for cb in range(o_ref.shape[0]):
            o_ref[cb] = res[:, cb * HEAD_DIM:(cb + 1) * HEAD_DIM].astype(BF16)
    else:
        o_ref[...] = res.reshape(o_ref.shape).astype(BF16)


def _qkv_projection(x, g, sc, sh, w, *, layout, side_casts=()):
    s, d = x.shape
    n = w.shape[1]
    tm, tn = QKV_TM, QKV_TN
    vec = pl.BlockSpec((1, d), lambda i, j: (0, 0))
    in_specs = [pl.BlockSpec((tm, d), lambda i, j: (i, 0)), vec, vec, vec,
                pl.BlockSpec((d, tn), lambda i, j: (0, j))]
    args = [x, g, sc, sh, w]
    vmem = 2 * tm * d * 4 + tm * d * 2 + 2 * d * tn * 2 + 2 * tm * tn * 2 + tm * tn * 4
    if layout == "head_major":
        out_shape = jax.ShapeDtypeStruct((n // HEAD_DIM, s, HEAD_DIM), BF16)
        out_spec = pl.BlockSpec((tn // HEAD_DIM, tm, HEAD_DIM), lambda i, j: (j, i, 0))
    else:
        assert layout == "residue_major"
        out_shape = jax.ShapeDtypeStruct((N_RESIDUES, s // N_RESIDUES, n), BF16)
        out_spec = pl.BlockSpec((N_RESIDUES, tm // N_RESIDUES, tn), lambda i, j: (0, i, j))
        in_specs.append(pl.BlockSpec((tm, tm), lambda i, j: (0, 0)))
        args.append(jnp.asarray(_residue_major_permutation(tm), BF16))
        vmem += 2 * tm * tm * 2 + tm * d * (4 + 2)
    nj = n // tn
    n_steps = (s // tm) * nj
    out_specs, out_shapes = [out_spec], [out_shape]
    for cast in side_casts:
        assert cast.n_blocks <= n_steps

        def block_of(i, j, cast=cast):
            return jnp.minimum(i * nj + j, cast.n_blocks - 1)
        in_specs.append(pl.BlockSpec(cast.in_block, lambda i, j, c=cast, b=block_of: c.in_index(b(i, j))))
        out_specs.append(pl.BlockSpec(cast.out_block, lambda i, j, c=cast, b=block_of: c.out_index(b(i, j))))
        out_shapes.append(jax.ShapeDtypeStruct(cast.out_shape, BF16))
        args.append(cast.src)
        vmem += 2 * (int(np.prod(cast.in_block)) * 4 + int(np.prod(cast.out_block)) * 2)
    outs = pl.pallas_call(
        functools.partial(_qkv_kernel, layout=layout,
                          side_casts=tuple((c.body, c.n_blocks) for c in side_casts)),
        name="qkv_projection",
        grid=(s // tm, nj),
        in_specs=in_specs,
        out_specs=out_specs,
        out_shape=out_shapes,
        scratch_shapes=[pltpu.VMEM((tm, d), BF16)],
        compiler_params=_params(("arbitrary", "arbitrary"), vmem),
    )(*args)
    return outs[0] if not side_casts else outs


def _alibi_slopes(n):
    return [float(np.float32(2.0 ** (-8.0 * (i + 1) / n))) for i in range(n)]


def _dilated_kernel(q_ref, kc_ref, vc_ref, bias_ref, o_ref, lse_ref, kp_ref, vp_ref, *, n_heads):
    slots, rows_per, _ = q_ref.shape
    rows = slots * rows_per
    first_block = (pl.program_id(1) == 0).astype(jnp.int32)

    @pl.when(pl.program_id(1) == 0)
    def _():
        kp_ref[...] = jnp.zeros_like(kp_ref)
        vp_ref[...] = jnp.zeros_like(vp_ref)
    lane = lax.broadcasted_iota(jnp.int32, (rows, LANES), 1)
    to_log2 = HEAD_DIM ** -0.5 * LOG2_E
    contract_last = (((1,), (1,)), ((), ()))

    def head(ref, h):
        return ref[:, :, h * HEAD_DIM:(h + 1) * HEAD_DIM].reshape(rows, HEAD_DIM)

    scores = []
    for h in range(n_heads):
        keys = jnp.concatenate([head(kp_ref, h), head(kc_ref, h)], axis=0)
        scores.append(lax.dot_general(head(q_ref, h), keys, contract_last, preferred_element_type=F32))

    max_tile = jnp.zeros((rows, LANES), F32)
    den_tile = jnp.ones((rows, LANES), F32)
    for h in range(n_heads):
        s = scores[h] * to_log2 + bias_ref[first_block, h]
        m = jnp.max(s, axis=-1, keepdims=True)
        p = jnp.exp2(s - m)
        denom = jnp.sum(p, axis=-1, keepdims=True)
        values = jnp.concatenate([head(vp_ref, h), head(vc_ref, h)], axis=0)
        acc = jnp.dot(p.astype(BF16), values, preferred_element_type=F32)
        o_ref[:, :, h * HEAD_DIM:(h + 1) * HEAD_DIM] = (
            (acc / denom).reshape(slots, rows_per, HEAD_DIM).astype(o_ref.dtype))
        max_tile = jnp.where(lane == h, m, max_tile)
        den_tile = jnp.where(lane == h, denom, den_tile)
    lse_ref[...] = ((max_tile + jnp.log2(den_tile)) * LN_2).reshape(lse_ref.shape)
    kp_ref[...] = kc_ref[...]
    vp_ref[...] = vc_ref[...]


def _alibi_bias_tables(dilation, span, slots, rows_per, n_heads):
    rows = slots * rows_per
    rho = np.arange(rows)
    pos = N_RESIDUES * (rho % rows_per) + _residue_of_slot(rho // rows_per)
    m = pos // dilation
    j_cur = m[:, None] - m[None, :]
    j = np.concatenate([j_cur + rows, j_cur], axis=1)
    valid = (j >= 0) & (j <= span)
    valid = jnp.asarray(np.stack([valid, valid & (np.arange(2 * rows) >= rows)[None, :]]))
    dist = jnp.asarray((j * dilation).astype(np.float32))
    slopes = jnp.asarray(_alibi_slopes(n_heads), F32)
    bias = -(slopes[:, None, None] * dist[None]) * LOG2_E
    return jnp.where(valid[:, None], bias[None], NEG_BIG)


def _dilated_group_attention(qkv, group, *, n_heads):
    window, dilation = DILATED_CONFIGS[group]
    n_res, s_per, _ = qkv.shape
    dm = n_heads * HEAD_DIM
    span = window // dilation
    assert n_res == N_RESIDUES and N_RESIDUES % dilation == 0 and n_heads <= LANES
    assert window % dilation == 0 and span <= ATTN_SPAN_MAX
    slots = N_RESIDUES // dilation
    rows_per = max(ATTN_SPAN_MAX // slots, BF16_SUBLANES)
    rows = slots * rows_per
    assert s_per % rows_per == 0 and rows >= span
    bias = _alibi_bias_tables(dilation, span, slots, rows_per, n_heads)
    base = group * 3

    def spec(which):
        return pl.BlockSpec((slots, rows_per, dm), lambda p, b: (p, b, base + which))

    return pl.pallas_call(
        functools.partial(_dilated_kernel, n_heads=n_heads),
        name=f"dilated_attention_g{group}",
        grid=(N_RESIDUES // slots, s_per // rows_per),
        in_specs=[spec(0), spec(1), spec(2),
                  pl.BlockSpec((2, n_heads, rows, 2 * rows), lambda p, b: (0, 0, 0, 0),
                               pipeline_mode=pl.Buffered(1))],
        out_specs=[pl.BlockSpec((slots, rows_per, dm), lambda p, b: (p, b, 0)),
                   pl.BlockSpec((slots, rows_per, LANES), lambda p, b: (p, b, 0))],
        out_shape=[jax.ShapeDtypeStruct((N_RESIDUES, s_per, dm), BF16),
                   jax.ShapeDtypeStruct((N_RESIDUES, s_per, LANES), F32)],
        scratch_shapes=[pltpu.VMEM((slots, rows_per, dm), BF16), pltpu.VMEM((slots, rows_per, dm), BF16)],
        compiler_params=_params(("arbitrary", "arbitrary"),
                                (2 * 4 + 2) * rows * dm * 2 + 2 * rows * LANES * 4
                                + (2 + 4 * n_heads) * rows * 2 * rows * 4),
    )(qkv, qkv, qkv, bias)


SB_TQ = 1024
SB_SUB = 128
SB_BLK = 256
SB_NEAR = 512
SB_GROUP = 4


def _sb_kernel(q_ref, k_ref, v_ref, tri2_ref, o_ref, acc_ref, c_ref):
    tq = q_ref.shape[1]
    sub, blk, near = SB_SUB, SB_BLK, SB_NEAR
    n_sub = tq // sub
    i = pl.program_id(1)
    to_log2 = HEAD_DIM ** -0.5 * LOG2_E
    contract_last = (((1,), (1,)), ((), ()))

    def masked_scores(r, kstart, col, limit):
        q = q_ref[0, r * sub:(r + 1) * sub, :]
        k = k_ref[0, pl.ds(kstart, col.shape[1]), :]
        y = lax.dot_general(q, k, contract_last, preferred_element_type=F32) * to_log2
        return jnp.where(col < limit, y, NEG_BIG)

    def log_terms(y):
        log_beta = jnp.minimum(y, 0.0) - jnp.log2(1.0 + jnp.exp2(-jnp.abs(y)))
        log_keep = log_beta - y
        return log_beta, log_keep

    def staged_log_terms(ys):
        n_blk = ys[0].shape[1] // blk
        stage = []
        for g in range(0, len(ys), SB_GROUP):
            terms, lhs = [], []
            for y in ys[g:g + SB_GROUP]:
                log_beta, log_keep = log_terms(y)
                hi = log_keep.astype(BF16)
                lo = (log_keep - hi.astype(F32)).astype(BF16)
                lhs += [jnp.concatenate([hi[:, b * blk:(b + 1) * blk], lo[:, b * blk:(b + 1) * blk]], axis=1)
                        for b in range(n_blk)]
                terms.append((log_beta, log_keep))
            local = jnp.dot(jnp.concatenate(lhs, axis=0), tri2_ref[...], preferred_element_type=F32)
            rows = n_blk * sub
            stage += [(lb, lk, local[k * rows:(k + 1) * rows, :]) for k, (lb, lk) in enumerate(terms)]
        return stage

    n_blk = near // blk
    col_minus_row = (lax.broadcasted_iota(jnp.int32, (sub, near), 1)
                     - lax.broadcasted_iota(jnp.int32, (sub, near), 0))
    kstarts, ys = [], []
    for r in range(n_sub):
        q_start = i * tq + r * sub
        kstart = pl.multiple_of(jnp.maximum(q_start + sub - near, 0), sub)
        kstarts.append(kstart)
        ys.append(masked_scores(r, kstart, col_minus_row, q_start - kstart))
    stage = staged_log_terms(ys)
    for r in range(n_sub):
        log_beta, log_keep, local = stage[r]
        newer = jnp.zeros((sub, 1), F32)
        after = [None] * n_blk
        for b in reversed(range(n_blk)):
            loc = local[b * sub:(b + 1) * sub, :]
            after[b] = loc + newer if b < n_blk - 1 else loc
            newer = newer + (loc[:, 0:1] + log_keep[:, b * blk:b * blk + 1])
        a = jnp.exp2(log_beta + jnp.concatenate(after, axis=1))
        v = v_ref[0, pl.ds(kstarts[r], near), :]
        acc_ref[r * sub:(r + 1) * sub, :] = jnp.dot(a.astype(BF16), v, preferred_element_type=F32)
        c_ref[r * sub:(r + 1) * sub, :] = newer

    def cond(carry):
        m, c_max = carry
        return (kstarts[n_sub - 1] - m * blk > 0) & (c_max > -F32_EXP2_UNDERFLOW)

    def body(carry):
        m, _ = carry
        col = lax.broadcasted_iota(jnp.int32, (sub, blk), 1)
        wins, ys = [], []
        for r in range(n_sub):
            kend = kstarts[r] - m * blk
            wstart = pl.multiple_of(jnp.maximum(kend - blk, 0), sub)
            wins.append(wstart)
            ys.append(masked_scores(r, wstart, col, kend - wstart))
        stage = staged_log_terms(ys)
        c_old = c_ref[...]
        outs, c_news = [], []
        for r in range(n_sub):
            log_beta, log_keep, local = stage[r]
            after = local + c_old[r * sub:(r + 1) * sub, :]
            a = jnp.exp2(log_beta + after)
            v = v_ref[0, pl.ds(wins[r], blk), :]
            outs.append(jnp.dot(a.astype(BF16), v, preferred_element_type=F32))
            c_news.append(after[:, 0:1] + log_keep[:, 0:1])
        c_new = jnp.concatenate(c_news, axis=0)
        acc_ref[...] += jnp.concatenate(outs, axis=0)
        c_ref[...] = c_new
        return m + 1, jnp.max(c_new)

    lax.while_loop(cond, body, (0, jnp.max(c_ref[...])))
    o_ref[...] = acc_ref[...].astype(o_ref.dtype)


def _stick_breaking_attention(qkv, *, n_heads):
    _, s, _ = qkv.shape
    tq, sub, blk, near = SB_TQ, SB_SUB, SB_BLK, SB_NEAR
    assert s % tq == 0 and tq % sub == 0 and near % blk == 0 and blk % sub == 0 and s >= near
    tri = np.tril(np.ones((blk, blk), np.float32), k=-1)
    tri2 = jnp.asarray(np.concatenate([tri, tri], axis=0), BF16)
    vmem = (2 * 2 * s * HEAD_DIM * 2 + 2 * 2 * blk * blk * 2 + 4 * tq * HEAD_DIM * 2
            + tq * LANES * 4 * 2 + 10 * tq * near * 4)
    return pl.pallas_call(
        _sb_kernel,
        name="stick_breaking_attention",
        grid=(n_heads, s // tq),
        in_specs=[pl.BlockSpec((1, tq, HEAD_DIM), lambda h, i: (h, i, 0)),
                  pl.BlockSpec((1, s, HEAD_DIM), lambda h, i: (n_heads + h, 0, 0)),
                  pl.BlockSpec((1, s, HEAD_DIM), lambda h, i: (2 * n_heads + h, 0, 0)),
                  pl.BlockSpec((2 * blk, blk), lambda h, i: (0, 0))],
        out_specs=pl.BlockSpec((tq, HEAD_DIM), lambda h, i: (i, h)),
        out_shape=jax.ShapeDtypeStruct((s, n_heads * HEAD_DIM), BF16),
        scratch_shapes=[pltpu.VMEM((tq, HEAD_DIM), F32), pltpu.VMEM((tq, 1), F32)],
        compiler_params=_params(("arbitrary", "arbitrary"), vmem),
    )(qkv, qkv, qkv, tri2)


OUT_TM = 512
OUT_HEAD_GROUPS = 4


def _out_kernel(*refs, n_groups, n_heads):
    o_refs = refs[:n_groups]
    lse_refs = refs[n_groups:2 * n_groups] if n_groups > 1 else ()
    rest = refs[len(o_refs) + len(lse_refs):]
    w_ref, x_ref, gate_ref, g_ref, sc_ref, sh_ref = rest[:6]
    if n_groups == 1:
        xo_ref, h_ref = rest[6:]
    else:
        unperm_ref, xo_ref, h_ref, merged_ref = rest[6:]
    tm = x_ref.shape[0]

    def finish(row0, y):
        for k in range(y.shape[0] // NORM_ROWS):
            rows = slice(row0 + k * NORM_ROWS, row0 + (k + 1) * NORM_ROWS)
            xn = x_ref[rows, :] + gate_ref[...] * y[k * NORM_ROWS:(k + 1) * NORM_ROWS]
            xo_ref[rows, :] = xn
            h_ref[rows, :] = _modulated_rms_norm(xn, g_ref[...], sc_ref[...], sh_ref[...]).astype(BF16)

    if n_groups == 1:
        half = tm // 2
        ys = [jnp.dot(o_refs[0][r * half:(r + 1) * half, :], w_ref[...], preferred_element_type=F32)
              for r in range(2)]
        for r in range(2):
            finish(r * half, ys[r])
    else:
        rows_per = tm // N_RESIDUES
        wts = []
        for k in range(N_RESIDUES):
            lses = [lr[k] for lr in lse_refs]
            m = functools.reduce(jnp.maximum, lses)
            es = [jnp.exp(l - m) for l in lses]
            inv = 1.0 / functools.reduce(jnp.add, es)
            wts.append([e * inv for e in es[1:]])
        heads_per = n_heads // OUT_HEAD_GROUPS
        y = None
        for c in range(OUT_HEAD_GROUPS):
            for k in range(N_RESIDUES):
                rows = slice(k * rows_per, (k + 1) * rows_per)
                for h in range(c * heads_per, (c + 1) * heads_per):
                    cols = slice(h * HEAD_DIM, (h + 1) * HEAD_DIM)
                    base = o_refs[0][k, :, cols].astype(F32)
                    acc = base
                    for g in range(1, n_groups):
                        acc = acc + wts[k][g - 1][:, h:h + 1] * (o_refs[g][k, :, cols].astype(F32) - base)
                    merged_ref[rows, cols] = acc.astype(BF16)
            gcols = slice(c * heads_per * HEAD_DIM, (c + 1) * heads_per * HEAD_DIM)
            mixed = jnp.dot(unperm_ref[...], merged_ref[:, gcols], preferred_element_type=F32).astype(BF16)
            part = jnp.dot(mixed, w_ref[gcols, :], preferred_element_type=F32)
            y = part if y is None else y + part
        finish(0, y)


def _out_projection(outs, lses, w, x, gate, g, sc, sh, *, n_heads):
    s, d = x.shape
    dm = w.shape[0]
    n_groups = len(outs)
    tm = OUT_TM
    row_d = pl.BlockSpec((tm, d), lambda i: (i, 0))
    vec = pl.BlockSpec((1, d), lambda i: (0, 0))
    scratch = []
    if n_groups == 1:
        in_specs = [pl.BlockSpec((tm, dm), lambda i: (i, 0))]
        args = list(outs)
    else:
        rows_per = tm // N_RESIDUES
        in_specs = ([pl.BlockSpec((N_RESIDUES, rows_per, dm), lambda i: (0, i, 0))] * n_groups
                    + [pl.BlockSpec((N_RESIDUES, rows_per, LANES), lambda i: (0, i, 0))] * n_groups)
        args = list(outs) + list(lses)
        scratch = [pltpu.VMEM((tm, dm), BF16)]
    in_specs += [pl.BlockSpec((dm, d), lambda i: (0, 0)), row_d, vec, vec, vec, vec]
    args += [w, x, gate, g, sc, sh]
    if n_groups > 1:
        in_specs.append(pl.BlockSpec((tm, tm), lambda i: (0, 0)))
        args.append(jnp.asarray(_residue_major_permutation(tm).T, BF16))
    vmem = (2 * n_groups * tm * dm * 2 + 2 * dm * d * 2 + 2 * tm * d * 4 + 2 * tm * d * 4
            + 2 * tm * d * 2 + tm * dm * 2 + tm * d * 4)
    return pl.pallas_call(
        functools.partial(_out_kernel, n_groups=n_groups, n_heads=n_heads),
        name="mixer_out_projection",
        grid=(s // tm,),
        in_specs=in_specs,
        out_specs=[row_d, row_d],
        out_shape=[jax.ShapeDtypeStruct((s, d), F32), jax.ShapeDtypeStruct((s, d), BF16)],
        scratch_shapes=scratch,
        compiler_params=_params(("arbitrary",), vmem),
    )(*args)


FFN_TM = 1024
FFN_TF = 512
FFN_ROW_SPLIT = 2


def _causal_conv(u_ref, row0, rows, cw_ref, cb_ref):
    def at(shift):
        return u_ref[SUBLANES + row0 - shift:SUBLANES + row0 - shift + rows, :]

    y = cb_ref[...] + cw_ref[0:1, :] * at(2)
    y = y + cw_ref[1:2, :] * at(1)
    return y + cw_ref[2:3, :] * at(0)


def _ffn_kernel(h_ref, wa_ref, wg_ref, cwa_ref, cwg_ref, cba_ref, cbg_ref, wd_ref, x_ref, gate_ref,
                fg_ref, o_ref, tail_a_ref, tail_g_ref, ua_ref, ug_ref, *, final_norm):
    i = pl.program_id(0)
    j = pl.program_id(1)
    tm = h_ref.shape[0]

    @pl.when(i == 0)
    def _():
        tail_a_ref[j] = jnp.zeros(tail_a_ref.shape[1:], F32)
        tail_g_ref[j] = jnp.zeros(tail_g_ref.shape[1:], F32)

    @pl.when(j == 0)
    def _():
        o_ref[...] = jnp.zeros_like(o_ref)

    ua_ref[0:SUBLANES, :] = tail_a_ref[j]
    ug_ref[0:SUBLANES, :] = tail_g_ref[j]

    sub = tm // FFN_ROW_SPLIT
    for r in range(FFN_ROW_SPLIT):
        h = h_ref[r * sub:(r + 1) * sub, :]
        rows = slice(SUBLANES + r * sub, SUBLANES + (r + 1) * sub)
        ua_ref[rows, :] = jnp.dot(h, wa_ref[...], preferred_element_type=F32)
        ug_ref[rows, :] = jnp.dot(h, wg_ref[...], preferred_element_type=F32)
    for r in range(FFN_ROW_SPLIT):
        ya = _causal_conv(ua_ref, r * sub, sub, cwa_ref, cba_ref)
        yg = _causal_conv(ug_ref, r * sub, sub, cwg_ref, cbg_ref)
        act = (yg * (1.0 / (1.0 + jnp.exp(-yg))) * ya).astype(BF16)
        o_ref[r * sub:(r + 1) * sub, :] += jnp.dot(act, wd_ref[...], preferred_element_type=F32)
    tail_a_ref[j] = ua_ref[tm:tm + SUBLANES, :]
    tail_g_ref[j] = ug_ref[tm:tm + SUBLANES, :]

    @pl.when(j == pl.num_programs(1) - 1)
    def _():
        def fin(k, carry):
            r = pl.multiple_of(k * NORM_ROWS, NORM_ROWS)
            rows = pl.ds(r, NORM_ROWS)
            xn = x_ref[rows, :] + gate_ref[...] * o_ref[rows, :]
            if final_norm:
                xn = (xn * lax.rsqrt(jnp.mean(xn * xn, axis=-1, keepdims=True) + EPS)) * fg_ref[...]
            o_ref[rows, :] = xn
            return carry
        lax.fori_loop(0, tm // NORM_ROWS, fin, 0, unroll=4)


def _pad_halves(a, d_ff, d_ff_pad):
    pad = [(0, 0)] * (a.ndim - 1) + [(0, d_ff_pad - d_ff)]
    return jnp.concatenate([jnp.pad(a[..., :d_ff], pad), jnp.pad(a[..., d_ff:], pad)], axis=-1)


def _conv_ffn(h, layer, w_up_p, conv_w, conv_b, w_down_p, x, gate, final_g, *, final_norm):
    s, d = x.shape
    d_ff = conv_w.shape[1] // 2
    tm, tf = FFN_TM, FFN_TF
    d_ff_pad = w_down_p.shape[1]
    nj = d_ff_pad // tf
    conv_w_p = _pad_halves(conv_w, d_ff, d_ff_pad)
    conv_b_p = _pad_halves(conv_b.reshape(1, -1), d_ff, d_ff_pad)
    vec = pl.BlockSpec((1, d), lambda i, j: (0, 0))
    vmem = (2 * tm * d * 2 + 2 * 2 * d * tf * 2 + 2 * tf * d * 2 + tm * d * 4 + 2 * tm * d * 4
            + 2 * nj * SUBLANES * tf * 4 + 8 * tm * tf * 4)
    return pl.pallas_call(
        functools.partial(_ffn_kernel, final_norm=final_norm),
        name="conv_ffn",
        grid=(s // tm, nj),
        in_specs=[pl.BlockSpec((tm, d), lambda i, j: (i, 0)),
                  pl.BlockSpec((None, d, tf), lambda i, j: (layer, 0, j)),
                  pl.BlockSpec((None, d, tf), lambda i, j: (layer, 0, nj + j)),
                  pl.BlockSpec((CONV_WIDTH, tf), lambda i, j: (0, j)),
                  pl.BlockSpec((CONV_WIDTH, tf), lambda i, j: (0, nj + j)),
                  pl.BlockSpec((1, tf), lambda i, j: (0, j)),
                  pl.BlockSpec((1, tf), lambda i, j: (0, nj + j)),
                  pl.BlockSpec((None, tf, d), lambda i, j: (layer, j, 0)),
                  pl.BlockSpec((tm, d), lambda i, j: (i, 0), pipeline_mode=pl.Buffered(1)),
                  vec, vec],
        out_specs=pl.BlockSpec((tm, d), lambda i, j: (i, 0)),
        out_shape=jax.ShapeDtypeStruct((s, d), F32),
        scratch_shapes=[pltpu.VMEM((nj, SUBLANES, tf), F32), pltpu.VMEM((nj, SUBLANES, tf), F32),
                        pltpu.VMEM((SUBLANES + tm, tf), F32), pltpu.VMEM((SUBLANES + tm, tf), F32)],
        compiler_params=_params(("arbitrary", "arbitrary"), vmem),
    )(h, w_up_p, w_up_p, conv_w_p, conv_w_p, conv_b_p, conv_b_p, w_down_p, x, gate, final_g.reshape(1, d))


def kernel(x, c, norm_mix_g, norm_ffn_g, ada_w, ada_b, w_in_a, w_out_a, w_in_b, w_out_b,
           w_up, conv_w, conv_b, w_down, final_g):
    batch, s, d = x.shape
    assert batch == 1, "the sequence is processed as one (S, D) slab"
    depth = ada_w.shape[0]
    n_heads = w_out_a.shape[1] // HEAD_DIM
    n_groups = len(DILATED_CONFIGS)

    xs = x.reshape(s, d)
    mod = _ada_modulation(c, ada_w, ada_b)
    d_ff = w_down.shape[1]
    d_ff_pad = pl.cdiv(d_ff, FFN_TF) * FFN_TF
    n_steps = (s // QKV_TM) * (w_in_a.shape[2] // QKV_TN)
    side_casts = [_side_cast_up(w_up, d_ff, d_ff_pad, n_steps), _side_cast_down(w_down, d_ff_pad, n_steps),
                  _side_cast_rows(w_in_b, n_steps), _side_cast_rows(w_out_a, n_steps),
                  _side_cast_rows(w_out_b, n_steps)]
    w_in_a16 = _cast_bf16(w_in_a)
    w_up_p = w_down_p = w_in_b16 = w_out_a16 = w_out_b16 = None
    for i in range(depth):
        sh1, sc1, g1, sh2, sc2, g2 = [mod[i, :, k * d:(k + 1) * d] for k in range(N_MOD)]
        norm_g = norm_mix_g[i].reshape(1, d)
        ffn_g = norm_ffn_g[i].reshape(1, d)
        if i % 2 == 0:
            w_in = w_in_a16[i // 2]
            if i == 0:
                qkv, w_up_p, w_down_p, w_in_b16, w_out_a16, w_out_b16 = _qkv_projection(
                    xs, norm_g, sc1, sh1, w_in, layout="residue_major", side_casts=side_casts)
            else:
                qkv = _qkv_projection(xs, norm_g, sc1, sh1, w_in, layout="residue_major")
            outs, lses = zip(*[_dilated_group_attention(qkv, g, n_heads=n_heads) for g in range(n_groups)])
            w_out = w_out_a16[i // 2]
        else:
            qkv = _qkv_projection(xs, norm_g, sc1, sh1, w_in_b16[i // 2], layout="head_major")
            outs, lses = [_stick_breaking_attention(qkv, n_heads=n_heads)], None
            w_out = w_out_b16[i // 2]
        xs, h2 = _out_projection(outs, lses, w_out, xs, g1, ffn_g, sc2, sh2, n_heads=n_heads)
        xs = _conv_ffn(h2, i, w_up_p, conv_w[i], conv_b[i], w_down_p, xs, g2, final_g,
                       final_norm=(i == depth - 1))
    return xs.reshape(batch, s, d)
```

```python
import functools
import math
from typing import Callable, NamedTuple

import numpy as np
import jax
import jax.numpy as jnp
from jax import lax
from jax.experimental import pallas as pl
from jax.experimental.pallas import tpu as pltpu

HEAD_DIM = 128
DILATED_CONFIGS = ((128, 1), (512, 4), (2048, 16))
ATTN_SPAN_MAX = 128
CONV_WIDTH = 3
EPS = 1e-6
N_MOD = 6
NEG_BIG = -1e30
N_RESIDUES = 16
MID_DILATION = 4

LANES = 128
SUBLANES = 8
BF16_SUBLANES = 16
V7X_VMEM_BYTES = 64 * 1024 * 1024
VMEM_HEADROOM_BYTES = 6 * 1024 * 1024

F32_EXP2_UNDERFLOW = 150.0
LOG2_E = 1.4426950408889634
LN_2 = 0.6931471805599453

F32 = jnp.float32
BF16 = jnp.bfloat16


def _params(semantics, vmem_estimate_bytes):
    limit = min(int(vmem_estimate_bytes) + VMEM_HEADROOM_BYTES, V7X_VMEM_BYTES - VMEM_HEADROOM_BYTES)
    return pltpu.CompilerParams(dimension_semantics=semantics, vmem_limit_bytes=limit)


def _modulated_rms_norm(x, g, sc, sh):
    y = x * lax.rsqrt(jnp.mean(x * x, axis=-1, keepdims=True) + EPS)
    return (y * g) * (1.0 + sc) + sh


ADA_TN = 1024
ADA_ROWS = 256


def _ada_kernel(c_ref, w_ref, b_ref, o_ref):
    d = w_ref.shape[1]
    tn = w_ref.shape[2]

    def body(k, acc):
        r = pl.multiple_of(k * ADA_ROWS, ADA_ROWS)
        prod = c_ref[pl.ds(r, ADA_ROWS), :] * w_ref[0, pl.ds(r, ADA_ROWS), :]
        return acc + prod.reshape(ADA_ROWS // SUBLANES, SUBLANES, tn).sum(axis=0)

    acc = lax.fori_loop(0, d // ADA_ROWS, body, jnp.zeros((SUBLANES, tn), F32))
    o_ref[0] = acc.sum(axis=0, keepdims=True) + b_ref[0]


def _ada_modulation(c, ada_w, ada_b):
    depth, d, n = ada_w.shape
    c_col = c.reshape(d, 1)
    return pl.pallas_call(
        _ada_kernel,
        name="ada_modulation",
        grid=(depth, n // ADA_TN),
        in_specs=[
            pl.BlockSpec((d, 1), lambda l, j: (0, 0)),
            pl.BlockSpec((1, d, ADA_TN), lambda l, j: (l, 0, j)),
            pl.BlockSpec((1, 1, ADA_TN), lambda l, j: (l, 0, j)),
        ],
        out_specs=pl.BlockSpec((1, 1, ADA_TN), lambda l, j: (l, 0, j)),
        out_shape=jax.ShapeDtypeStruct((depth, 1, n), F32),
        compiler_params=_params(("arbitrary", "arbitrary"),
                                2 * d * ADA_TN * 4 + d * LANES * 4),
    )(c_col, ada_w, ada_b.reshape(depth, 1, n))


QKV_TM = 1024
QKV_TN = 1024
NORM_ROWS = 64


def _residue_of_slot(slot):
    per = N_RESIDUES // MID_DILATION
    return MID_DILATION * (slot % per) + slot // per


def _residue_major_permutation(rows):
    rho = np.arange(rows)
    rows_per = rows // N_RESIDUES
    src = N_RESIDUES * (rho % rows_per) + _residue_of_slot(rho // rows_per)
    perm = np.zeros((rows, rows), np.float32)
    perm[rho, src] = 1.0
    return perm


def _side_block_rows(rows, layers, n_steps, granule):
    for r in range(granule, rows + 1, granule):
        if rows % r == 0 and layers * (rows // r) <= n_steps:
            return r
    raise ValueError("side cast does not fit in the host call's grid")


class _SideCast(NamedTuple):
    src: jax.Array
    in_block: tuple
    in_index: Callable
    out_shape: tuple
    out_block: tuple
    out_index: Callable
    n_blocks: int
    body: Callable


def _side_cast_rows(w, n_steps):
    layers, rows, cols = w.shape
    block_rows = _side_block_rows(rows, layers, n_steps, BF16_SUBLANES)
    per = rows // block_rows

    def index(b):
        return (b // per, b % per, 0)

    def body(src_ref, dst_ref, b):
        dst_ref[...] = src_ref[...].astype(BF16)
    block = (1, block_rows, cols)
    return _SideCast(w, block, index, w.shape, block, index, layers * per, body)


def _side_cast_up(w_up, d_ff, d_ff_pad, n_steps):
    layers, d, _ = w_up.shape
    assert d_ff % LANES == 0 and d_ff_pad % LANES == 0
    block_rows = _side_block_rows(d, layers, n_steps, BF16_SUBLANES)
    per = d // block_rows

    def index(b):
        return (b // per, b % per, 0)

    def body(src_ref, dst_ref, b):
        for half in range(2):
            dst_ref[0, :, half * d_ff_pad:half * d_ff_pad + d_ff] = (
                src_ref[0, :, half * d_ff:(half + 1) * d_ff].astype(BF16))
            if d_ff_pad > d_ff:
                dst_ref[0, :, half * d_ff_pad + d_ff:(half + 1) * d_ff_pad] = (
                    jnp.zeros((block_rows, d_ff_pad - d_ff), BF16))
    return _SideCast(w_up, (1, block_rows, 2 * d_ff), index, (layers, d, 2 * d_ff_pad),
                     (1, block_rows, 2 * d_ff_pad), index, layers * per, body)


def _side_cast_down(w_down, d_ff_pad, n_steps):
    layers, d_ff, d = w_down.shape
    common = math.gcd(d_ff, d_ff_pad)
    rows = _side_block_rows(common, layers * (d_ff_pad // common), n_steps, BF16_SUBLANES)
    src_per, per = d_ff // rows, d_ff_pad // rows

    def in_index(b):
        return (b // per, jnp.minimum(b % per, src_per - 1), 0)

    def out_index(b):
        return (b // per, b % per, 0)

    def body(src_ref, dst_ref, b):
        dst_ref[...] = jnp.where(b % per < src_per, src_ref[...], 0.0).astype(BF16)
    return _SideCast(w_down, (1, rows, d), in_index, (layers, d_ff_pad, d), (1, rows, d), out_index,
                     layers * per, body)


CAST_ROWS = 128


def _cast_kernel(w_ref, o_ref):
    o_ref[...] = w_ref[...].astype(BF16)


def _cast_bf16(w):
    layers, rows, cols = w.shape
    assert rows % CAST_ROWS == 0
    block = (1, CAST_ROWS, cols)
    return pl.pallas_call(
        _cast_kernel,
        name="cast_weight",
        grid=(layers, rows // CAST_ROWS),
        in_specs=[pl.BlockSpec(block, lambda l, i: (l, i, 0))],
        out_specs=pl.BlockSpec(block, lambda l, i: (l, i, 0)),
        out_shape=jax.ShapeDtypeStruct(w.shape, BF16),
        compiler_params=_params(("arbitrary", "arbitrary"), 2 * CAST_ROWS * cols * (4 + 2)),
    )(w)


def _qkv_kernel(*refs, layout, side_casts):
    n_side = len(side_casts)
    n_in = 6 if layout == "residue_major" else 5
    x_ref, g_ref, sc_ref, sh_ref, w_ref = refs[:5]
    perm_ref = refs[5] if layout == "residue_major" else None
    side_in = refs[n_in:n_in + n_side]
    o_ref = refs[n_in + n_side]
    side_out = refs[n_in + n_side + 1:n_in + 2 * n_side + 1]
    h_ref = refs[-1]
    tm = x_ref.shape[0]

    @pl.when(pl.program_id(1) == 0)
    def _():
        def body(k, carry):
            r = pl.multiple_of(k * NORM_ROWS, NORM_ROWS)
            h = _modulated_rms_norm(x_ref[pl.ds(r, NORM_ROWS), :], g_ref[...], sc_ref[...], sh_ref[...])
            h_ref[pl.ds(r, NORM_ROWS), :] = h.astype(BF16)
            return carry
        lax.fori_loop(0, tm // NORM_ROWS, body, 0, unroll=4)
        if layout == "residue_major":
            h_ref[...] = jnp.dot(perm_ref[...], h_ref[...], preferred_element_type=F32).astype(BF16)

    step = pl.program_id(0) * pl.num_programs(1) + pl.program_id(1)
    for (body, n_blocks), src_ref, dst_ref in zip(side_casts, side_in, side_out):
        body(src_ref, dst_ref, jnp.minimum(step, n_blocks - 1))

    res = jnp.dot(h_ref[...], w_ref[...], preferred_element_type=F32)
    if layout == "head_major":
        for cb in range(o_ref.shape[0]):
            o_ref[cb] = res[:, cb * HEAD_DIM:(cb + 1) * HEAD_DIM].astype(BF16)
    else:
        o_ref[...] = res.reshape(o_ref.shape).astype(BF16)


def _qkv_projection(x, g, sc, sh, w, *, layout, side_casts=()):
    s, d = x.shape
    n = w.shape[1]
    tm, tn = QKV_TM, QKV_TN
    vec = pl.BlockSpec((1, d), lambda i, j: (0, 0))
    in_specs = [pl.BlockSpec((tm, d), lambda i, j: (i, 0)), vec, vec, vec,
                pl.BlockSpec((d, tn), lambda i, j: (0, j))]
    args = [x, g, sc, sh, w]
    vmem = 2 * tm * d * 4 + tm * d * 2 + 2 * d * tn * 2 + 2 * tm * tn * 2 + tm * tn * 4
    if layout == "head_major":
        out_shape = jax.ShapeDtypeStruct((n // HEAD_DIM, s, HEAD_DIM), BF16)
        out_spec = pl.BlockSpec((tn // HEAD_DIM, tm, HEAD_DIM), lambda i, j: (j, i, 0))
    else:
        assert layout == "residue_major"
        out_shape = jax.ShapeDtypeStruct((N_RESIDUES, s // N_RESIDUES, n), BF16)
        out_spec = pl.BlockSpec((N_RESIDUES, tm // N_RESIDUES, tn), lambda i, j: (0, i, j))
        in_specs.append(pl.BlockSpec((tm, tm), lambda i, j: (0, 0)))
        args.append(jnp.asarray(_residue_major_permutation(tm), BF16))
        vmem += 2 * tm * tm * 2 + tm * d * (4 + 2)
    nj = n // tn
    n_steps = (s // tm) * nj
    out_specs, out_shapes = [out_spec], [out_shape]
    for cast in side_casts:
        assert cast.n_blocks <= n_steps

        def block_of(i, j, cast=cast):
            return jnp.minimum(i * nj + j, cast.n_blocks - 1)
        in_specs.append(pl.BlockSpec(cast.in_block, lambda i, j, c=cast, b=block_of: c.in_index(b(i, j))))
        out_specs.append(pl.BlockSpec(cast.out_block, lambda i, j, c=cast, b=block_of: c.out_index(b(i, j))))
        out_shapes.append(jax.ShapeDtypeStruct(cast.out_shape, BF16))
        args.append(cast.src)
        vmem += 2 * (int(np.prod(cast.in_block)) * 4 + int(np.prod(cast.out_block)) * 2)
    outs = pl.pallas_call(
        functools.partial(_qkv_kernel, layout=layout,
                          side_casts=tuple((c.body, c.n_blocks) for c in side_casts)),
        name="qkv_projection",
        grid=(s // tm, nj),
        in_specs=in_specs,
        out_specs=out_specs,
        out_shape=out_shapes,
        scratch_shapes=[pltpu.VMEM((tm, d), BF16)],
        compiler_params=_params(("arbitrary", "arbitrary"), vmem),
    )(*args)
    return outs[0] if not side_casts else outs


def _alibi_slopes(n):
    return [float(np.float32(2.0 ** (-8.0 * (i + 1) / n))) for i in range(n)]


def _dilated_kernel(q_ref, kc_ref, vc_ref, bias_ref, o_ref, lse_ref, kp_ref, vp_ref, *, n_heads):
    slots, rows_per, _ = q_ref.shape
    rows = slots * rows_per
    first_block = (pl.program_id(1) == 0).astype(jnp.int32)

    @pl.when(pl.program_id(1) == 0)
    def _():
        kp_ref[...] = jnp.zeros_like(kp_ref)
        vp_ref[...] = jnp.zeros_like(vp_ref)
    lane = lax.broadcasted_iota(jnp.int32, (rows, LANES), 1)
    to_log2 = HEAD_DIM ** -0.5 * LOG2_E
    contract_last = (((1,), (1,)), ((), ()))

    def head(ref, h):
        return ref[:, :, h * HEAD_DIM:(h + 1) * HEAD_DIM].reshape(rows, HEAD_DIM)

    scores = []
    for h in range(n_heads):
        keys = jnp.concatenate([head(kp_ref, h), head(kc_ref, h)], axis=0)
        scores.append(lax.dot_general(head(q_ref, h), keys, contract_last, preferred_element_type=F32))

    max_tile = jnp.zeros((rows, LANES), F32)
    den_tile = jnp.ones((rows, LANES), F32)
    for h in range(n_heads):
        s = scores[h] * to_log2 + bias_ref[first_block, h]
        m = jnp.max(s, axis=-1, keepdims=True)
        p = jnp.exp2(s - m)
        denom = jnp.sum(p, axis=-1, keepdims=True)
        values = jnp.concatenate([head(vp_ref, h), head(vc_ref, h)], axis=0)
        acc = jnp.dot(p.astype(BF16), values, preferred_element_type=F32)
        o_ref[:, :, h * HEAD_DIM:(h + 1) * HEAD_DIM] = (
            (acc / denom).reshape(slots, rows_per, HEAD_DIM).astype(o_ref.dtype))
        max_tile = jnp.where(lane == h, m, max_tile)
        den_tile = jnp.where(lane == h, denom, den_tile)
    lse_ref[...] = ((max_tile + jnp.log2(den_tile)) * LN_2).reshape(lse_ref.shape)
    kp_ref[...] = kc_ref[...]
    vp_ref[...] = vc_ref[...]


def _alibi_bias_tables(dilation, span, slots, rows_per, n_heads):
    rows = slots * rows_per
    rho = np.arange(rows)
    pos = N_RESIDUES * (rho % rows_per) + _residue_of_slot(rho // rows_per)
    m = pos // dilation
    j_cur = m[:, None] - m[None, :]
    j = np.concatenate([j_cur + rows, j_cur], axis=1)
    valid = (j >= 0) & (j <= span)
    valid = jnp.asarray(np.stack([valid, valid & (np.arange(2 * rows) >= rows)[None, :]]))
    dist = jnp.asarray((j * dilation).astype(np.float32))
    slopes = jnp.asarray(_alibi_slopes(n_heads), F32)
    bias = -(slopes[:, None, None] * dist[None]) * LOG2_E
    return jnp.where(valid[:, None], bias[None], NEG_BIG)


def _dilated_group_attention(qkv, group, *, n_heads):
    window, dilation = DILATED_CONFIGS[group]
    n_res, s_per, _ = qkv.shape
    dm = n_heads * HEAD_DIM
    span = window // dilation
    assert n_res == N_RESIDUES and N_RESIDUES % dilation == 0 and n_heads <= LANES
    assert window % dilation == 0 and span <= ATTN_SPAN_MAX
    slots = N_RESIDUES // dilation
    rows_per = max(ATTN_SPAN_MAX // slots, BF16_SUBLANES)
    rows = slots * rows_per
    assert s_per % rows_per == 0 and rows >= span
    bias = _alibi_bias_tables(dilation, span, slots, rows_per, n_heads)
    base = group * 3

    def spec(which):
        return pl.BlockSpec((slots, rows_per, dm), lambda p, b: (p, b, base + which))

    return pl.pallas_call(
        functools.partial(_dilated_kernel, n_heads=n_heads),
        name=f"dilated_attention_g{group}",
        grid=(N_RESIDUES // slots, s_per // rows_per),
        in_specs=[spec(0), spec(1), spec(2),
                  pl.BlockSpec((2, n_heads, rows, 2 * rows), lambda p, b: (0, 0, 0, 0),
                               pipeline_mode=pl.Buffered(1))],
        out_specs=[pl.BlockSpec((slots, rows_per, dm), lambda p, b: (p, b, 0)),
                   pl.BlockSpec((slots, rows_per, LANES), lambda p, b: (p, b, 0))],
        out_shape=[jax.ShapeDtypeStruct((N_RESIDUES, s_per, dm), BF16),
                   jax.ShapeDtypeStruct((N_RESIDUES, s_per, LANES), F32)],
        scratch_shapes=[pltpu.VMEM((slots, rows_per, dm), BF16), pltpu.VMEM((slots, rows_per, dm), BF16)],
        compiler_params=_params(("arbitrary", "arbitrary"),
                                (2 * 4 + 2) * rows * dm * 2 + 2 * rows * LANES * 4
                                + (2 + 4 * n_heads) * rows * 2 * rows * 4),
    )(qkv, qkv, qkv, bias)


SB_TQ = 1024
SB_SUB = 128
SB_BLK = 256
SB_NEAR = 512
SB_GROUP = 4
SB_HEADS_PER_STEP = 2


def _sb_kernel(q_ref, k_ref, v_ref, tri2_ref, o_ref, acc_ref, c_ref):
    for hh in range(q_ref.shape[0]):
        _sb_head(q_ref, k_ref, v_ref, tri2_ref, o_ref, acc_ref, c_ref, hh)


def _sb_head(q_ref, k_ref, v_ref, tri2_ref, o_ref, acc_ref, c_ref, hh):
    tq = q_ref.shape[1]
    sub, blk, near = SB_SUB, SB_BLK, SB_NEAR
    n_sub = tq // sub
    i = pl.program_id(1)
    to_log2 = HEAD_DIM ** -0.5 * LOG2_E
    contract_last = (((1,), (1,)), ((), ()))

    def masked_scores(r, kstart, col, limit):
        q = q_ref[hh, r * sub:(r + 1) * sub, :]
        k = k_ref[hh, pl.ds(kstart, col.shape[1]), :]
        y = lax.dot_general(q, k, contract_last, preferred_element_type=F32) * to_log2
        return jnp.where(col < limit, y, NEG_BIG)

    def log_terms(y):
        log_beta = jnp.minimum(y, 0.0) - jnp.log2(1.0 + jnp.exp2(-jnp.abs(y)))
        log_keep = log_beta - y
        return log_beta, log_keep

    def staged_log_terms(ys):
        n_blk = ys[0].shape[1] // blk
        stage = []
        for g in range(0, len(ys), SB_GROUP):
            terms, lhs = [], []
            for y in ys[g:g + SB_GROUP]:
                log_beta, log_keep = log_terms(y)
                hi = log_keep.astype(BF16)
                lo = (log_keep - hi.astype(F32)).astype(BF16)
                lhs += [jnp.concatenate([hi[:, b * blk:(b + 1) * blk], lo[:, b * blk:(b + 1) * blk]], axis=1)
                        for b in range(n_blk)]
                terms.append((log_beta, log_keep))
            local = jnp.dot(jnp.concatenate(lhs, axis=0), tri2_ref[...], preferred_element_type=F32)
            rows = n_blk * sub
            stage += [(lb, lk, local[k * rows:(k + 1) * rows, :]) for k, (lb, lk) in enumerate(terms)]
        return stage

    n_blk = near // blk
    col_minus_row = (lax.broadcasted_iota(jnp.int32, (sub, near), 1)
                     - lax.broadcasted_iota(jnp.int32, (sub, near), 0))
    kstarts, ys = [], []
    for r in range(n_sub):
        q_start = i * tq + r * sub
        kstart = pl.multiple_of(jnp.maximum(q_start + sub - near, 0), sub)
        kstarts.append(kstart)
        ys.append(masked_scores(r, kstart, col_minus_row, q_start - kstart))
    stage = staged_log_terms(ys)
    for r in range(n_sub):
        log_beta, log_keep, local = stage[r]
        newer = jnp.zeros((sub, 1), F32)
        after = [None] * n_blk
        for b in reversed(range(n_blk)):
            loc = local[b * sub:(b + 1) * sub, :]
            after[b] = loc + newer if b < n_blk - 1 else loc
            newer = newer + (loc[:, 0:1] + log_keep[:, b * blk:b * blk + 1])
        a = jnp.exp2(log_beta + jnp.concatenate(after, axis=1))
        v = v_ref[hh, pl.ds(kstarts[r], near), :]
        acc_ref[r * sub:(r + 1) * sub, :] = jnp.dot(a.astype(BF16), v, preferred_element_type=F32)
        c_ref[r * sub:(r + 1) * sub, :] = newer

    def cond(carry):
        m, c_max = carry
        return (kstarts[n_sub - 1] - m * blk > 0) & (c_max > -F32_EXP2_UNDERFLOW)

    def body(carry):
        m, _ = carry
        col = lax.broadcasted_iota(jnp.int32, (sub, blk), 1)
        wins, ys = [], []
        for r in range(n_sub):
            kend = kstarts[r] - m * blk
            wstart = pl.multiple_of(jnp.maximum(kend - blk, 0), sub)
            wins.append(wstart)
            ys.append(masked_scores(r, wstart, col, kend - wstart))
        stage = staged_log_terms(ys)
        c_old = c_ref[...]
        outs, c_news = [], []
        for r in range(n_sub):
            log_beta, log_keep, local = stage[r]
            after = local + c_old[r * sub:(r + 1) * sub, :]
            a = jnp.exp2(log_beta + after)
            v = v_ref[hh, pl.ds(wins[r], blk), :]
            outs.append(jnp.dot(a.astype(BF16), v, preferred_element_type=F32))
            c_news.append(after[:, 0:1] + log_keep[:, 0:1])
        c_new = jnp.concatenate(c_news, axis=0)
        acc_ref[...] += jnp.concatenate(outs, axis=0)
        c_ref[...] = c_new
        return m + 1, jnp.max(c_new)

    lax.while_loop(cond, body, (0, jnp.max(c_ref[...])))
    o_ref[:, hh * HEAD_DIM:(hh + 1) * HEAD_DIM] = acc_ref[...].astype(o_ref.dtype)


def _stick_breaking_attention(qkv, *, n_heads):
    _, s, _ = qkv.shape
    tq, sub, blk, near = SB_TQ, SB_SUB, SB_BLK, SB_NEAR
    assert s % tq == 0 and tq % sub == 0 and near % blk == 0 and blk % sub == 0 and s >= near
    hps = SB_HEADS_PER_STEP
    assert n_heads % hps == 0
    tri = np.tril(np.ones((blk, blk), np.float32), k=-1)
    tri2 = jnp.asarray(np.concatenate([tri, tri], axis=0), BF16)
    vmem = (hps * 2 * 2 * s * HEAD_DIM * 2 + 2 * 2 * blk * blk * 2 + hps * 4 * tq * HEAD_DIM * 2
            + tq * LANES * 4 * 2 + 10 * tq * near * 4)
    return pl.pallas_call(
        _sb_kernel,
        name="stick_breaking_attention",
        grid=(n_heads // hps, s // tq),
        in_specs=[pl.BlockSpec((hps, tq, HEAD_DIM), lambda h, i: (h, i, 0)),
                  pl.BlockSpec((hps, s, HEAD_DIM), lambda h, i: (n_heads // hps + h, 0, 0)),
                  pl.BlockSpec((hps, s, HEAD_DIM), lambda h, i: (2 * (n_heads // hps) + h, 0, 0)),
                  pl.BlockSpec((2 * blk, blk), lambda h, i: (0, 0))],
        out_specs=pl.BlockSpec((tq, hps * HEAD_DIM), lambda h, i: (i, h)),
        out_shape=jax.ShapeDtypeStruct((s, n_heads * HEAD_DIM), BF16),
        scratch_shapes=[pltpu.VMEM((tq, HEAD_DIM), F32), pltpu.VMEM((tq, 1), F32)],
        compiler_params=_params(("arbitrary", "arbitrary"), vmem),
    )(qkv, qkv, qkv, tri2)


OUT_TM = 512
OUT_HEAD_GROUPS = 4


def _out_kernel(*refs, n_groups, n_heads):
    o_refs = refs[:n_groups]
    lse_refs = refs[n_groups:2 * n_groups] if n_groups > 1 else ()
    rest = refs[len(o_refs) + len(lse_refs):]
    w_ref, x_ref, gate_ref, g_ref, sc_ref, sh_ref = rest[:6]
    if n_groups == 1:
        xo_ref, h_ref = rest[6:]
    else:
        unperm_ref, xo_ref, h_ref, merged_ref = rest[6:]
    tm = x_ref.shape[0]

    def finish(row0, y):
        for k in range(y.shape[0] // NORM_ROWS):
            rows = slice(row0 + k * NORM_ROWS, row0 + (k + 1) * NORM_ROWS)
            xn = x_ref[rows, :] + gate_ref[...] * y[k * NORM_ROWS:(k + 1) * NORM_ROWS]
            xo_ref[rows, :] = xn
            h_ref[rows, :] = _modulated_rms_norm(xn, g_ref[...], sc_ref[...], sh_ref[...]).astype(BF16)

    if n_groups == 1:
        half = tm // 2
        ys = [jnp.dot(o_refs[0][r * half:(r + 1) * half, :], w_ref[...], preferred_element_type=F32)
              for r in range(2)]
        for r in range(2):
            finish(r * half, ys[r])
    else:
        rows_per = tm // N_RESIDUES
        wts = []
        for k in range(N_RESIDUES):
            lses = [lr[k] for lr in lse_refs]
            m = functools.reduce(jnp.maximum, lses)
            es = [jnp.exp(l - m) for l in lses]
            inv = 1.0 / functools.reduce(jnp.add, es)
            wts.append([e * inv for e in es[1:]])
        heads_per = n_heads // OUT_HEAD_GROUPS
        y = None
        for c in range(OUT_HEAD_GROUPS):
            for k in range(N_RESIDUES):
                rows = slice(k * rows_per, (k + 1) * rows_per)
                for h in range(c * heads_per, (c + 1) * heads_per):
                    cols = slice(h * HEAD_DIM, (h + 1) * HEAD_DIM)
                    base = o_refs[0][k, :, cols].astype(F32)
                    acc = base
                    for g in range(1, n_groups):
                        acc = acc + wts[k][g - 1][:, h:h + 1] * (o_refs[g][k, :, cols].astype(F32) - base)
                    merged_ref[rows, cols] = acc.astype(BF16)
            gcols = slice(c * heads_per * HEAD_DIM, (c + 1) * heads_per * HEAD_DIM)
            mixed = jnp.dot(unperm_ref[...], merged_ref[:, gcols], preferred_element_type=F32).astype(BF16)
            part = jnp.dot(mixed, w_ref[gcols, :], preferred_element_type=F32)
            y = part if y is None else y + part
        finish(0, y)


def _out_projection(outs, lses, w, x, gate, g, sc, sh, *, n_heads):
    s, d = x.shape
    dm = w.shape[0]
    n_groups = len(outs)
    tm = OUT_TM
    row_d = pl.BlockSpec((tm, d), lambda i: (i, 0))
    vec = pl.BlockSpec((1, d), lambda i: (0, 0))
    scratch = []
    if n_groups == 1:
        in_specs = [pl.BlockSpec((tm, dm), lambda i: (i, 0))]
        args = list(outs)
    else:
        rows_per = tm // N_RESIDUES
        in_specs = ([pl.BlockSpec((N_RESIDUES, rows_per, dm), lambda i: (0, i, 0))] * n_groups
                    + [pl.BlockSpec((N_RESIDUES, rows_per, LANES), lambda i: (0, i, 0))] * n_groups)
        args = list(outs) + list(lses)
        scratch = [pltpu.VMEM((tm, dm), BF16)]
    in_specs += [pl.BlockSpec((dm, d), lambda i: (0, 0)), row_d, vec, vec, vec, vec]
    args += [w, x, gate, g, sc, sh]
    if n_groups > 1:
        in_specs.append(pl.BlockSpec((tm, tm), lambda i: (0, 0)))
        args.append(jnp.asarray(_residue_major_permutation(tm).T, BF16))
    vmem = (2 * n_groups * tm * dm * 2 + 2 * dm * d * 2 + 2 * tm * d * 4 + 2 * tm * d * 4
            + 2 * tm * d * 2 + tm * dm * 2 + tm * d * 4)
    return pl.pallas_call(
        functools.partial(_out_kernel, n_groups=n_groups, n_heads=n_heads),
        name="mixer_out_projection",
        grid=(s // tm,),
        in_specs=in_specs,
        out_specs=[row_d, row_d],
        out_shape=[jax.ShapeDtypeStruct((s, d), F32), jax.ShapeDtypeStruct((s, d), BF16)],
        scratch_shapes=scratch,
        compiler_params=_params(("arbitrary",), vmem),
    )(*args)


FFN_TM = 1024
FFN_TF = 512
FFN_ROW_SPLIT = 2


def _causal_conv(u_ref, row0, rows, cw_ref, cb_ref):
    def at(shift):
        return u_ref[SUBLANES + row0 - shift:SUBLANES + row0 - shift + rows, :]

    y = cb_ref[...] + cw_ref[0:1, :] * at(2)
    y = y + cw_ref[1:2, :] * at(1)
    return y + cw_ref[2:3, :] * at(0)


def _ffn_kernel(h_ref, wa_ref, wg_ref, cwa_ref, cwg_ref, cba_ref, cbg_ref, wd_ref, x_ref, gate_ref,
                fg_ref, o_ref, tail_a_ref, tail_g_ref, ua_ref, ug_ref, *, final_norm):
    i = pl.program_id(0)
    j = pl.program_id(1)
    tm = h_ref.shape[0]

    @pl.when(i == 0)
    def _():
        tail_a_ref[j] = jnp.zeros(tail_a_ref.shape[1:], F32)
        tail_g_ref[j] = jnp.zeros(tail_g_ref.shape[1:], F32)

    @pl.when(j == 0)
    def _():
        o_ref[...] = jnp.zeros_like(o_ref)

    ua_ref[0:SUBLANES, :] = tail_a_ref[j]
    ug_ref[0:SUBLANES, :] = tail_g_ref[j]

    sub = tm // FFN_ROW_SPLIT
    for r in range(FFN_ROW_SPLIT):
        h = h_ref[r * sub:(r + 1) * sub, :]
        rows = slice(SUBLANES + r * sub, SUBLANES + (r + 1) * sub)
        ua_ref[rows, :] = jnp.dot(h, wa_ref[...], preferred_element_type=F32)
        ug_ref[rows, :] = jnp.dot(h, wg_ref[...], preferred_element_type=F32)
    for r in range(FFN_ROW_SPLIT):
        ya = _causal_conv(ua_ref, r * sub, sub, cwa_ref, cba_ref)
        yg = _causal_conv(ug_ref, r * sub, sub, cwg_ref, cbg_ref)
        act = (yg * (1.0 / (1.0 + jnp.exp(-yg))) * ya).astype(BF16)
        o_ref[r * sub:(r + 1) * sub, :] += jnp.dot(act, wd_ref[...], preferred_element_type=F32)
    tail_a_ref[j] = ua_ref[tm:tm + SUBLANES, :]
    tail_g_ref[j] = ug_ref[tm:tm + SUBLANES, :]

    @pl.when(j == pl.num_programs(1) - 1)
    def _():
        def fin(k, carry):
            r = pl.multiple_of(k * NORM_ROWS, NORM_ROWS)
            rows = pl.ds(r, NORM_ROWS)
            xn = x_ref[rows, :] + gate_ref[...] * o_ref[rows, :]
            if final_norm:
                xn = (xn * lax.rsqrt(jnp.mean(xn * xn, axis=-1, keepdims=True) + EPS)) * fg_ref[...]
            o_ref[rows, :] = xn
            return carry
        lax.fori_loop(0, tm // NORM_ROWS, fin, 0, unroll=4)


def _pad_halves(a, d_ff, d_ff_pad):
    pad = [(0, 0)] * (a.ndim - 1) + [(0, d_ff_pad - d_ff)]
    return jnp.concatenate([jnp.pad(a[..., :d_ff], pad), jnp.pad(a[..., d_ff:], pad)], axis=-1)


def _conv_ffn(h, layer, w_up_p, conv_w, conv_b, w_down_p, x, gate, final_g, *, final_norm):
    s, d = x.shape
    d_ff = conv_w.shape[1] // 2
    tm, tf = FFN_TM, FFN_TF
    d_ff_pad = w_down_p.shape[1]
    nj = d_ff_pad // tf
    conv_w_p = _pad_halves(conv_w, d_ff, d_ff_pad)
    conv_b_p = _pad_halves(conv_b.reshape(1, -1), d_ff, d_ff_pad)
    vec = pl.BlockSpec((1, d), lambda i, j: (0, 0))
    vmem = (2 * tm * d * 2 + 2 * 2 * d * tf * 2 + 2 * tf * d * 2 + tm * d * 4 + 2 * tm * d * 4
            + 2 * nj * SUBLANES * tf * 4 + 8 * tm * tf * 4)
    return pl.pallas_call(
        functools.partial(_ffn_kernel, final_norm=final_norm),
        name="conv_ffn",
        grid=(s // tm, nj),
        in_specs=[pl.BlockSpec((tm, d), lambda i, j: (i, 0)),
                  pl.BlockSpec((None, d, tf), lambda i, j: (layer, 0, j)),
                  pl.BlockSpec((None, d, tf), lambda i, j: (layer, 0, nj + j)),
                  pl.BlockSpec((CONV_WIDTH, tf), lambda i, j: (0, j)),
                  pl.BlockSpec((CONV_WIDTH, tf), lambda i, j: (0, nj + j)),
                  pl.BlockSpec((1, tf), lambda i, j: (0, j)),
                  pl.BlockSpec((1, tf), lambda i, j: (0, nj + j)),
                  pl.BlockSpec((None, tf, d), lambda i, j: (layer, j, 0)),
                  pl.BlockSpec((tm, d), lambda i, j: (i, 0), pipeline_mode=pl.Buffered(1)),
                  vec, vec],
        out_specs=pl.BlockSpec((tm, d), lambda i, j: (i, 0)),
        out_shape=jax.ShapeDtypeStruct((s, d), F32),
        scratch_shapes=[pltpu.VMEM((nj, SUBLANES, tf), F32), pltpu.VMEM((nj, SUBLANES, tf), F32),
                        pltpu.VMEM((SUBLANES + tm, tf), F32), pltpu.VMEM((SUBLANES + tm, tf), F32)],
        compiler_params=_params(("arbitrary", "arbitrary"), vmem),
    )(h, w_up_p, w_up_p, conv_w_p, conv_w_p, conv_b_p, conv_b_p, w_down_p, x, gate, final_g.reshape(1, d))


def kernel(x, c, norm_mix_g, norm_ffn_g, ada_w, ada_b, w_in_a, w_out_a, w_in_b, w_out_b,
           w_up, conv_w, conv_b, w_down, final_g):
    batch, s, d = x.shape
    assert batch == 1, "the sequence is processed as one (S, D) slab"
    depth = ada_w.shape[0]
    n_heads = w_out_a.shape[1] // HEAD_DIM
    n_groups = len(DILATED_CONFIGS)

    xs = x.reshape(s, d)
    mod = _ada_modulation(c, ada_w, ada_b)
    d_ff = w_down.shape[1]
    d_ff_pad = pl.cdiv(d_ff, FFN_TF) * FFN_TF
    n_steps = (s // QKV_TM) * (w_in_a.shape[2] // QKV_TN)
    side_casts = [_side_cast_up(w_up, d_ff, d_ff_pad, n_steps), _side_cast_down(w_down, d_ff_pad, n_steps),
                  _side_cast_rows(w_in_b, n_steps), _side_cast_rows(w_out_a, n_steps),
                  _side_cast_rows(w_out_b, n_steps)]
    w_in_a16 = _cast_bf16(w_in_a)
    w_up_p = w_down_p = w_in_b16 = w_out_a16 = w_out_b16 = None
    for i in range(depth):
        sh1, sc1, g1, sh2, sc2, g2 = [mod[i, :, k * d:(k + 1) * d] for k in range(N_MOD)]
        norm_g = norm_mix_g[i].reshape(1, d)
        ffn_g = norm_ffn_g[i].reshape(1, d)
        if i % 2 == 0:
            w_in = w_in_a16[i // 2]
            if i == 0:
                qkv, w_up_p, w_down_p, w_in_b16, w_out_a16, w_out_b16 = _qkv_projection(
                    xs, norm_g, sc1, sh1, w_in, layout="residue_major", side_casts=side_casts)
            else:
                qkv = _qkv_projection(xs, norm_g, sc1, sh1, w_in, layout="residue_major")
            outs, lses = zip(*[_dilated_group_attention(qkv, g, n_heads=n_heads) for g in range(n_groups)])
            w_out = w_out_a16[i // 2]
        else:
            qkv = _qkv_projection(xs, norm_g, sc1, sh1, w_in_b16[i // 2], layout="head_major")
            outs, lses = [_stick_breaking_attention(qkv, n_heads=n_heads)], None
            w_out = w_out_b16[i // 2]
        xs, h2 = _out_projection(outs, lses, w_out, xs, g1, ffn_g, sc2, sh2, n_heads=n_heads)
        xs = _conv_ffn(h2, i, w_up_p, conv_w[i], conv_b[i], w_down_p, xs, g2, final_g,
                       final_norm=(i == depth - 1))
    return xs.reshape(batch, s, d)
```
